```python
import math
import jax
import jax.numpy as jnp
from jax import lax
import numpy as np

D_MODEL = 4096
BATCH = 4
SEQ = 2048
DEPTH = 4

N_MIXERS = 4
N_HEADS = 32
HEAD_DIM = 128
ATT_WIDTH = N_HEADS * HEAD_DIM
Q_LORA = 1024
KV_LORA = 512
IDX_HEADS = 64
IDX_DIM = 128
TOPK_MAX = 256
Q_BLOCK = 128
ATT_IN = Q_LORA + KV_LORA + IDX_DIM + IDX_HEADS
REL_BUCKETS = 32
REL_MAX_DIST = 128
SCONV_WIDTH = 3
CONF_WIDTH = 31
LRU_WIDTH = D_MODEL
LRU_HEADS = 16
LRU_BLOCK = LRU_WIDTH // LRU_HEADS
LRU_CONV_WIDTH = 4
LRU_C = 8.0
D_FF = 2 * D_MODEL
N_EXPERTS = 8
MOE_TOPK = 2
D_FF_EXPERT = D_MODEL // 2
N_ADA = 6
NORM_EPS = 1e-6

N_ATT = (DEPTH + 3) // 4
N_SCONV = (DEPTH + 2) // 4
N_CONF = (DEPTH + 1) // 4
N_LRU = DEPTH // 4
N_DENSE = (DEPTH + 1) // 2
N_MOE = DEPTH // 2

kernel_name = "hybrid_dsa_conv_conformer_rglru_moe_trunk"


def rms_norm(x, g):
    xf = x.astype(jnp.float32)
    y = xf * lax.rsqrt(jnp.mean(xf * xf, axis=-1, keepdims=True) + NORM_EPS)
    return (y * g.astype(jnp.float32)).astype(x.dtype)


def layer_norm(x, g, b):
    xf = x.astype(jnp.float32)
    mu = jnp.mean(xf, axis=-1, keepdims=True)
    var = jnp.mean(jnp.square(xf - mu), axis=-1, keepdims=True)
    y = (xf - mu) * lax.rsqrt(var + NORM_EPS)
    return (y * g.astype(jnp.float32) + b.astype(jnp.float32)).astype(x.dtype)


def causal_depthwise_conv(x, conv_w):
    width, ch = conv_w.shape
    return lax.conv_general_dilated(
        x, conv_w[:, None, :].astype(x.dtype), window_strides=(1,),
        padding=[(width - 1, 0)], dimension_numbers=("NWC", "WIO", "NWC"),
        feature_group_count=ch)


def causal_rel_bucket(dist):
    dist = jnp.maximum(dist, 0)
    max_exact = REL_BUCKETS // 2
    large = max_exact + (jnp.log(jnp.maximum(dist, 1).astype(jnp.float32) / max_exact)
                         / math.log(REL_MAX_DIST / max_exact)
                         * (REL_BUCKETS - max_exact)).astype(jnp.int32)
    large = jnp.minimum(large, REL_BUCKETS - 1)
    return jnp.where(dist < max_exact, dist, large)


def dsa_attention(h, rel_bias, w_in, g_cq, g_ckv, w_uq, w_uk, w_uv, w_qidx, w_out):
    bsz, seq, _ = h.shape
    k_sel = min(TOPK_MAX, seq // 4)
    n_blk = seq // Q_BLOCK
    proj = h @ w_in
    c_q, c_kv, k_idx, w_head = jnp.split(
        proj, [Q_LORA, Q_LORA + KV_LORA, Q_LORA + KV_LORA + IDX_DIM], axis=-1)
    c_q = rms_norm(c_q, g_cq)
    c_kv = rms_norm(c_kv, g_ckv)
    w_head = w_head * (IDX_HEADS ** -0.5)
    key_pos = jnp.arange(seq, dtype=jnp.int32)
    q_pos = key_pos.reshape(n_blk, Q_BLOCK)
    rel_bias_f = rel_bias.astype(jnp.float32)

    def to_blocks(t):
        return t.reshape(bsz, n_blk, Q_BLOCK, t.shape[-1]).swapaxes(0, 1)

    def one_block(args):
        cq_b, wh_b, qp = args
        q_idx = (cq_b @ w_qidx).reshape(bsz, Q_BLOCK, IDX_HEADS, IDX_DIM)
        dots = jnp.einsum("bqhd,bsd->bqhs", q_idx, k_idx) * (IDX_DIM ** -0.5)
        score = jnp.einsum("bqh,bqhs->bqs", wh_b.astype(jnp.float32),
                           jax.nn.relu(dots).astype(jnp.float32))
        causal = key_pos[None, :] <= qp[:, None]
        score = jnp.where(causal[None], score, -jnp.inf)
        _, idx = lax.top_k(score, k_sel)
        valid = idx <= qp[None, :, None]
        kv_sel = jax.vmap(lambda c, i: c[i])(c_kv, idx)
        q = (cq_b @ w_uq).reshape(bsz, Q_BLOCK, N_HEADS, HEAD_DIM)
        q_lat = jnp.einsum("bqhd,chd->bqhc", q, w_uk)
        logits = jnp.einsum("bqhc,bqkc->bqhk", q_lat, kv_sel).astype(jnp.float32) * (HEAD_DIM ** -0.5)
        bucket = causal_rel_bucket(qp[None, :, None] - idx)
        logits = logits + jnp.transpose(rel_bias_f[bucket], (0, 1, 3, 2))
        logits = jnp.where(valid[:, :, None, :], logits, -jnp.inf)
        p = jax.nn.softmax(logits, axis=-1).astype(h.dtype)
        o_lat = jnp.einsum("bqhk,bqkc->bqhc", p, kv_sel)
        o = jnp.einsum("bqhc,chd->bqhd", o_lat, w_uv)
        return o.reshape(bsz, Q_BLOCK, ATT_WIDTH)

    out = lax.map(one_block, (to_blocks(c_q), to_blocks(w_head), q_pos))
    out = out.swapaxes(0, 1).reshape(bsz, seq, ATT_WIDTH)
    return out @ w_out


def short_conv_mixer(h, w_in, conv_w, w_out):
    gate_b, gate_c, xin = jnp.split(h @ w_in, 3, axis=-1)
    y = gate_b * causal_depthwise_conv(gate_c * xin, conv_w)
    return y @ w_out


def conformer_conv_mixer(h, w_in, conv_w, conv_b, ln_g, ln_b, w_out):
    a, g = jnp.split(h @ w_in, 2, axis=-1)
    u = a * jax.nn.sigmoid(g)
    u = causal_depthwise_conv(u, conv_w) + conv_b
    u = layer_norm(u, ln_g, ln_b)
    return jax.nn.silu(u) @ w_out


def rglru_mixer(h, w_in, conv_w, conv_b, w_a, b_a, w_x, b_x, lam, w_out):
    bsz, seq, _ = h.shape
    gate_br, x_br = jnp.split(h @ w_in, 2, axis=-1)
    gate_br = jax.nn.gelu(gate_br)
    xc = causal_depthwise_conv(x_br, conv_w) + conv_b
    xb = xc.reshape(bsz, seq, LRU_HEADS, LRU_BLOCK)
    r = jax.nn.sigmoid(jnp.einsum("bshi,hij->bshj", xb, w_a).reshape(bsz, seq, LRU_WIDTH) + b_a)
    i_g = jax.nn.sigmoid(jnp.einsum("bshi,hij->bshj", xb, w_x).reshape(bsz, seq, LRU_WIDTH) + b_x)
    log_a = -LRU_C * r.astype(jnp.float32) * jax.nn.softplus(-lam.astype(jnp.float32))
    a = jnp.exp(log_a)
    u = jnp.sqrt(-jnp.expm1(2.0 * log_a)) * (i_g * xc).astype(jnp.float32)

    def combine(left, right):
        a1, b1 = left
        a2, b2 = right
        return a1 * a2, a2 * b1 + b2

    _, hs = lax.associative_scan(combine, (a, u), axis=1)
    return (hs.astype(h.dtype) * gate_br) @ w_out


def swiglu(t, w13, w2):
    g, u = jnp.split(t @ w13, 2, axis=-1)
    return (jax.nn.silu(g) * u) @ w2


def moe_swiglu(h, router, w13, w2):
    d = h.shape[-1]
    t = h.reshape(-1, d)
    logits = (t @ router).astype(jnp.float32)
    top_v, top_i = lax.top_k(logits, MOE_TOPK)
    top_w = jax.nn.softmax(top_v, axis=-1)
    gates = jnp.sum(jax.nn.one_hot(top_i, N_EXPERTS, dtype=jnp.float32) * top_w[..., None],
                    axis=1).astype(t.dtype)
    out = jnp.zeros_like(t)
    for e in range(N_EXPERTS):
        out = out + gates[:, e:e + 1] * swiglu(t, w13[e], w2[e])
    return out.reshape(h.shape)


def modulate(h, shift, scale):
    return h * (1.0 + scale) + shift


def setup_inputs(seed: int = 0) -> dict:
    key = jax.random.key(seed)
    ks = iter(jax.random.split(key, 64))
    f32 = jnp.float32

    def nrm(shape, fan_in, scale=1.0):
        return jax.random.normal(next(ks), shape, f32) * (scale * fan_in ** -0.5)

    def gain(shape):
        return 1.0 + 0.02 * jax.random.normal(next(ks), shape, f32)

    def small(shape, s=0.02):
        return s * jax.random.normal(next(ks), shape, f32)

    x = jax.random.normal(next(ks), (BATCH, SEQ, D_MODEL), f32)
    c = jax.random.normal(next(ks), (BATCH, D_MODEL), f32)
    ada_w = nrm((D_MODEL, N_ADA * D_MODEL), D_MODEL, 0.5)
    ada_b = small((N_ADA * D_MODEL,))
    ada_table = small((DEPTH, N_ADA, D_MODEL), 0.1)
    norm_mix = gain((DEPTH, D_MODEL))
    norm_ffn = gain((DEPTH, D_MODEL))
    norm_final = gain((D_MODEL,))
    rel_bias = small((REL_BUCKETS, N_HEADS), 0.5)

    att_w_in = nrm((N_ATT, D_MODEL, ATT_IN), D_MODEL)
    att_g_cq = gain((N_ATT, Q_LORA))
    att_g_ckv = gain((N_ATT, KV_LORA))
    att_w_uq = nrm((N_ATT, Q_LORA, ATT_WIDTH), Q_LORA)
    att_w_uk = nrm((N_ATT, KV_LORA, N_HEADS, HEAD_DIM), KV_LORA)
    att_w_uv = nrm((N_ATT, KV_LORA, N_HEADS, HEAD_DIM), KV_LORA)
    att_w_qidx = nrm((N_ATT, Q_LORA, IDX_HEADS * IDX_DIM), Q_LORA)
    att_w_out = nrm((N_ATT, ATT_WIDTH, D_MODEL), ATT_WIDTH)

    sconv_w_in = nrm((N_SCONV, D_MODEL, 3 * D_MODEL), D_MODEL)
    sconv_conv_w = nrm((N_SCONV, SCONV_WIDTH, D_MODEL), SCONV_WIDTH)
    sconv_w_out = nrm((N_SCONV, D_MODEL, D_MODEL), D_MODEL)

    conf_w_in = nrm((N_CONF, D_MODEL, 2 * D_MODEL), D_MODEL)
    conf_conv_w = nrm((N_CONF, CONF_WIDTH, D_MODEL), CONF_WIDTH)
    conf_conv_b = small((N_CONF, D_MODEL))
    conf_ln_g = gain((N_CONF, D_MODEL))
    conf_ln_b = small((N_CONF, D_MODEL))
    conf_w_out = nrm((N_CONF, D_MODEL, D_MODEL), D_MODEL)

    lru_w_in = nrm((N_LRU, D_MODEL, 2 * LRU_WIDTH), D_MODEL)
    lru_conv_w = nrm((N_LRU, LRU_CONV_WIDTH, LRU_WIDTH), LRU_CONV_WIDTH)
    lru_conv_b = small((N_LRU, LRU_WIDTH))
    lru_w_a = nrm((N_LRU, LRU_HEADS, LRU_BLOCK, LRU_BLOCK), LRU_BLOCK)
    lru_b_a = small((N_LRU, LRU_WIDTH))
    lru_w_x = nrm((N_LRU, LRU_HEADS, LRU_BLOCK, LRU_BLOCK), LRU_BLOCK)
    lru_b_x = small((N_LRU, LRU_WIDTH))
    a_pow_c = jax.random.uniform(next(ks), (N_LRU, LRU_WIDTH), f32, minval=0.9, maxval=0.999)
    s = a_pow_c ** (1.0 / LRU_C)
    lru_lambda = jnp.log(s) - jnp.log1p(-s)
    lru_w_out = nrm((N_LRU, LRU_WIDTH, D_MODEL), LRU_WIDTH)

    ffn_w13 = nrm((N_DENSE, D_MODEL, 2 * D_FF), D_MODEL)
    ffn_w2 = nrm((N_DENSE, D_FF, D_MODEL), D_FF)
    moe_router = nrm((N_MOE, D_MODEL, N_EXPERTS), D_MODEL)
    moe_w13 = nrm((N_MOE, N_EXPERTS, D_MODEL, 2 * D_FF_EXPERT), D_MODEL)
    moe_w2 = nrm((N_MOE, N_EXPERTS, D_FF_EXPERT, D_MODEL), D_FF_EXPERT)

    return {
        "x": x, "c": c, "ada_w": ada_w, "ada_b": ada_b, "ada_table": ada_table,
        "norm_mix": norm_mix, "norm_ffn": norm_ffn, "norm_final": norm_final, "rel_bias": rel_bias,
        "att_w_in": att_w_in, "att_g_cq": att_g_cq, "att_g_ckv": att_g_ckv, "att_w_uq": att_w_uq,
        "att_w_uk": att_w_uk, "att_w_uv": att_w_uv, "att_w_qidx": att_w_qidx, "att_w_out": att_w_out,
        "sconv_w_in": sconv_w_in, "sconv_conv_w": sconv_conv_w, "sconv_w_out": sconv_w_out,
        "conf_w_in": conf_w_in, "conf_conv_w": conf_conv_w, "conf_conv_b": conf_conv_b,
        "conf_ln_g": conf_ln_g, "conf_ln_b": conf_ln_b, "conf_w_out": conf_w_out,
        "lru_w_in": lru_w_in, "lru_conv_w": lru_conv_w, "lru_conv_b": lru_conv_b,
        "lru_w_a": lru_w_a, "lru_b_a": lru_b_a, "lru_w_x": lru_w_x, "lru_b_x": lru_b_x,
        "lru_lambda": lru_lambda, "lru_w_out": lru_w_out,
        "ffn_w13": ffn_w13, "ffn_w2": ffn_w2, "moe_router": moe_router,
        "moe_w13": moe_w13, "moe_w2": moe_w2,
    }


def reference(x, c, ada_w, ada_b, ada_table, norm_mix, norm_ffn, norm_final, rel_bias,
              att_w_in, att_g_cq, att_g_ckv, att_w_uq, att_w_uk, att_w_uv, att_w_qidx, att_w_out,
              sconv_w_in, sconv_conv_w, sconv_w_out,
              conf_w_in, conf_conv_w, conf_conv_b, conf_ln_g, conf_ln_b, conf_w_out,
              lru_w_in, lru_conv_w, lru_conv_b, lru_w_a, lru_b_a, lru_w_x, lru_b_x, lru_lambda, lru_w_out,
              ffn_w13, ffn_w2, moe_router, moe_w13, moe_w2):
    bsz = x.shape[0]
    mod_all = (jax.nn.silu(c) @ ada_w + ada_b).reshape(bsz, N_ADA, D_MODEL)
    for i in range(DEPTH):
        mod = mod_all + ada_table[i]
        shift_m, scale_m, gate_m = mod[:, 0, None, :], mod[:, 1, None, :], mod[:, 2, None, :]
        shift_f, scale_f, gate_f = mod[:, 3, None, :], mod[:, 4, None, :], mod[:, 5, None, :]

        h = modulate(rms_norm(x, norm_mix[i]), shift_m, scale_m)
        kind, j = i % N_MIXERS, i // N_MIXERS
        if kind == 0:
            y = dsa_attention(h, rel_bias, att_w_in[j], att_g_cq[j], att_g_ckv[j], att_w_uq[j],
                              att_w_uk[j], att_w_uv[j], att_w_qidx[j], att_w_out[j])
        elif kind == 1:
            y = short_conv_mixer(h, sconv_w_in[j], sconv_conv_w[j], sconv_w_out[j])
        elif kind == 2:
            y = conformer_conv_mixer(h, conf_w_in[j], conf_conv_w[j], conf_conv_b[j],
                                     conf_ln_g[j], conf_ln_b[j], conf_w_out[j])
        else:
            y = rglru_mixer(h, lru_w_in[j], lru_conv_w[j], lru_conv_b[j], lru_w_a[j], lru_b_a[j],
                            lru_w_x[j], lru_b_x[j], lru_lambda[j], lru_w_out[j])
        x = x + gate_m * y

        h = modulate(rms_norm(x, norm_ffn[i]), shift_f, scale_f)
        if i % 2 == 0:
            y = swiglu(h, ffn_w13[i // 2], ffn_w2[i // 2])
        else:
            y = moe_swiglu(h, moe_router[i // 2], moe_w13[i // 2], moe_w2[i // 2])
        x = x + gate_f * y
    return rms_norm(x, norm_final)
```

```python
import functools
import math

import jax
import jax.numpy as jnp
from jax import lax
from jax.experimental import pallas as pl
from jax.experimental.pallas import tpu as pltpu

F32 = jnp.float32
BF16 = jnp.bfloat16

NORM_EPS = 1e-6
N_ADA = 6
MOE_TOPK = 2
LRU_C = 8.0
REL_BUCKETS = 32
REL_MAX_DIST = 128
TOPK_MAX = 256

LANES = 128
SUBLANES = 8
Q_BLOCK = 128
KEY_CHUNK = 256
HEAD_GROUP = 8
CONV_HALO = 32
VMEM_LIMIT = 56 * 1024 * 1024
MASK_NEG = -1e30
INT_MIN = -2 ** 31


def _cparams(*sem):
    return pltpu.CompilerParams(dimension_semantics=sem, vmem_limit_bytes=VMEM_LIMIT)


def _sigmoid(x):
    return 1.0 / (1.0 + jnp.exp(-x))


def _silu(x):
    return x * _sigmoid(x)


def _gelu_tanh(x):
    return 0.5 * x * (1.0 + jnp.tanh(math.sqrt(2.0 / math.pi) * (x + 0.044715 * (x * x * x))))


def _rms(x, g):
    return x * lax.rsqrt(jnp.mean(x * x, axis=-1, keepdims=True) + NORM_EPS) * g


def _mm_kernel(*refs, n_w, n_ex, n_out, nk, epilogue, a_fn, w_fn):
    a_ref = refs[0]
    w_refs = refs[1:1 + n_w]
    ex_refs = refs[1 + n_w:1 + n_w + n_ex]
    out_refs = refs[1 + n_w + n_ex:1 + n_w + n_ex + n_out]
    acc_refs = refs[1 + n_w + n_ex + n_out:]
    ids = (pl.program_id(0), pl.program_id(1))
    a = a_fn(a_ref[...])
    if nk == 1:
        accs = [jnp.dot(a, w_fn(w[...]), preferred_element_type=F32) for w in w_refs]
        epilogue(accs, ex_refs, out_refs, ids)
        return
    k = pl.program_id(2)

    @pl.when(k == 0)
    def _():
        for acc in acc_refs:
            acc[...] = jnp.zeros_like(acc)

    for acc, w in zip(acc_refs, w_refs):
        acc[...] += jnp.dot(a, w_fn(w[...]), preferred_element_type=F32)

    @pl.when(k == nk - 1)
    def _():
        epilogue([acc[...] for acc in acc_refs], ex_refs, out_refs, ids)


def _identity(v):
    return v


def _mm(a, a_spec, ws, extras, outs, epilogue, *, grid, acc_shapes=(), a_fn=_identity, w_fn=_identity):
    nk = grid[2] if len(grid) == 3 else 1
    kern = functools.partial(_mm_kernel, n_w=len(ws), n_ex=len(extras), n_out=len(outs), nk=nk,
                             epilogue=epilogue, a_fn=a_fn, w_fn=w_fn)
    sem = ("parallel", "parallel") + (("arbitrary",) if len(grid) == 3 else ())
    res = pl.pallas_call(
        kern,
        grid=grid,
        in_specs=[a_spec] + [s for _, s in ws] + [s for _, s in extras],
        out_specs=[s for _, s in outs],
        out_shape=[o for o, _ in outs],
        scratch_shapes=[pltpu.VMEM(s, F32) for s in acc_shapes] if nk > 1 else [],
        compiler_params=_cparams(*sem),
    )(a, *[w for w, _ in ws], *[e for e, _ in extras])
    return res


def _ep_resid(accs, ex, outs, ids):
    x_ref, gate_ref = ex
    outs[0][...] = x_ref[...] + gate_ref[0] * accs[0]


def _matmul_resid(a, w, x, gate, seq, *, tm, tn, tk):
    t, kdim = a.shape
    n = w.shape[1]
    tm, tn, tk = min(tm, seq), min(tn, n), min(tk, kdim)
    nk = kdim // tk
    grid = (t // tm, n // tn) + ((nk,) if nk > 1 else ())
    rows_per_batch = seq // tm
    out = _mm(
        a, pl.BlockSpec((tm, tk), lambda i, j, *k: (i, k[0] if k else 0)),
        [(w, pl.BlockSpec((tk, tn), lambda i, j, *k: (k[0] if k else 0, j)))],
        [(x, pl.BlockSpec((tm, tn), lambda i, j, *k: (i, j))),
         (gate, pl.BlockSpec((1, 1, tn), lambda i, j, *k: (i // rows_per_batch, 0, j)))],
        [(jax.ShapeDtypeStruct((t, n), F32), pl.BlockSpec((tm, tn), lambda i, j, *k: (i, j)))],
        _ep_resid, grid=grid, acc_shapes=[(tm, tn)])
    return out[0]


def _swiglu_act(h, w13, gates, *, tm, tn):
    t, d = h.shape
    n_e, _, f2 = w13.shape
    f = f2 // 2
    tm, tn = min(tm, t), min(tn, f)
    nf = f // tn

    def epilogue(accs, ex, outs, ids):
        act = _silu(accs[0]) * accs[1]
        if gates is not None:
            g = ex[0][...]
            lane = lax.broadcasted_iota(jnp.int32, g.shape, 1)
            act = act * jnp.sum(jnp.where(lane == ids[1] // nf, g, 0.0), axis=1, keepdims=True)
        outs[0][...] = act.astype(BF16)

    extras = [] if gates is None else [(gates, pl.BlockSpec((tm, LANES), lambda i, j: (i, 0)))]
    out = _mm(
        h, pl.BlockSpec((tm, d), lambda i, j: (i, 0)),
        [(w13, pl.BlockSpec((None, d, tn), lambda i, j: (j // nf, 0, j % nf))),
         (w13, pl.BlockSpec((None, d, tn), lambda i, j: (j // nf, 0, j % nf + nf)))],
        extras,
        [(jax.ShapeDtypeStruct((t, n_e * f), BF16), pl.BlockSpec((tm, tn), lambda i, j: (i, j)))],
        epilogue, grid=(t // tm, n_e * nf))
    return out[0]


def _ada_mod(c, ada_w, ada_b, ada_table):
    bsz, d = c.shape
    b = 16
    c = jnp.pad(c, ((0, b - bsz), (0, 0)))
    depth = ada_table.shape[0]
    n = ada_w.shape[1]
    tn = min(512, n)

    def epilogue(accs, ex, outs, ids):
        bias_ref, tab_ref = ex
        outs[0][...] = (accs[0] + bias_ref[...])[None] + tab_ref[...]

    out = _mm(
        c, pl.BlockSpec((b, d), lambda i, j: (0, 0)),
        [(ada_w, pl.BlockSpec((d, tn), lambda i, j: (0, j)))],
        [(ada_b.reshape(1, n), pl.BlockSpec((1, tn), lambda i, j: (0, j))),
         (ada_table.reshape(depth, 1, n), pl.BlockSpec((depth, 1, tn), lambda i, j: (0, 0, j)))],
        [(jax.ShapeDtypeStruct((depth, b, n), F32), pl.BlockSpec((depth, b, tn), lambda i, j: (0, 0, j)))],
        epilogue, grid=(1, n // tn),
        a_fn=lambda v: _silu(v).astype(BF16), w_fn=lambda v: v.astype(BF16))
    return out[0][:, :bsz].reshape(depth, bsz, N_ADA, d)


def _norm_kernel(*refs, modulate, n_experts):
    if not modulate:
        x_ref, g_ref, out_ref = refs
        out_ref[...] = _rms(x_ref[...], g_ref[...])
        return
    x_ref, g_ref, sh_ref, sc_ref = refs[:4]
    h = _rms(x_ref[...], g_ref[...]) * (1.0 + sc_ref[0]) + sh_ref[0]
    if not n_experts:
        refs[4][...] = h.astype(BF16)
        return
    r_ref, h_ref, gates_ref = refs[4:]
    h_ref[...] = h.astype(BF16)
    logits = jnp.dot(h, r_ref[...], preferred_element_type=F32, precision=lax.Precision.HIGHEST)
    lane = lax.broadcasted_iota(jnp.int32, logits.shape, 1)
    lg = jnp.where(lane < n_experts, logits, -jnp.inf)
    m1 = jnp.max(lg, axis=1, keepdims=True)
    i1 = jnp.min(jnp.where(lg == m1, lane, LANES), axis=1, keepdims=True)
    lg2 = jnp.where(lane == i1, -jnp.inf, lg)
    m2 = jnp.max(lg2, axis=1, keepdims=True)
    i2 = jnp.min(jnp.where(lg2 == m2, lane, LANES), axis=1, keepdims=True)
    e2 = jnp.exp(m2 - m1)
    w1 = 1.0 / (1.0 + e2)
    gates_ref[...] = jnp.where(lane == i1, w1, 0.0) + jnp.where(lane == i2, e2 * w1, 0.0)


def _norm_mod(x, g, shift, scale, seq, router=None, *, tm=256):
    t, d = x.shape
    tm = min(tm, seq)
    rpb = seq // tm
    row = pl.BlockSpec((tm, d), lambda i: (i, 0))
    vec = pl.BlockSpec((1, d), lambda i: (0, 0))
    per_batch = pl.BlockSpec((1, 1, d), lambda i: (i // rpb, 0, 0))
    ins = [x, g.reshape(1, d), shift, scale]
    in_specs = [row, vec, per_batch, per_batch]
    out_shape = [jax.ShapeDtypeStruct((t, d), BF16)]
    out_specs = [row]
    n_experts = 0
    if router is not None:
        n_experts = router.shape[1]
        ins.append(jnp.pad(router, ((0, 0), (0, LANES - n_experts))))
        in_specs.append(pl.BlockSpec((d, LANES), lambda i: (0, 0)))
        out_shape.append(jax.ShapeDtypeStruct((t, LANES), F32))
        out_specs.append(pl.BlockSpec((tm, LANES), lambda i: (i, 0)))
    res = pl.pallas_call(
        functools.partial(_norm_kernel, modulate=True, n_experts=n_experts),
        grid=(t // tm,), in_specs=in_specs, out_specs=out_specs, out_shape=out_shape,
        compiler_params=_cparams("parallel"))(*ins)
    return res if router is not None else res[0]


def _final_norm(x, g, *, tm=256):
    t, d = x.shape
    tm = min(tm, t)
    row = pl.BlockSpec((tm, d), lambda i: (i, 0))
    return pl.pallas_call(
        functools.partial(_norm_kernel, modulate=False, n_experts=0),
        grid=(t // tm,), in_specs=[row, pl.BlockSpec((1, d), lambda i: (0, 0))], out_specs=row,
        out_shape=jax.ShapeDtypeStruct((t, d), F32), compiler_params=_cparams("parallel"))(x, g.reshape(1, d))


def _conv_kernel(*refs, width, ts, tc, has_bias, has_mul):
    cur_ref, halo_ref, w_ref = refs[:3]
    rest = list(refs[3:])
    b_ref = rest.pop(0) if has_bias else None
    mul_ref = rest.pop(0) if has_mul else None
    out_ref, buf = rest
    first = pl.program_id(1) == 0
    buf[0:CONV_HALO, :] = jnp.where(first, 0.0, halo_ref[0])
    buf[CONV_HALO:CONV_HALO + ts, :] = cur_ref[0]
    rows = 32
    for r in range(0, ts, rows):
        acc = None
        for k in range(width):
            term = w_ref[k:k + 1, :] * buf[pl.ds(CONV_HALO + r - (width - 1) + k, rows), :]
            acc = term if acc is None else acc + term
        if has_bias:
            acc = acc + b_ref[...]
        if has_mul:
            acc = acc * mul_ref[0, r:r + rows, :].astype(F32)
        out_ref[0, r:r + rows, :] = acc.astype(out_ref.dtype)


def _causal_conv(x, conv_w, bias=None, mul=None, out_dtype=F32, *, ts=128, tc=512):
    b, s, ch = x.shape
    width = conv_w.shape[0]
    assert width - 1 <= CONV_HALO
    ts, tc = min(ts, s), min(tc, ch)
    hpb = ts // CONV_HALO
    blk = pl.BlockSpec((1, ts, tc), lambda bi, si, ci: (bi, si, ci))
    ins = [x, x, conv_w]
    in_specs = [blk,
                pl.BlockSpec((1, CONV_HALO, tc), lambda bi, si, ci: (bi, jnp.maximum(si * hpb - 1, 0), ci)),
                pl.BlockSpec((width, tc), lambda bi, si, ci: (0, ci))]
    if bias is not None:
        ins.append(bias.reshape(1, ch))
        in_specs.append(pl.BlockSpec((1, tc), lambda bi, si, ci: (0, ci)))
    if mul is not None:
        ins.append(mul)
        in_specs.append(blk)
    return pl.pallas_call(
        functools.partial(_conv_kernel, width=width, ts=ts, tc=tc, has_bias=bias is not None,
                          has_mul=mul is not None),
        grid=(b, s // ts, ch // tc), in_specs=in_specs, out_specs=blk,
        out_shape=jax.ShapeDtypeStruct((b, s, ch), out_dtype),
        scratch_shapes=[pltpu.VMEM((CONV_HALO + ts, tc), F32)],
        compiler_params=_cparams("parallel", "parallel", "parallel"))(*ins)


def _ln_silu_kernel(x_ref, g_ref, b_ref, out_ref):
    x = x_ref[...]
    mu = jnp.mean(x, axis=-1, keepdims=True)
    xc = x - mu
    var = jnp.mean(xc * xc, axis=-1, keepdims=True)
    y = xc * lax.rsqrt(var + NORM_EPS) * g_ref[...] + b_ref[...]
    out_ref[...] = _silu(y).astype(BF16)


def _ln_silu(x, g, b, *, tm=256):
    t, d = x.shape
    tm = min(tm, t)
    row = pl.BlockSpec((tm, d), lambda i: (i, 0))
    vec = pl.BlockSpec((1, d), lambda i: (0, 0))
    return pl.pallas_call(
        _ln_silu_kernel, grid=(t // tm,), in_specs=[row, vec, vec], out_specs=row,
        out_shape=jax.ShapeDtypeStruct((t, d), BF16), compiler_params=_cparams("parallel"))(
            x, g.reshape(1, d), b.reshape(1, d))


def _rglru_kernel(xc_ref, gate_ref, wa_ref, wx_ref, ba_ref, bx_ref, lam_ref, out_ref,
                  a_scr, u_scr, h_scr, *, ts, n_heads, blk, cw):
    @pl.when(pl.program_id(1) == 0)
    def _():
        h_scr[...] = jnp.zeros_like(h_scr)

    neg_lam = -lam_ref[...]
    softplus = jnp.maximum(neg_lam, 0.0) + jnp.log1p(jnp.exp(-jnp.abs(neg_lam)))
    for hd in range(n_heads):
        sl = slice(hd * blk, (hd + 1) * blk)
        xh = xc_ref[:, sl]
        xb = xh.astype(BF16)
        r = _sigmoid(jnp.dot(xb, wa_ref[hd], preferred_element_type=F32) + ba_ref[:, sl])
        i_g = _sigmoid(jnp.dot(xb, wx_ref[hd], preferred_element_type=F32) + bx_ref[:, sl])
        log_a = -LRU_C * r * softplus[:, sl]
        a_scr[:, sl] = jnp.exp(log_a)
        u_scr[:, sl] = jnp.sqrt(1.0 - jnp.exp(2.0 * log_a)) * (i_g * xh)

    row = lax.broadcasted_iota(jnp.int32, (SUBLANES, cw), 0)
    width = n_heads * blk
    for c0 in range(0, width, cw):
        def body(g, h_prev, c0=c0):
            r0 = pl.multiple_of(g * SUBLANES, SUBLANES)
            a8 = a_scr[pl.ds(r0, SUBLANES), c0:c0 + cw]
            b8 = u_scr[pl.ds(r0, SUBLANES), c0:c0 + cw]
            for d in (1, 2, 4):
                keep = row >= d
                b8 = jnp.where(keep, a8 * pltpu.roll(b8, d, axis=0) + b8, b8)
                a8 = jnp.where(keep, a8 * pltpu.roll(a8, d, axis=0), a8)
            h8 = b8 + a8 * h_prev
            u_scr[pl.ds(r0, SUBLANES), c0:c0 + cw] = h8
            return h8[SUBLANES - 1:SUBLANES, :]

        h_scr[:, c0:c0 + cw] = lax.fori_loop(0, ts // SUBLANES, body, h_scr[:, c0:c0 + cw])
    out_ref[...] = (u_scr[...] * gate_ref[...].astype(F32)).astype(BF16)


def _rglru(xc, gate, w_a, w_x, b_a, b_x, lam, seq, *, ts=256):
    t, width = xc.shape
    n_heads, blk, _ = w_a.shape
    ts = min(ts, seq)
    nts = seq // ts
    row = pl.BlockSpec((ts, width), lambda b, j: (b * nts + j, 0))
    wsp = pl.BlockSpec((n_heads, blk, blk), lambda b, j: (0, 0, 0))
    vec = pl.BlockSpec((1, width), lambda b, j: (0, 0))
    return pl.pallas_call(
        functools.partial(_rglru_kernel, ts=ts, n_heads=n_heads, blk=blk, cw=min(1024, width)),
        grid=(t // seq, nts), in_specs=[row, row, wsp, wsp, vec, vec, vec], out_specs=row,
        out_shape=jax.ShapeDtypeStruct((t, width), BF16),
        scratch_shapes=[pltpu.VMEM((ts, width), F32), pltpu.VMEM((ts, width), F32), pltpu.VMEM((1, width), F32)],
        compiler_params=_cparams("parallel", "arbitrary"))(
            xc, gate, w_a, w_x, b_a.reshape(1, width), b_x.reshape(1, width), lam.reshape(1, width))


def _bias_tile_kernel(rb_ref, out_ref):
    hd = pl.program_id(0)
    shape = (Q_BLOCK, 2 * Q_BLOCK)
    dist = Q_BLOCK + lax.broadcasted_iota(jnp.int32, shape, 0) - lax.broadcasted_iota(jnp.int32, shape, 1)
    dist = jnp.maximum(dist, 0)
    max_exact = REL_BUCKETS // 2
    large = max_exact + (jnp.log(jnp.maximum(dist, 1).astype(F32) / max_exact)
                         / math.log(REL_MAX_DIST / max_exact) * (REL_BUCKETS - max_exact)).astype(jnp.int32)
    bucket = jnp.where(dist < max_exact, dist, jnp.minimum(large, REL_BUCKETS - 1))
    tile = jnp.zeros(shape, F32)
    for bkt in range(REL_BUCKETS):
        tile = jnp.where(bucket == bkt, rb_ref[bkt, hd], tile)
    out_ref[0] = tile


def _bias_tile(rel_bias):
    n_heads = rel_bias.shape[1]
    return pl.pallas_call(
        _bias_tile_kernel, grid=(n_heads,),
        in_specs=[pl.BlockSpec(memory_space=pltpu.SMEM)],
        out_specs=pl.BlockSpec((1, Q_BLOCK, 2 * Q_BLOCK), lambda h: (h, 0, 0)),
        out_shape=jax.ShapeDtypeStruct((n_heads, Q_BLOCK, 2 * Q_BLOCK), F32),
        compiler_params=_cparams("parallel"))(rel_bias)


def _dsa_kernel(qidx_ref, wcol_ref, kidx_ref, ckv_ref, qlat_ref, bnear_ref, bfar_ref, out_ref,
                score_ref, madd_ref, m_ref, l_ref, acc_ref, *, n_slab, k_sel, idx_heads, kv_dim):
    qb = pl.program_id(1)
    rows = HEAD_GROUP * Q_BLOCK
    nt = (((1,), (1,)), ((), ()))

    @pl.when(pl.program_id(2) == 0)
    def _select():
        n_chunks = (qb + 2) // 2

        def chunk_body(c, carry):
            keys = kidx_ref[0, pl.ds(pl.multiple_of(c * KEY_CHUNK, KEY_CHUNK), KEY_CHUNK), :]

            def head_body(hg, acc):
                h0 = pl.multiple_of(hg * HEAD_GROUP, HEAD_GROUP)
                q = qidx_ref[0, pl.ds(h0, HEAD_GROUP)].reshape(rows, LANES)
                dots = lax.dot_general(q, keys, nt, preferred_element_type=F32)
                dots = jnp.maximum(dots, 0.0).reshape(HEAD_GROUP, Q_BLOCK, KEY_CHUNK)
                return acc + jnp.sum(dots * wcol_ref[0, pl.ds(h0, HEAD_GROUP)], axis=0)

            acc = lax.fori_loop(0, idx_heads // HEAD_GROUP, head_body, jnp.zeros((Q_BLOCK, KEY_CHUNK), F32))
            score_ref[2 * c] = acc[:, :LANES]
            score_ref[2 * c + 1] = acc[:, LANES:]
            return carry

        lax.fori_loop(0, n_chunks, chunk_body, 0)

        def zero_body(s, carry):
            score_ref[s] = jnp.zeros((Q_BLOCK, LANES), F32)
            return carry

        lax.fori_loop(2 * n_chunks, n_slab, zero_body, 0)

        shape = (n_slab, Q_BLOCK, LANES)
        bits = pltpu.bitcast(score_ref[...], jnp.int32)
        key = jnp.where(bits >= 0, bits, bits ^ jnp.int32(0x7FFFFFFF))
        key_pos = lax.broadcasted_iota(jnp.int32, shape, 0) * LANES + lax.broadcasted_iota(jnp.int32, shape, 2)
        q_pos = qb * Q_BLOCK + lax.broadcasted_iota(jnp.int32, shape, 1)
        causal = key_pos <= q_pos
        key = jnp.where(causal, key, INT_MIN)

        def bit_body(it, thr):
            cand = thr + lax.shift_left(jnp.int32(1), 31 - it)
            cnt = jnp.sum(jnp.where(key >= cand, 1.0, 0.0), axis=0)
            cnt = jnp.sum(cnt, axis=1, keepdims=True)[None]
            return jnp.where(cnt >= k_sel, cand, thr)

        thr = lax.fori_loop(0, 32, bit_body, jnp.full((1, Q_BLOCK, 1), INT_MIN, jnp.int32))
        madd_ref[0] = jnp.full((Q_BLOCK, LANES), MASK_NEG, F32)
        madd_ref[1:] = jnp.where(causal & (key >= thr), 0.0, MASK_NEG)

    q = qlat_ref[0].reshape(rows, kv_dim)
    m_ref[...] = jnp.full(m_ref.shape, MASK_NEG, F32)
    l_ref[...] = jnp.zeros(l_ref.shape, F32)
    acc_ref[...] = jnp.zeros(acc_ref.shape, F32)

    def attend(kv, z):
        m_old = m_ref[...]
        m_new = jnp.maximum(m_old, jnp.max(z, axis=-1, keepdims=True))
        alpha = jnp.exp(m_old - m_new)
        p = jnp.exp(z - m_new)
        l_ref[...] = alpha * l_ref[...] + jnp.sum(p, axis=-1, keepdims=True)
        pv = jnp.dot(p.astype(BF16).reshape(rows, KEY_CHUNK), kv, preferred_element_type=F32)
        acc_ref[...] = alpha * acc_ref[...] + pv.reshape(HEAD_GROUP, Q_BLOCK, kv_dim)
        m_ref[...] = m_new

    def logits_of(kv):
        return lax.dot_general(q, kv, nt, preferred_element_type=F32).reshape(HEAD_GROUP, Q_BLOCK, KEY_CHUNK)

    def far_body(c, carry):
        kv = ckv_ref[0, pl.ds(pl.multiple_of(Q_BLOCK + c * KEY_CHUNK, Q_BLOCK), KEY_CHUNK), :]
        second = jnp.where(2 * c + 2 < qb, madd_ref[2 * c + 2], MASK_NEG)
        madd = jnp.concatenate([madd_ref[2 * c + 1], second], axis=1)
        attend(kv, logits_of(kv) + madd[None] + bfar_ref[0])
        return carry

    lax.fori_loop(0, qb // 2, far_body, 0)

    kv = ckv_ref[0, pl.ds(pl.multiple_of(qb * Q_BLOCK, Q_BLOCK), KEY_CHUNK), :]
    madd = jnp.concatenate([madd_ref[qb], madd_ref[qb + 1]], axis=1)
    attend(kv, logits_of(kv) + madd[None] + bnear_ref[...])
    out_ref[0] = (acc_ref[...] / l_ref[...]).astype(BF16)


def _dsa_core(qidx, wcol, kidx, ckv_pad, qlat, bnear, bfar, *, bsz, seq, k_sel):
    n_qb = seq // Q_BLOCK
    idx_heads = qidx.shape[1]
    n_heads, kv_dim = qlat.shape[1], qlat.shape[3]
    n_groups = n_heads // HEAD_GROUP
    blk_q = lambda b, i, g: (b * n_qb + i, 0, 0, 0)
    grp = lambda b, i, g: (b * n_qb + i, g, 0, 0)
    return pl.pallas_call(
        functools.partial(_dsa_kernel, n_slab=n_qb, k_sel=k_sel, idx_heads=idx_heads, kv_dim=kv_dim),
        grid=(bsz, n_qb, n_groups),
        in_specs=[
            pl.BlockSpec((1, idx_heads, Q_BLOCK, LANES), blk_q),
            pl.BlockSpec((1, idx_heads, Q_BLOCK, 1), blk_q),
            pl.BlockSpec((1, seq, LANES), lambda b, i, g: (b, 0, 0)),
            pl.BlockSpec((1, seq + Q_BLOCK, kv_dim), lambda b, i, g: (b, 0, 0)),
            pl.BlockSpec((1, HEAD_GROUP, Q_BLOCK, kv_dim), grp),
            pl.BlockSpec((HEAD_GROUP, Q_BLOCK, KEY_CHUNK), lambda b, i, g: (g, 0, 0)),
            pl.BlockSpec((1, HEAD_GROUP, Q_BLOCK, 1), lambda b, i, g: (g, 0, 0, 0)),
        ],
        out_specs=pl.BlockSpec((1, HEAD_GROUP, Q_BLOCK, kv_dim), grp),
        out_shape=jax.ShapeDtypeStruct((bsz * n_qb, n_heads, Q_BLOCK, kv_dim), BF16),
        scratch_shapes=[
            pltpu.VMEM((n_qb, Q_BLOCK, LANES), F32),
            pltpu.VMEM((n_qb + 1, Q_BLOCK, LANES), F32),
            pltpu.VMEM((HEAD_GROUP, Q_BLOCK, 1), F32),
            pltpu.VMEM((HEAD_GROUP, Q_BLOCK, 1), F32),
            pltpu.VMEM((HEAD_GROUP, Q_BLOCK, kv_dim), F32),
        ],
        compiler_params=_cparams("parallel", "parallel", "arbitrary"))(
            qidx, wcol, kidx, ckv_pad, qlat, bnear, bfar)


def _dsa_attention(h, x, gate, rel_bias, w_in, g_cq, g_ckv, w_uq, w_uk, w_uv, w_qidx, w_out, *, bsz, seq):
    t, d = h.shape
    q_lora, kv_lora = g_cq.shape[0], g_ckv.shape[0]
    n_heads, head_dim = w_uk.shape[1], w_uk.shape[2]
    idx_dim = LANES
    idx_heads = w_qidx.shape[1] // idx_dim
    k_sel = min(TOPK_MAX, seq // 4)
    n_qb = seq // Q_BLOCK
    tm = min(512, seq)
    qpt = tm // Q_BLOCK

    w_in = w_in.astype(BF16)
    splits = (0, q_lora, q_lora + kv_lora, q_lora + kv_lora + idx_dim, w_in.shape[1])
    w_parts = [w_in[:, splits[i]:splits[i + 1]] for i in range(4)]
    head_scale = idx_heads ** -0.5 * idx_dim ** -0.5

    def in_epilogue(accs, ex, outs, ids):
        outs[0][...] = _rms(accs[0], ex[0][...]).astype(BF16)
        outs[1][...] = _rms(accs[1], ex[1][...]).astype(BF16)
        outs[2][...] = accs[2].astype(BF16)
        outs[3][...] = accs[3] * head_scale

    full = lambda n: pl.BlockSpec((d, n), lambda i, j: (0, 0))
    rowsp = lambda n: pl.BlockSpec((tm, n), lambda i, j: (i, 0))
    vecsp = lambda n: pl.BlockSpec((1, n), lambda i, j: (0, 0))
    widths = [w.shape[1] for w in w_parts]
    c_q, c_kv, k_idx, w_head = _mm(
        h, pl.BlockSpec((tm, d), lambda i, j: (i, 0)),
        [(w, full(n)) for w, n in zip(w_parts, widths)],
        [(g_cq.reshape(1, q_lora), vecsp(q_lora)), (g_ckv.reshape(1, kv_lora), vecsp(kv_lora))],
        [(jax.ShapeDtypeStruct((t, n), dt), rowsp(n)) for n, dt in zip(widths, (BF16, BF16, BF16, F32))],
        in_epilogue, grid=(t // tm, 1))

    hpt = 4
    tn = hpt * idx_dim

    def qidx_epilogue(accs, ex, outs, ids):
        for hh in range(hpt):
            outs[0][:, hh] = accs[0][:, hh * idx_dim:(hh + 1) * idx_dim].reshape(qpt, Q_BLOCK, idx_dim).astype(BF16)

    qidx = _mm(
        c_q, pl.BlockSpec((tm, q_lora), lambda i, j: (i, 0)),
        [(w_qidx.astype(BF16), pl.BlockSpec((q_lora, tn), lambda i, j: (0, j)))], [],
        [(jax.ShapeDtypeStruct((t // Q_BLOCK, idx_heads, Q_BLOCK, idx_dim), BF16),
          pl.BlockSpec((qpt, hpt, Q_BLOCK, idx_dim), lambda i, j: (i, j, 0, 0)))],
        qidx_epilogue, grid=(t // tm, idx_heads // hpt))[0]

    w_ukt = jnp.transpose(w_uk, (1, 2, 0)).astype(BF16)
    logit_scale = head_dim ** -0.5

    def qlat_kernel(cq_ref, wuq_ref, wuk_ref, out_ref):
        qh = jnp.dot(cq_ref[...], wuq_ref[...], preferred_element_type=F32).astype(BF16)
        ql = jnp.dot(qh, wuk_ref[...], preferred_element_type=F32) * logit_scale
        out_ref[:, 0] = ql.reshape(qpt, Q_BLOCK, kv_lora).astype(BF16)

    qlat = pl.pallas_call(
        qlat_kernel, grid=(t // tm, n_heads),
        in_specs=[pl.BlockSpec((tm, q_lora), lambda i, hd: (i, 0)),
                  pl.BlockSpec((q_lora, head_dim), lambda i, hd: (0, hd)),
                  pl.BlockSpec((None, head_dim, kv_lora), lambda i, hd: (hd, 0, 0))],
        out_specs=pl.BlockSpec((qpt, 1, Q_BLOCK, kv_lora), lambda i, hd: (i, hd, 0, 0)),
        out_shape=jax.ShapeDtypeStruct((t // Q_BLOCK, n_heads, Q_BLOCK, kv_lora), BF16),
        compiler_params=_cparams("parallel", "parallel"))(c_q, w_uq.astype(BF16), w_ukt)

    wcol = w_head.reshape(bsz * n_qb, Q_BLOCK, idx_heads).transpose(0, 2, 1)[..., None]
    ckv_pad = jnp.pad(c_kv.reshape(bsz, seq, kv_lora), ((0, 0), (Q_BLOCK, 0), (0, 0)))
    bnear = _bias_tile(rel_bias.astype(F32))
    bfar = jnp.broadcast_to(rel_bias[REL_BUCKETS - 1].astype(F32).reshape(n_heads // HEAD_GROUP, HEAD_GROUP, 1, 1),
                            (n_heads // HEAD_GROUP, HEAD_GROUP, Q_BLOCK, 1))
    o_lat = _dsa_core(qidx, wcol, k_idx.reshape(bsz, seq, idx_dim), ckv_pad, qlat, bnear, bfar,
                      bsz=bsz, seq=seq, k_sel=k_sel)

    w_uvh = jnp.transpose(w_uv, (1, 0, 2)).astype(BF16)

    def uv_kernel(ol_ref, wuv_ref, out_ref):
        out_ref[...] = jnp.dot(ol_ref[:, 0].reshape(tm, kv_lora), wuv_ref[...],
                               preferred_element_type=F32).astype(BF16)

    o = pl.pallas_call(
        uv_kernel, grid=(t // tm, n_heads),
        in_specs=[pl.BlockSpec((qpt, 1, Q_BLOCK, kv_lora), lambda i, hd: (i, hd, 0, 0)),
                  pl.BlockSpec((None, kv_lora, head_dim), lambda i, hd: (hd, 0, 0))],
        out_specs=pl.BlockSpec((tm, head_dim), lambda i, hd: (i, hd)),
        out_shape=jax.ShapeDtypeStruct((t, n_heads * head_dim), BF16),
        compiler_params=_cparams("parallel", "parallel"))(o_lat, w_uvh)
    return _matmul_resid(o, w_out.astype(BF16), x, gate, seq, tm=1024, tn=512, tk=4096)


def _split_proj(h, w, n_parts, epilogue, out_dtypes, *, tm=1024, tn=256):
    t, d = h.shape
    n = w.shape[1] // n_parts
    tm, tn = min(tm, t), min(tn, n)
    nb = n // tn
    out_sp = pl.BlockSpec((tm, tn), lambda i, j: (i, j))
    return _mm(
        h, pl.BlockSpec((tm, d), lambda i, j: (i, 0)),
        [(w, pl.BlockSpec((d, tn), lambda i, j, p=p: (0, j + p * nb))) for p in range(n_parts)], [],
        [(jax.ShapeDtypeStruct((t, n), dt), out_sp) for dt in out_dtypes],
        epilogue, grid=(t // tm, nb))


def _short_conv_mixer(h, x, gate, w_in, conv_w, w_out, *, bsz, seq):
    def epilogue(accs, ex, outs, ids):
        outs[0][...] = accs[0].astype(BF16)
        outs[1][...] = accs[1] * accs[2]

    gate_b, cx = _split_proj(h, w_in.astype(BF16), 3, epilogue, (BF16, F32))
    ch = cx.shape[1]
    y = _causal_conv(cx.reshape(bsz, seq, ch), conv_w, mul=gate_b.reshape(bsz, seq, ch), out_dtype=BF16)
    return _matmul_resid(y.reshape(bsz * seq, ch), w_out.astype(BF16), x, gate, seq, tm=1024, tn=512, tk=4096)


def _conformer_mixer(h, x, gate, w_in, conv_w, conv_b, ln_g, ln_b, w_out, *, bsz, seq):
    def epilogue(accs, ex, outs, ids):
        outs[0][...] = accs[0] * _sigmoid(accs[1])

    u = _split_proj(h, w_in.astype(BF16), 2, epilogue, (F32,))[0]
    ch = u.shape[1]
    u = _causal_conv(u.reshape(bsz, seq, ch), conv_w, bias=conv_b)
    u = _ln_silu(u.reshape(bsz * seq, ch), ln_g, ln_b)
    return _matmul_resid(u, w_out.astype(BF16), x, gate, seq, tm=1024, tn=512, tk=4096)


def _rglru_mixer(h, x, gate, w_in, conv_w, conv_b, w_a, b_a, w_x, b_x, lam, w_out, *, bsz, seq):
    def epilogue(accs, ex, outs, ids):
        outs[0][...] = _gelu_tanh(accs[0]).astype(BF16)
        outs[1][...] = accs[1]

    gate_br, x_br = _split_proj(h, w_in.astype(BF16), 2, epilogue, (BF16, F32))
    width = x_br.shape[1]
    xc = _causal_conv(x_br.reshape(bsz, seq, width), conv_w, bias=conv_b)
    y = _rglru(xc.reshape(bsz * seq, width), gate_br, w_a.astype(BF16), w_x.astype(BF16), b_a, b_x, lam, seq)
    return _matmul_resid(y, w_out.astype(BF16), x, gate, seq, tm=1024, tn=512, tk=4096)


def _dense_ffn(h, x, gate, w13, w2, *, seq):
    act = _swiglu_act(h, w13.astype(BF16)[None], None, tm=1024, tn=256)
    return _matmul_resid(act, w2.astype(BF16), x, gate, seq, tm=1024, tn=512, tk=4096)


def _moe_ffn(h, gates, x, gate, w13, w2, *, seq):
    n_e, f, d = w2.shape
    act = _swiglu_act(h, w13.astype(BF16), gates, tm=1024, tn=256)
    return _matmul_resid(act, w2.astype(BF16).reshape(n_e * f, d), x, gate, seq, tm=1024, tn=512, tk=4096)


def kernel(x, c, ada_w, ada_b, ada_table, norm_mix, norm_ffn, norm_final, rel_bias, att_w_in, att_g_cq, att_g_ckv, att_w_uq, att_w_uk, att_w_uv, att_w_qidx, att_w_out, sconv_w_in, sconv_conv_w, sconv_w_out, conf_w_in, conf_conv_w, conf_conv_b, conf_ln_g, conf_ln_b, conf_w_out, lru_w_in, lru_conv_w, lru_conv_b, lru_w_a, lru_b_a, lru_w_x, lru_b_x, lru_lambda, lru_w_out, ffn_w13, ffn_w2, moe_router, moe_w13, moe_w2):
    bsz, seq, d = x.shape
    depth = ada_table.shape[0]
    xf = x.reshape(bsz * seq, d)
    mods = _ada_mod(c, ada_w, ada_b, ada_table)
    for i in range(depth):
        shift_m, scale_m, gate_m, shift_f, scale_f, gate_f = (mods[i][:, k:k + 1, :] for k in range(N_ADA))
        h = _norm_mod(xf, norm_mix[i], shift_m, scale_m, seq)
        kind, j = i % 4, i // 4
        if kind == 0:
            xf = _dsa_attention(h, xf, gate_m, rel_bias, att_w_in[j], att_g_cq[j], att_g_ckv[j], att_w_uq[j],
                                att_w_uk[j], att_w_uv[j], att_w_qidx[j], att_w_out[j], bsz=bsz, seq=seq)
        elif kind == 1:
            xf = _short_conv_mixer(h, xf, gate_m, sconv_w_in[j], sconv_conv_w[j], sconv_w_out[j], bsz=bsz, seq=seq)
        elif kind == 2:
            xf = _conformer_mixer(h, xf, gate_m, conf_w_in[j], conf_conv_w[j], conf_conv_b[j], conf_ln_g[j],
                                  conf_ln_b[j], conf_w_out[j], bsz=bsz, seq=seq)
        else:
            xf = _rglru_mixer(h, xf, gate_m, lru_w_in[j], lru_conv_w[j], lru_conv_b[j], lru_w_a[j], lru_b_a[j],
                              lru_w_x[j], lru_b_x[j], lru_lambda[j], lru_w_out[j], bsz=bsz, seq=seq)
        if i % 2 == 0:
            h = _norm_mod(xf, norm_ffn[i], shift_f, scale_f, seq)
            xf = _dense_ffn(h, xf, gate_f, ffn_w13[i // 2], ffn_w2[i // 2], seq=seq)
        else:
            h, gates = _norm_mod(xf, norm_ffn[i], shift_f, scale_f, seq, router=moe_router[i // 2])
            xf = _moe_ffn(h, gates, xf, gate_f, moe_w13[i // 2], moe_w2[i // 2], seq=seq)
    return _final_norm(xf, norm_final).reshape(bsz, seq, d)
```

```python
import functools
import math

import jax
import jax.numpy as jnp
from jax import lax
from jax.experimental import pallas as pl
from jax.experimental.pallas import tpu as pltpu

F32 = jnp.float32
BF16 = jnp.bfloat16

NORM_EPS = 1e-6
N_ADA = 6
MOE_TOPK = 2
LRU_C = 8.0
REL_BUCKETS = 32
REL_MAX_DIST = 128
TOPK_MAX = 256

LANES = 128
SUBLANES = 8
Q_BLOCK = 128
KEY_CHUNK = 256
HEAD_GROUP = 8
CONV_HALO = 32
VMEM_LIMIT = 56 * 1024 * 1024
MASK_NEG = -1e30
INT_MIN = -2 ** 31


def _cparams(*sem):
    return pltpu.CompilerParams(dimension_semantics=sem, vmem_limit_bytes=VMEM_LIMIT)


def _sigmoid(x):
    return 1.0 / (1.0 + jnp.exp(-x))


def _silu(x):
    return x * _sigmoid(x)


def _gelu_tanh(x):
    return 0.5 * x * (1.0 + jnp.tanh(math.sqrt(2.0 / math.pi) * (x + 0.044715 * (x * x * x))))


def _rms(x, g):
    return x * lax.rsqrt(jnp.mean(x * x, axis=-1, keepdims=True) + NORM_EPS) * g


def _mm_kernel(*refs, n_w, n_ex, n_out, nk, epilogue, a_fn, w_fn):
    a_ref = refs[0]
    w_refs = refs[1:1 + n_w]
    ex_refs = refs[1 + n_w:1 + n_w + n_ex]
    out_refs = refs[1 + n_w + n_ex:1 + n_w + n_ex + n_out]
    acc_refs = refs[1 + n_w + n_ex + n_out:]
    ids = (pl.program_id(0), pl.program_id(1))
    a = a_fn(a_ref[...])
    if nk == 1:
        accs = [jnp.dot(a, w_fn(w[...]), preferred_element_type=F32) for w in w_refs]
        epilogue(accs, ex_refs, out_refs, ids)
        return
    k = pl.program_id(2)

    @pl.when(k == 0)
    def _():
        for acc in acc_refs:
            acc[...] = jnp.zeros_like(acc)

    for acc, w in zip(acc_refs, w_refs):
        acc[...] += jnp.dot(a, w_fn(w[...]), preferred_element_type=F32)

    @pl.when(k == nk - 1)
    def _():
        epilogue([acc[...] for acc in acc_refs], ex_refs, out_refs, ids)


def _identity(v):
    return v


def _mm(a, a_spec, ws, extras, outs, epilogue, *, grid, name, acc_shapes=(), a_fn=_identity, w_fn=_identity):
    nk = grid[2] if len(grid) == 3 else 1
    kern = functools.partial(_mm_kernel, n_w=len(ws), n_ex=len(extras), n_out=len(outs), nk=nk,
                             epilogue=epilogue, a_fn=a_fn, w_fn=w_fn)
    sem = ("parallel", "parallel") + (("arbitrary",) if len(grid) == 3 else ())
    res = pl.pallas_call(
        kern,
        grid=grid,
        in_specs=[a_spec] + [s for _, s in ws] + [s for _, s in extras],
        out_specs=[s for _, s in outs],
        out_shape=[o for o, _ in outs],
        scratch_shapes=[pltpu.VMEM(s, F32) for s in acc_shapes] if nk > 1 else [],
        compiler_params=_cparams(*sem), name=name,
    )(a, *[w for w, _ in ws], *[e for e, _ in extras])
    return res


def _ep_resid(accs, ex, outs, ids):
    x_ref, gate_ref = ex
    outs[0][...] = x_ref[...] + gate_ref[0] * accs[0]


def _matmul_resid(a, w, x, gate, seq, *, name, tm=1024, tn=512, tk=4096):
    t, kdim = a.shape
    n = w.shape[1]
    tm, tn, tk = min(tm, seq), min(tn, n), min(tk, kdim)
    nk = kdim // tk
    grid = (t // tm, n // tn) + ((nk,) if nk > 1 else ())
    rows_per_batch = seq // tm
    out = _mm(
        a, pl.BlockSpec((tm, tk), lambda i, j, *k: (i, k[0] if k else 0)),
        [(w, pl.BlockSpec((tk, tn), lambda i, j, *k: (k[0] if k else 0, j)))],
        [(x, pl.BlockSpec((tm, tn), lambda i, j, *k: (i, j))),
         (gate, pl.BlockSpec((1, 1, tn), lambda i, j, *k: (i // rows_per_batch, 0, j)))],
        [(jax.ShapeDtypeStruct((t, n), F32), pl.BlockSpec((tm, tn), lambda i, j, *k: (i, j)))],
        _ep_resid, grid=grid, acc_shapes=[(tm, tn)], name=name)
    return out[0]


def _swiglu_act(h, w13, *, tm=1024, tn=256):
    t, d = h.shape
    f = w13.shape[1] // 2
    tm, tn = min(tm, t), min(tn, f)
    nf = f // tn

    def epilogue(accs, ex, outs, ids):
        outs[0][...] = (_silu(accs[0]) * accs[1]).astype(BF16)

    out = _mm(
        h, pl.BlockSpec((tm, d), lambda i, j: (i, 0)),
        [(w13, pl.BlockSpec((d, tn), lambda i, j: (0, j))),
         (w13, pl.BlockSpec((d, tn), lambda i, j: (0, j + nf)))],
        [],
        [(jax.ShapeDtypeStruct((t, f), BF16), pl.BlockSpec((tm, tn), lambda i, j: (i, j)))],
        epilogue, grid=(t // tm, nf), name="ffn_swiglu")
    return out[0]


MOE_ROW_TILE = 512
MOE_DMA_TILE = 256
SEG_PAD = SUBLANES


class _SegLayout:
    def __init__(self, d):
        self.n_seg = d // LANES
        self.slab = -(-self.n_seg // SUBLANES) * SUBLANES
        self.pitch = self.slab + SEG_PAD

    def seg(self, s, n_rows):
        return pl.ds(s, n_rows, stride=self.pitch)

    def slab_of(self, row):
        return pl.ds(pl.multiple_of(row * self.pitch, SUBLANES), self.slab)

    def store(self, ref, value):
        n_rows = value.shape[0]
        for s in range(self.n_seg):
            ref[self.seg(s, n_rows), :] = value[:, s * LANES:(s + 1) * LANES]
        for s in range(self.n_seg, self.pitch):
            ref[self.seg(s, n_rows), :] = jnp.zeros((n_rows, LANES), ref.dtype)

    def row_copy(self, src, src_row, dst, dst_row, sem):
        return pltpu.make_async_copy(src.at[self.slab_of(src_row)], dst.at[self.slab_of(dst_row)], sem)


def _dispatch_kernel(pos_ref, h_hbm, zero_hbm, out_hbm, sem, *, n_tok, tile, lay):
    del zero_hbm
    base = pl.program_id(0) * tile

    def copies(i):
        tok = base + i
        return [lay.row_copy(h_hbm, tok, out_hbm, pos_ref[k * n_tok + tok], sem) for k in range(MOE_TOPK)]

    def start(i, carry):
        for cp in copies(i):
            cp.start()
        return carry

    def wait(i, carry):
        for cp in copies(i):
            cp.wait()
        return carry

    lax.fori_loop(0, tile, start, 0)
    lax.fori_loop(0, tile, wait, 0)


def _moe_dispatch(h_seg, pos, n_rows, lay):
    t = h_seg.shape[0] // lay.pitch
    tile = min(MOE_DMA_TILE, t)
    any_spec = pl.BlockSpec(memory_space=pl.ANY)
    return pl.pallas_call(
        functools.partial(_dispatch_kernel, n_tok=t, tile=tile, lay=lay),
        grid_spec=pltpu.PrefetchScalarGridSpec(
            num_scalar_prefetch=1, grid=(t // tile,), in_specs=[any_spec, any_spec], out_specs=any_spec,
            scratch_shapes=[pltpu.SemaphoreType.DMA(())]),
        out_shape=jax.ShapeDtypeStruct((n_rows * lay.pitch, LANES), h_seg.dtype),
        input_output_aliases={2: 0},
        compiler_params=_cparams("arbitrary"), name="moe_dispatch")(
            pos, h_seg, jnp.zeros((n_rows * lay.pitch, LANES), h_seg.dtype))


def _grouped_swiglu_kernel(te_ref, nu_ref, a_ref, wg_ref, wu_ref, out_ref, a_scr, *, tm, lay):
    active = pl.program_id(1) < nu_ref[0]

    @pl.when(active)
    def _():
        for s in range(lay.n_seg):
            a_scr[:, s * LANES:(s + 1) * LANES] = a_ref[lay.seg(s, tm), :].astype(BF16)
        a = a_scr[...]
        g = jnp.dot(a, wg_ref[...], preferred_element_type=F32)
        u = jnp.dot(a, wu_ref[...], preferred_element_type=F32)
        out_ref[...] = (_silu(g) * u).astype(BF16)

    @pl.when(jnp.logical_not(active))
    def _():
        out_ref[...] = jnp.zeros_like(out_ref)


def _grouped_swiglu(a_seg, w13, tile_expert, n_used, lay, *, tn=512):
    rows, kdim = a_seg.shape[0] // lay.pitch, w13.shape[1]
    f = w13.shape[2] // 2
    tm, tn = MOE_ROW_TILE, min(tn, f)
    nb = f // tn
    return pl.pallas_call(
        functools.partial(_grouped_swiglu_kernel, tm=tm, lay=lay),
        grid_spec=pltpu.PrefetchScalarGridSpec(
            num_scalar_prefetch=2, grid=(nb, rows // tm),
            in_specs=[pl.BlockSpec((tm * lay.pitch, LANES), lambda j, r, te, nu: (r, 0)),
                      pl.BlockSpec((None, kdim, tn), lambda j, r, te, nu: (te[r], 0, j)),
                      pl.BlockSpec((None, kdim, tn), lambda j, r, te, nu: (te[r], 0, j + nb))],
            out_specs=pl.BlockSpec((tm, tn), lambda j, r, te, nu: (r, j)),
            scratch_shapes=[pltpu.VMEM((tm, kdim), BF16)]),
        out_shape=jax.ShapeDtypeStruct((rows, f), BF16),
        compiler_params=_cparams("parallel", "parallel"), name="moe_swiglu")(
            tile_expert, n_used, a_seg, w13, w13)


def _grouped_down_kernel(te_ref, nu_ref, a_ref, w_ref, out_ref, acc_ref, *, nk, lay):
    active = pl.program_id(0) < nu_ref[0]
    k = pl.program_id(1)

    @pl.when(active & (k == 0))
    def _():
        acc_ref[...] = jnp.zeros_like(acc_ref)

    @pl.when(active)
    def _():
        acc_ref[...] += jnp.dot(a_ref[...], w_ref[...], preferred_element_type=F32)

    @pl.when(active & (k == nk - 1))
    def _():
        lay.store(out_ref, acc_ref[...])

    @pl.when(jnp.logical_not(active) & (k == nk - 1))
    def _():
        out_ref[...] = jnp.zeros_like(out_ref)


def _grouped_down(a, w2, tile_expert, n_used, lay, *, tk=512):
    rows, kdim = a.shape
    d = w2.shape[2]
    tm, tk = MOE_ROW_TILE, min(tk, kdim)
    nk = kdim // tk
    return pl.pallas_call(
        functools.partial(_grouped_down_kernel, nk=nk, lay=lay),
        grid_spec=pltpu.PrefetchScalarGridSpec(
            num_scalar_prefetch=2, grid=(rows // tm, nk),
            in_specs=[pl.BlockSpec((tm, tk), lambda r, k, te, nu: (r, k)),
                      pl.BlockSpec((None, tk, d), lambda r, k, te, nu: (te[r], k, 0))],
            out_specs=pl.BlockSpec((tm * lay.pitch, LANES), lambda r, k, te, nu: (r, 0)),
            scratch_shapes=[pltpu.VMEM((tm, d), F32)]),
        out_shape=jax.ShapeDtypeStruct((rows * lay.pitch, LANES), F32),
        compiler_params=_cparams("parallel", "arbitrary"), name="moe_down")(tile_expert, n_used, a, w2)


def _combine_kernel(pos_ref, y_hbm, x_ref, route_ref, gate_ref, out_ref, buf0, buf1, sem, *, n_tok, tile, lay):
    base = pl.program_id(0) * tile
    bufs = (buf0, buf1)

    def copies(i):
        return [lay.row_copy(y_hbm, pos_ref[k * n_tok + base + i], bufs[k], i, sem) for k in range(MOE_TOPK)]

    def start(i, carry):
        for cp in copies(i):
            cp.start()
        return carry

    def wait(i, carry):
        for cp in copies(i):
            cp.wait()
        return carry

    lax.fori_loop(0, tile, start, 0)
    lax.fori_loop(0, tile, wait, 0)
    route = route_ref[...]
    w0 = route[:, ROUTE_WEIGHT:ROUTE_WEIGHT + 1]
    w1 = route[:, ROUTE_WEIGHT + 1:ROUTE_WEIGHT + 2]
    for s in range(lay.n_seg):
        cols = slice(s * LANES, (s + 1) * LANES)
        y = w0 * buf0[lay.seg(s, tile), :] + w1 * buf1[lay.seg(s, tile), :]
        out_ref[:, cols] = x_ref[:, cols] + gate_ref[0][:, cols] * y


def _moe_combine(y_seg, pos, x, route, gate, seq, lay):
    t, d = x.shape
    tile = min(MOE_DMA_TILE, seq)
    rpb = seq // tile
    row = pl.BlockSpec((tile, d), lambda i, p: (i, 0))
    buf = pltpu.VMEM((tile * lay.pitch, LANES), F32)
    return pl.pallas_call(
        functools.partial(_combine_kernel, n_tok=t, tile=tile, lay=lay),
        grid_spec=pltpu.PrefetchScalarGridSpec(
            num_scalar_prefetch=1, grid=(t // tile,),
            in_specs=[pl.BlockSpec(memory_space=pl.ANY), row,
                      pl.BlockSpec((tile, LANES), lambda i, p: (i, 0)),
                      pl.BlockSpec((1, 1, d), lambda i, p: (i // rpb, 0, 0))],
            out_specs=row,
            scratch_shapes=[buf, buf, pltpu.SemaphoreType.DMA(())]),
        out_shape=jax.ShapeDtypeStruct((t, d), F32),
        compiler_params=_cparams("arbitrary"), name="moe_combine")(pos, y_seg, x, route, gate)


def _ada_mod(c, ada_w, ada_b, ada_table):
    bsz, d = c.shape
    b = 16
    c = jnp.pad(c, ((0, b - bsz), (0, 0)))
    depth = ada_table.shape[0]
    n = ada_w.shape[1]
    tn = min(512, n)

    def epilogue(accs, ex, outs, ids):
        bias_ref, tab_ref = ex
        outs[0][...] = (accs[0] + bias_ref[...])[None] + tab_ref[...]

    out = _mm(
        c, pl.BlockSpec((b, d), lambda i, j: (0, 0)),
        [(ada_w, pl.BlockSpec((d, tn), lambda i, j: (0, j)))],
        [(ada_b.reshape(1, n), pl.BlockSpec((1, tn), lambda i, j: (0, j))),
         (ada_table.reshape(depth, 1, n), pl.BlockSpec((depth, 1, tn), lambda i, j: (0, 0, j)))],
        [(jax.ShapeDtypeStruct((depth, b, n), F32), pl.BlockSpec((depth, b, tn), lambda i, j: (0, 0, j)))],
        epilogue, grid=(1, n // tn), name="ada_mod",
        a_fn=lambda v: _silu(v).astype(BF16), w_fn=lambda v: v.astype(BF16))
    return out[0][:, :bsz].reshape(depth, bsz, N_ADA, d)


def _norm_kernel(*refs, modulate, n_experts):
    if not modulate:
        x_ref, g_ref, out_ref = refs
        out_ref[...] = _rms(x_ref[...], g_ref[...])
        return
    x_ref, g_ref, sh_ref, sc_ref = refs[:4]
    h = _rms(x_ref[...], g_ref[...]) * (1.0 + sc_ref[0]) + sh_ref[0]
    if not n_experts:
        refs[4][...] = h.astype(BF16)
        return
    r_ref, hseg_ref, route_ref, count_ref, carry_ref = refs[4:]

    @pl.when(pl.program_id(0) == 0)
    def _():
        carry_ref[...] = jnp.zeros_like(carry_ref)

    _SegLayout(h.shape[1]).store(hseg_ref, h)
    logits = jnp.dot(h, r_ref[...], preferred_element_type=F32, precision=lax.Precision.HIGHEST)
    lane = lax.broadcasted_iota(jnp.int32, logits.shape, 1)
    lg = jnp.where(lane < n_experts, logits, -jnp.inf)
    m1 = jnp.max(lg, axis=1, keepdims=True)
    i1 = jnp.min(jnp.where(lg == m1, lane, LANES), axis=1, keepdims=True)
    lg2 = jnp.where(lane == i1, -jnp.inf, lg)
    m2 = jnp.max(lg2, axis=1, keepdims=True)
    i2 = jnp.min(jnp.where(lg2 == m2, lane, LANES), axis=1, keepdims=True)
    e2 = jnp.exp(m2 - m1)
    w1 = 1.0 / (1.0 + e2)
    sel = jnp.where((lane == i1) | (lane == i2), 1.0, 0.0)
    tm = sel.shape[0]
    lower = lax.broadcasted_iota(jnp.int32, (tm, tm), 0) >= lax.broadcasted_iota(jnp.int32, (tm, tm), 1)
    cum = jnp.dot(jnp.where(lower, 1.0, 0.0).astype(BF16), sel.astype(BF16), preferred_element_type=F32)
    rank = cum - sel + carry_ref[...]
    r1 = jnp.sum(jnp.where(lane == i1, rank, 0.0), axis=1, keepdims=True)
    r2 = jnp.sum(jnp.where(lane == i2, rank, 0.0), axis=1, keepdims=True)
    carry_ref[...] += jnp.sum(sel, axis=0, keepdims=True)
    count_ref[...] = carry_ref[...]
    cols = (i1.astype(F32), i2.astype(F32), r1, r2, w1, e2 * w1)
    table = jnp.zeros_like(logits)
    for k, col in enumerate(cols):
        table = jnp.where(lane == k, col, table)
    route_ref[...] = table


ROUTE_EXPERT, ROUTE_RANK, ROUTE_WEIGHT = 0, 2, 4


def _norm_mod(x, g, shift, scale, seq, router=None, *, tm=256):
    t, d = x.shape
    tm = min(tm, seq)
    rpb = seq // tm
    row = pl.BlockSpec((tm, d), lambda i: (i, 0))
    vec = pl.BlockSpec((1, d), lambda i: (0, 0))
    per_batch = pl.BlockSpec((1, 1, d), lambda i: (i // rpb, 0, 0))
    ins = [x, g.reshape(1, d), shift, scale]
    in_specs = [row, vec, per_batch, per_batch]
    out_shape = [jax.ShapeDtypeStruct((t, d), BF16)]
    out_specs = [row]
    n_experts = 0
    scratch = []
    if router is not None:
        n_experts = router.shape[1]
        pitch = _SegLayout(d).pitch
        out_shape = [jax.ShapeDtypeStruct((t * pitch, LANES), F32)]
        out_specs = [pl.BlockSpec((tm * pitch, LANES), lambda i: (i, 0))]
        ins.append(jnp.pad(router, ((0, 0), (0, LANES - n_experts))))
        in_specs.append(pl.BlockSpec((d, LANES), lambda i: (0, 0)))
        out_shape += [jax.ShapeDtypeStruct((t, LANES), F32), jax.ShapeDtypeStruct((1, LANES), F32)]
        out_specs += [pl.BlockSpec((tm, LANES), lambda i: (i, 0)), pl.BlockSpec((1, LANES), lambda i: (0, 0))]
        scratch = [pltpu.VMEM((1, LANES), F32)]
    res = pl.pallas_call(
        functools.partial(_norm_kernel, modulate=True, n_experts=n_experts),
        grid=(t // tm,), in_specs=in_specs, out_specs=out_specs, out_shape=out_shape, scratch_shapes=scratch,
        compiler_params=_cparams("arbitrary" if router is not None else "parallel"),
        name="norm_router" if router is not None else "norm_mod")(*ins)
    return res if router is not None else res[0]


def _final_norm(x, g, *, tm=256):
    t, d = x.shape
    tm = min(tm, t)
    row = pl.BlockSpec((tm, d), lambda i: (i, 0))
    return pl.pallas_call(
        functools.partial(_norm_kernel, modulate=False, n_experts=0),
        grid=(t // tm,), in_specs=[row, pl.BlockSpec((1, d), lambda i: (0, 0))], out_specs=row,
        out_shape=jax.ShapeDtypeStruct((t, d), F32), compiler_params=_cparams("parallel"),
        name="final_norm")(x, g.reshape(1, d))


def _conv_kernel(*refs, width, ts, tc, has_bias, has_mul):
    cur_ref, halo_ref, w_ref = refs[:3]
    rest = list(refs[3:])
    b_ref = rest.pop(0) if has_bias else None
    mul_ref = rest.pop(0) if has_mul else None
    out_ref, buf = rest
    first = pl.program_id(1) == 0
    buf[0:CONV_HALO, :] = jnp.where(first, 0.0, halo_ref[0])
    buf[CONV_HALO:CONV_HALO + ts, :] = cur_ref[0]
    rows = 32
    for r in range(0, ts, rows):
        acc = None
        for k in range(width):
            term = w_ref[k:k + 1, :] * buf[pl.ds(CONV_HALO + r - (width - 1) + k, rows), :]
            acc = term if acc is None else acc + term
        if has_bias:
            acc = acc + b_ref[...]
        if has_mul:
            acc = acc * mul_ref[0, r:r + rows, :].astype(F32)
        out_ref[0, r:r + rows, :] = acc.astype(out_ref.dtype)


def _causal_conv(x, conv_w, bias=None, mul=None, out_dtype=F32, *, ts=128, tc=512):
    b, s, ch = x.shape
    width = conv_w.shape[0]
    assert width - 1 <= CONV_HALO
    ts, tc = min(ts, s), min(tc, ch)
    hpb = ts // CONV_HALO
    blk = pl.BlockSpec((1, ts, tc), lambda bi, si, ci: (bi, si, ci))
    ins = [x, x, conv_w]
    in_specs = [blk,
                pl.BlockSpec((1, CONV_HALO, tc), lambda bi, si, ci: (bi, jnp.maximum(si * hpb - 1, 0), ci)),
                pl.BlockSpec((width, tc), lambda bi, si, ci: (0, ci))]
    if bias is not None:
        ins.append(bias.reshape(1, ch))
        in_specs.append(pl.BlockSpec((1, tc), lambda bi, si, ci: (0, ci)))
    if mul is not None:
        ins.append(mul)
        in_specs.append(blk)
    return pl.pallas_call(
        functools.partial(_conv_kernel, width=width, ts=ts, tc=tc, has_bias=bias is not None,
                          has_mul=mul is not None),
        grid=(b, s // ts, ch // tc), in_specs=in_specs, out_specs=blk,
        out_shape=jax.ShapeDtypeStruct((b, s, ch), out_dtype),
        scratch_shapes=[pltpu.VMEM((CONV_HALO + ts, tc), F32)],
        compiler_params=_cparams("parallel", "parallel", "parallel"), name=f"causal_conv{width}")(*ins)


def _ln_silu_kernel(x_ref, g_ref, b_ref, out_ref):
    x = x_ref[...]
    mu = jnp.mean(x, axis=-1, keepdims=True)
    xc = x - mu
    var = jnp.mean(xc * xc, axis=-1, keepdims=True)
    y = xc * lax.rsqrt(var + NORM_EPS) * g_ref[...] + b_ref[...]
    out_ref[...] = _silu(y).astype(BF16)


def _ln_silu(x, g, b, *, tm=256):
    t, d = x.shape
    tm = min(tm, t)
    row = pl.BlockSpec((tm, d), lambda i: (i, 0))
    vec = pl.BlockSpec((1, d), lambda i: (0, 0))
    return pl.pallas_call(
        _ln_silu_kernel, grid=(t // tm,), in_specs=[row, vec, vec], out_specs=row,
        out_shape=jax.ShapeDtypeStruct((t, d), BF16), compiler_params=_cparams("parallel"), name="ln_silu")(
            x, g.reshape(1, d), b.reshape(1, d))


def _rglru_kernel(xc_ref, gate_ref, wa_ref, wx_ref, ba_ref, bx_ref, lam_ref, out_ref,
                  a_scr, u_scr, h_scr, *, ts, n_heads, blk, cw):
    @pl.when(pl.program_id(1) == 0)
    def _():
        h_scr[...] = jnp.zeros_like(h_scr)

    neg_lam = -lam_ref[...]
    softplus = jnp.maximum(neg_lam, 0.0) + jnp.log1p(jnp.exp(-jnp.abs(neg_lam)))
    for hd in range(n_heads):
        sl = slice(hd * blk, (hd + 1) * blk)
        xh = xc_ref[:, sl]
        xb = xh.astype(BF16)
        r = _sigmoid(jnp.dot(xb, wa_ref[hd], preferred_element_type=F32) + ba_ref[:, sl])
        i_g = _sigmoid(jnp.dot(xb, wx_ref[hd], preferred_element_type=F32) + bx_ref[:, sl])
        log_a = -LRU_C * r * softplus[:, sl]
        a_scr[:, sl] = jnp.exp(log_a)
        u_scr[:, sl] = jnp.sqrt(1.0 - jnp.exp(2.0 * log_a)) * (i_g * xh)

    row = lax.broadcasted_iota(jnp.int32, (SUBLANES, cw), 0)
    width = n_heads * blk
    for c0 in range(0, width, cw):
        def body(g, h_prev, c0=c0):
            r0 = pl.multiple_of(g * SUBLANES, SUBLANES)
            a8 = a_scr[pl.ds(r0, SUBLANES), c0:c0 + cw]
            b8 = u_scr[pl.ds(r0, SUBLANES), c0:c0 + cw]
            for d in (1, 2, 4):
                keep = row >= d
                b8 = jnp.where(keep, a8 * pltpu.roll(b8, d, axis=0) + b8, b8)
                a8 = jnp.where(keep, a8 * pltpu.roll(a8, d, axis=0), a8)
            h8 = b8 + a8 * h_prev
            u_scr[pl.ds(r0, SUBLANES), c0:c0 + cw] = h8
            return h8[SUBLANES - 1:SUBLANES, :]

        h_scr[:, c0:c0 + cw] = lax.fori_loop(0, ts // SUBLANES, body, h_scr[:, c0:c0 + cw])
    out_ref[...] = (u_scr[...] * gate_ref[...].astype(F32)).astype(BF16)


def _rglru(xc, gate, w_a, w_x, b_a, b_x, lam, seq, *, ts=256):
    t, width = xc.shape
    n_heads, blk, _ = w_a.shape
    ts = min(ts, seq)
    nts = seq // ts
    row = pl.BlockSpec((ts, width), lambda b, j: (b * nts + j, 0))
    wsp = pl.BlockSpec((n_heads, blk, blk), lambda b, j: (0, 0, 0))
    vec = pl.BlockSpec((1, width), lambda b, j: (0, 0))
    return pl.pallas_call(
        functools.partial(_rglru_kernel, ts=ts, n_heads=n_heads, blk=blk, cw=min(1024, width)),
        grid=(t // seq, nts), in_specs=[row, row, wsp, wsp, vec, vec, vec], out_specs=row,
        out_shape=jax.ShapeDtypeStruct((t, width), BF16),
        scratch_shapes=[pltpu.VMEM((ts, width), F32), pltpu.VMEM((ts, width), F32), pltpu.VMEM((1, width), F32)],
        compiler_params=_cparams("parallel", "arbitrary"), name="rglru")(
            xc, gate, w_a, w_x, b_a.reshape(1, width), b_x.reshape(1, width), lam.reshape(1, width))


def _bias_tile_kernel(rb_ref, out_ref):
    hd = pl.program_id(0)
    shape = (Q_BLOCK, 2 * Q_BLOCK)
    dist = Q_BLOCK + lax.broadcasted_iota(jnp.int32, shape, 0) - lax.broadcasted_iota(jnp.int32, shape, 1)
    dist = jnp.maximum(dist, 0)
    max_exact = REL_BUCKETS // 2
    large = max_exact + (jnp.log(jnp.maximum(dist, 1).astype(F32) / max_exact)
                         / math.log(REL_MAX_DIST / max_exact) * (REL_BUCKETS - max_exact)).astype(jnp.int32)
    bucket = jnp.where(dist < max_exact, dist, jnp.minimum(large, REL_BUCKETS - 1))
    tile = jnp.zeros(shape, F32)
    for bkt in range(REL_BUCKETS):
        tile = jnp.where(bucket == bkt, rb_ref[bkt, hd], tile)
    out_ref[0] = tile - rb_ref[REL_BUCKETS - 1, hd]


def _bias_tile(rel_bias):
    n_heads = rel_bias.shape[1]
    return pl.pallas_call(
        _bias_tile_kernel, grid=(n_heads,),
        in_specs=[pl.BlockSpec(memory_space=pltpu.SMEM)],
        out_specs=pl.BlockSpec((1, Q_BLOCK, 2 * Q_BLOCK), lambda h: (h, 0, 0)),
        out_shape=jax.ShapeDtypeStruct((n_heads, Q_BLOCK, 2 * Q_BLOCK), F32),
        compiler_params=_cparams("parallel"), name="dsa_bias_tile")(rel_bias)


def _dsa_kernel(qidx_ref, wcol_ref, kidx_ref, ckv_ref, qlat_ref, bnear_ref, out_ref,
                score_ref, key_ref, madd_ref, z_ref, mb_ref, acc_ref, *, n_slab, k_sel, idx_heads, kv_dim):
    qb = pl.program_id(1)
    rows = HEAD_GROUP * Q_BLOCK
    nt = (((1,), (1,)), ((), ()))
    n_chunks = (qb + 2) // 2

    @pl.when(pl.program_id(2) == 0)
    def _select():
        def chunk_body(c, carry):
            keys = kidx_ref[0, pl.ds(pl.multiple_of(c * KEY_CHUNK, KEY_CHUNK), KEY_CHUNK), :]

            def head_body(hg, acc):
                h0 = pl.multiple_of(hg * HEAD_GROUP, HEAD_GROUP)
                q = qidx_ref[0, pl.ds(h0, HEAD_GROUP)].reshape(rows, LANES)
                dots = lax.dot_general(q, keys, nt, preferred_element_type=F32)
                dots = jnp.maximum(dots, 0.0).reshape(HEAD_GROUP, Q_BLOCK, KEY_CHUNK)
                return acc + jnp.sum(dots * wcol_ref[0, pl.ds(h0, HEAD_GROUP)], axis=0)

            acc = lax.fori_loop(0, idx_heads // HEAD_GROUP, head_body, jnp.zeros((Q_BLOCK, KEY_CHUNK), F32),
                                unroll=2)
            score_ref[2 * c] = acc[:, :LANES]
            score_ref[2 * c + 1] = acc[:, LANES:]
            return carry

        lax.fori_loop(0, n_chunks, chunk_body, 0)

        def zero_body(s, carry):
            score_ref[s] = jnp.zeros((Q_BLOCK, LANES), F32)
            return carry

        lax.fori_loop(2 * n_chunks, n_slab, zero_body, 0)

        shape = (n_slab, Q_BLOCK, LANES)
        bits = pltpu.bitcast(score_ref[...], jnp.int32)
        key = jnp.where(bits >= 0, bits, bits ^ jnp.int32(0x7FFFFFFF))
        key_pos = lax.broadcasted_iota(jnp.int32, shape, 0) * LANES + lax.broadcasted_iota(jnp.int32, shape, 2)
        q_pos = qb * Q_BLOCK + lax.broadcasted_iota(jnp.int32, shape, 1)
        causal = key_pos <= q_pos
        key = jnp.where(causal, key, INT_MIN)
        key_ref[...] = key

        def bit_body(it, thr):
            cand = thr + lax.shift_left(jnp.int32(1), 31 - it)
            cand_b = jnp.broadcast_to(cand, (Q_BLOCK, LANES))

            def count_body(c, cnt):
                cnt = cnt + jnp.where(key_ref[2 * c] >= cand_b, 1.0, 0.0)
                return cnt + jnp.where(key_ref[2 * c + 1] >= cand_b, 1.0, 0.0)

            cnt = lax.fori_loop(0, n_chunks, count_body, jnp.zeros((Q_BLOCK, LANES), F32))
            return jnp.where(jnp.sum(cnt, axis=1, keepdims=True) >= k_sel, cand, thr)

        thr = lax.fori_loop(0, 32, bit_body, jnp.full((Q_BLOCK, 1), INT_MIN, jnp.int32))
        madd_ref[0] = jnp.full((Q_BLOCK, LANES), MASK_NEG, F32)
        madd_ref[1:] = jnp.where(causal & (key >= thr[None]), 0.0, MASK_NEG)

    q = qlat_ref[0].reshape(rows, kv_dim)
    n_far = qb // 2
    near_row = qb * Q_BLOCK

    def logits_of(kv):
        return lax.dot_general(q, kv, nt, preferred_element_type=F32).reshape(HEAD_GROUP, Q_BLOCK, KEY_CHUNK)

    def fold(z):
        return jnp.maximum(z[..., :LANES], z[..., LANES:])

    def far_logits(c, m_tile):
        kv = ckv_ref[0, pl.ds(pl.multiple_of(Q_BLOCK + c * KEY_CHUNK, Q_BLOCK), KEY_CHUNK), :]
        second = jnp.where(2 * c + 2 < qb, madd_ref[2 * c + 2], MASK_NEG)
        z = logits_of(kv) + jnp.concatenate([madd_ref[2 * c + 1], second], axis=1)[None]
        z_ref[c] = z
        return jnp.maximum(m_tile, fold(z))

    m_tile = lax.fori_loop(0, n_far, far_logits, jnp.full((HEAD_GROUP, Q_BLOCK, LANES), MASK_NEG, F32))
    kv_near = ckv_ref[0, pl.ds(pl.multiple_of(near_row, Q_BLOCK), KEY_CHUNK), :]
    z = (logits_of(kv_near) + jnp.concatenate([madd_ref[qb], madd_ref[qb + 1]], axis=1)[None]) + bnear_ref[...]
    z_ref[n_far] = z
    m_tile = jnp.maximum(m_tile, fold(z))
    mb_ref[...] = jnp.broadcast_to(jnp.max(m_tile, axis=-1, keepdims=True), mb_ref.shape)
    acc_ref[...] = jnp.zeros(acc_ref.shape, F32)

    def weigh(c, l_tile):
        row0 = jnp.where(c < n_far, Q_BLOCK + c * KEY_CHUNK, near_row)
        kv = ckv_ref[0, pl.ds(pl.multiple_of(row0, Q_BLOCK), KEY_CHUNK), :]
        p = jnp.exp(z_ref[c] - mb_ref[...])
        pv = jnp.dot(p.astype(BF16).reshape(rows, KEY_CHUNK), kv, preferred_element_type=F32)
        acc_ref[...] += pv.reshape(HEAD_GROUP, Q_BLOCK, kv_dim)
        return l_tile + (p[..., :LANES] + p[..., LANES:])

    l_tile = lax.fori_loop(0, n_far + 1, weigh, jnp.zeros((HEAD_GROUP, Q_BLOCK, LANES), F32))
    out_ref[0] = (acc_ref[...] / jnp.sum(l_tile, axis=-1, keepdims=True)).astype(BF16)


def _dsa_core(qidx, wcol, kidx, ckv_pad, qlat, bnear, *, bsz, seq, k_sel):
    n_qb = seq // Q_BLOCK
    idx_heads = qidx.shape[1]
    n_heads, kv_dim = qlat.shape[1], qlat.shape[3]
    n_groups = n_heads // HEAD_GROUP
    blk_q = lambda b, i, g: (b * n_qb + i, 0, 0, 0)
    grp = lambda b, i, g: (b * n_qb + i, g, 0, 0)
    tile = (HEAD_GROUP, Q_BLOCK, KEY_CHUNK)
    return pl.pallas_call(
        functools.partial(_dsa_kernel, n_slab=n_qb, k_sel=k_sel, idx_heads=idx_heads, kv_dim=kv_dim),
        grid=(bsz, n_qb, n_groups),
        in_specs=[
            pl.BlockSpec((1, idx_heads, Q_BLOCK, LANES), blk_q),
            pl.BlockSpec((1, idx_heads, Q_BLOCK, 1), blk_q),
            pl.BlockSpec((1, seq, LANES), lambda b, i, g: (b, 0, 0)),
            pl.BlockSpec((1, seq + Q_BLOCK, kv_dim), lambda b, i, g: (b, 0, 0)),
            pl.BlockSpec((1, HEAD_GROUP, Q_BLOCK, kv_dim), grp),
            pl.BlockSpec(tile, lambda b, i, g: (g, 0, 0)),
        ],
        out_specs=pl.BlockSpec((1, HEAD_GROUP, Q_BLOCK, kv_dim), grp),
        out_shape=jax.ShapeDtypeStruct((bsz * n_qb, n_heads, Q_BLOCK, kv_dim), BF16),
        scratch_shapes=[
            pltpu.VMEM((n_qb, Q_BLOCK, LANES), F32),
            pltpu.VMEM((n_qb, Q_BLOCK, LANES), jnp.int32),
            pltpu.VMEM((n_qb + 1, Q_BLOCK, LANES), F32),
            pltpu.VMEM((n_qb // 2 + 1,) + tile, F32),
            pltpu.VMEM(tile, F32),
            pltpu.VMEM((HEAD_GROUP, Q_BLOCK, kv_dim), F32),
        ],
        compiler_params=_cparams("parallel", "parallel", "arbitrary"), name="dsa_core")(
            qidx, wcol, kidx, ckv_pad, qlat, bnear)


def _dsa_attention(h, x, gate, rel_bias, w_in, g_cq, g_ckv, w_uq, w_uk, w_uv, w_qidx, w_out, *, bsz, seq):
    t, d = h.shape
    q_lora, kv_lora = g_cq.shape[0], g_ckv.shape[0]
    n_heads, head_dim = w_uk.shape[1], w_uk.shape[2]
    idx_dim = LANES
    idx_heads = w_qidx.shape[1] // idx_dim
    k_sel = min(TOPK_MAX, seq // 4)
    n_qb = seq // Q_BLOCK
    tm = min(512, seq)
    qpt = tm // Q_BLOCK

    w_in = w_in.astype(BF16)
    splits = (0, q_lora, q_lora + kv_lora, q_lora + kv_lora + idx_dim, w_in.shape[1])
    w_parts = [w_in[:, splits[i]:splits[i + 1]] for i in range(4)]
    head_scale = idx_heads ** -0.5 * idx_dim ** -0.5

    def in_epilogue(accs, ex, outs, ids):
        outs[0][...] = _rms(accs[0], ex[0][...]).astype(BF16)
        outs[1][...] = _rms(accs[1], ex[1][...]).astype(BF16)
        outs[2][...] = accs[2].astype(BF16)
        outs[3][...] = accs[3] * head_scale

    full = lambda n: pl.BlockSpec((d, n), lambda i, j: (0, 0))
    rowsp = lambda n: pl.BlockSpec((tm, n), lambda i, j: (i, 0))
    vecsp = lambda n: pl.BlockSpec((1, n), lambda i, j: (0, 0))
    widths = [w.shape[1] for w in w_parts]
    c_q, c_kv, k_idx, w_head = _mm(
        h, pl.BlockSpec((tm, d), lambda i, j: (i, 0)),
        [(w, full(n)) for w, n in zip(w_parts, widths)],
        [(g_cq.reshape(1, q_lora), vecsp(q_lora)), (g_ckv.reshape(1, kv_lora), vecsp(kv_lora))],
        [(jax.ShapeDtypeStruct((t, n), dt), rowsp(n)) for n, dt in zip(widths, (BF16, BF16, BF16, F32))],
        in_epilogue, grid=(t // tm, 1), name="dsa_in_proj")

    hpt = 4
    tn = hpt * idx_dim

    def qidx_epilogue(accs, ex, outs, ids):
        for hh in range(hpt):
            outs[0][:, hh] = accs[0][:, hh * idx_dim:(hh + 1) * idx_dim].reshape(qpt, Q_BLOCK, idx_dim).astype(BF16)

    qidx = _mm(
        c_q, pl.BlockSpec((tm, q_lora), lambda i, j: (i, 0)),
        [(w_qidx.astype(BF16), pl.BlockSpec((q_lora, tn), lambda i, j: (0, j)))], [],
        [(jax.ShapeDtypeStruct((t // Q_BLOCK, idx_heads, Q_BLOCK, idx_dim), BF16),
          pl.BlockSpec((qpt, hpt, Q_BLOCK, idx_dim), lambda i, j: (i, j, 0, 0)))],
        qidx_epilogue, grid=(t // tm, idx_heads // hpt), name="dsa_qidx")[0]

    w_ukt = jnp.transpose(w_uk, (1, 2, 0)).astype(BF16)
    logit_scale = head_dim ** -0.5

    def qlat_kernel(cq_ref, wuq_ref, wuk_ref, out_ref):
        qh = jnp.dot(cq_ref[...], wuq_ref[...], preferred_element_type=F32).astype(BF16)
        ql = jnp.dot(qh, wuk_ref[...], preferred_element_type=F32) * logit_scale
        out_ref[:, 0] = ql.reshape(qpt, Q_BLOCK, kv_lora).astype(BF16)

    qlat = pl.pallas_call(
        qlat_kernel, grid=(t // tm, n_heads),
        in_specs=[pl.BlockSpec((tm, q_lora), lambda i, hd: (i, 0)),
                  pl.BlockSpec((q_lora, head_dim), lambda i, hd: (0, hd)),
                  pl.BlockSpec((None, head_dim, kv_lora), lambda i, hd: (hd, 0, 0))],
        out_specs=pl.BlockSpec((qpt, 1, Q_BLOCK, kv_lora), lambda i, hd: (i, hd, 0, 0)),
        out_shape=jax.ShapeDtypeStruct((t // Q_BLOCK, n_heads, Q_BLOCK, kv_lora), BF16),
        compiler_params=_cparams("parallel", "parallel"), name="dsa_qlat")(c_q, w_uq.astype(BF16), w_ukt)

    wcol = w_head.reshape(bsz * n_qb, Q_BLOCK, idx_heads).transpose(0, 2, 1)[..., None]
    ckv_pad = jnp.pad(c_kv.reshape(bsz, seq, kv_lora), ((0, 0), (Q_BLOCK, 0), (0, 0)))
    bnear = _bias_tile(rel_bias.astype(F32))
    o_lat = _dsa_core(qidx, wcol, k_idx.reshape(bsz, seq, idx_dim), ckv_pad, qlat, bnear,
                      bsz=bsz, seq=seq, k_sel=k_sel)

    w_uvh = jnp.transpose(w_uv, (1, 0, 2)).astype(BF16)

    def uv_kernel(ol_ref, wuv_ref, out_ref):
        out_ref[...] = jnp.dot(ol_ref[:, 0].reshape(tm, kv_lora), wuv_ref[...],
                               preferred_element_type=F32).astype(BF16)

    o = pl.pallas_call(
        uv_kernel, grid=(t // tm, n_heads),
        in_specs=[pl.BlockSpec((qpt, 1, Q_BLOCK, kv_lora), lambda i, hd: (i, hd, 0, 0)),
                  pl.BlockSpec((None, kv_lora, head_dim), lambda i, hd: (hd, 0, 0))],
        out_specs=pl.BlockSpec((tm, head_dim), lambda i, hd: (i, hd)),
        out_shape=jax.ShapeDtypeStruct((t, n_heads * head_dim), BF16),
        compiler_params=_cparams("parallel", "parallel"), name="dsa_uv")(o_lat, w_uvh)
    return _matmul_resid(o, w_out.astype(BF16), x, gate, seq, name="dsa_out_proj")


def _split_proj(h, w, n_parts, epilogue, out_dtypes, *, name, tm=1024, tn=256):
    t, d = h.shape
    n = w.shape[1] // n_parts
    tm, tn = min(tm, t), min(tn, n)
    nb = n // tn
    out_sp = pl.BlockSpec((tm, tn), lambda i, j: (i, j))
    return _mm(
        h, pl.BlockSpec((tm, d), lambda i, j: (i, 0)),
        [(w, pl.BlockSpec((d, tn), lambda i, j, p=p: (0, j + p * nb))) for p in range(n_parts)], [],
        [(jax.ShapeDtypeStruct((t, n), dt), out_sp) for dt in out_dtypes],
        epilogue, grid=(t // tm, nb), name=name)


def _short_conv_mixer(h, x, gate, w_in, conv_w, w_out, *, bsz, seq):
    def epilogue(accs, ex, outs, ids):
        outs[0][...] = accs[0].astype(BF16)
        outs[1][...] = accs[1] * accs[2]

    gate_b, cx = _split_proj(h, w_in.astype(BF16), 3, epilogue, (BF16, F32), name="sconv_in_proj")
    ch = cx.shape[1]
    y = _causal_conv(cx.reshape(bsz, seq, ch), conv_w, mul=gate_b.reshape(bsz, seq, ch), out_dtype=BF16)
    return _matmul_resid(y.reshape(bsz * seq, ch), w_out.astype(BF16), x, gate, seq, name="sconv_out_proj")


def _conformer_mixer(h, x, gate, w_in, conv_w, conv_b, ln_g, ln_b, w_out, *, bsz, seq):
    def epilogue(accs, ex, outs, ids):
        outs[0][...] = accs[0] * _sigmoid(accs[1])

    u = _split_proj(h, w_in.astype(BF16), 2, epilogue, (F32,), name="conf_in_proj")[0]
    ch = u.shape[1]
    u = _causal_conv(u.reshape(bsz, seq, ch), conv_w, bias=conv_b)
    u = _ln_silu(u.reshape(bsz * seq, ch), ln_g, ln_b)
    return _matmul_resid(u, w_out.astype(BF16), x, gate, seq, name="conf_out_proj")


def _rglru_mixer(h, x, gate, w_in, conv_w, conv_b, w_a, b_a, w_x, b_x, lam, w_out, *, bsz, seq):
    def epilogue(accs, ex, outs, ids):
        outs[0][...] = _gelu_tanh(accs[0]).astype(BF16)
        outs[1][...] = accs[1]

    gate_br, x_br = _split_proj(h, w_in.astype(BF16), 2, epilogue, (BF16, F32), name="lru_in_proj")
    width = x_br.shape[1]
    xc = _causal_conv(x_br.reshape(bsz, seq, width), conv_w, bias=conv_b)
    y = _rglru(xc.reshape(bsz * seq, width), gate_br, w_a.astype(BF16), w_x.astype(BF16), b_a, b_x, lam, seq)
    return _matmul_resid(y, w_out.astype(BF16), x, gate, seq, name="lru_out_proj")


def _dense_ffn(h, x, gate, w13, w2, *, seq):
    act = _swiglu_act(h, w13.astype(BF16))
    return _matmul_resid(act, w2.astype(BF16), x, gate, seq, name="ffn_down")


def _moe_ffn(h_seg, route, counts, x, gate, w13, w2, *, seq):
    t = x.shape[0]
    n_e = w13.shape[0]
    tile = MOE_ROW_TILE
    n_rows = MOE_TOPK * t + n_e * tile
    counts = counts[0, :n_e].astype(jnp.int32)
    padded = (counts + tile - 1) // tile * tile
    ends = jnp.cumsum(padded)
    starts = ends - padded
    experts = route[:, ROUTE_EXPERT:ROUTE_EXPERT + MOE_TOPK].astype(jnp.int32)
    ranks = route[:, ROUTE_RANK:ROUTE_RANK + MOE_TOPK].astype(jnp.int32)
    pos = (starts[experts] + ranks).T.reshape(MOE_TOPK * t)
    tile_start = jnp.arange(n_rows // tile, dtype=jnp.int32) * tile
    tile_expert = jnp.minimum(jnp.sum(tile_start[:, None] >= ends[None, :], axis=1), n_e - 1).astype(jnp.int32)
    n_used = (ends[-1:] // tile).astype(jnp.int32)

    lay = _SegLayout(x.shape[1])
    hs = _moe_dispatch(h_seg, pos, n_rows, lay)
    act = _grouped_swiglu(hs, w13.astype(BF16), tile_expert, n_used, lay)
    y = _grouped_down(act, w2.astype(BF16), tile_expert, n_used, lay)
    return _moe_combine(y, pos, x, route, gate, seq, lay)


def kernel(x, c, ada_w, ada_b, ada_table, norm_mix, norm_ffn, norm_final, rel_bias, att_w_in, att_g_cq, att_g_ckv, att_w_uq, att_w_uk, att_w_uv, att_w_qidx, att_w_out, sconv_w_in, sconv_conv_w, sconv_w_out, conf_w_in, conf_conv_w, conf_conv_b, conf_ln_g, conf_ln_b, conf_w_out, lru_w_in, lru_conv_w, lru_conv_b, lru_w_a, lru_b_a, lru_w_x, lru_b_x, lru_lambda, lru_w_out, ffn_w13, ffn_w2, moe_router, moe_w13, moe_w2):
    bsz, seq, d = x.shape
    depth = ada_table.shape[0]
    xf = x.reshape(bsz * seq, d)
    mods = _ada_mod(c, ada_w, ada_b, ada_table)
    for i in range(depth):
        shift_m, scale_m, gate_m, shift_f, scale_f, gate_f = (mods[i][:, k:k + 1, :] for k in range(N_ADA))
        h = _norm_mod(xf, norm_mix[i], shift_m, scale_m, seq)
        kind, j = i % 4, i // 4
        if kind == 0:
            xf = _dsa_attention(h, xf, gate_m, rel_bias, att_w_in[j], att_g_cq[j], att_g_ckv[j], att_w_uq[j],
                                att_w_uk[j], att_w_uv[j], att_w_qidx[j], att_w_out[j], bsz=bsz, seq=seq)
        elif kind == 1:
            xf = _short_conv_mixer(h, xf, gate_m, sconv_w_in[j], sconv_conv_w[j], sconv_w_out[j], bsz=bsz, seq=seq)
        elif kind == 2:
            xf = _conformer_mixer(h, xf, gate_m, conf_w_in[j], conf_conv_w[j], conf_conv_b[j], conf_ln_g[j],
                                  conf_ln_b[j], conf_w_out[j], bsz=bsz, seq=seq)
        else:
            xf = _rglru_mixer(h, xf, gate_m, lru_w_in[j], lru_conv_w[j], lru_conv_b[j], lru_w_a[j], lru_b_a[j],
                              lru_w_x[j], lru_b_x[j], lru_lambda[j], lru_w_out[j], bsz=bsz, seq=seq)
        if i % 2 == 0:
            h = _norm_mod(xf, norm_ffn[i], shift_f, scale_f, seq)
            xf = _dense_ffn(h, xf, gate_f, ffn_w13[i // 2], ffn_w2[i // 2], seq=seq)
        else:
            h, route, counts = _norm_mod(xf, norm_ffn[i], shift_f, scale_f, seq, router=moe_router[i // 2])
            xf = _moe_ffn(h, route, counts, xf, gate_f, moe_w13[i // 2], moe_w2[i // 2], seq=seq)
    return _final_norm(xf, norm_final).reshape(bsz, seq, d)
```

```python
import functools
import math

import jax
import jax.numpy as jnp
from jax import lax
from jax.experimental import pallas as pl
from jax.experimental.pallas import tpu as pltpu

F32 = jnp.float32
BF16 = jnp.bfloat16

NORM_EPS = 1e-6
N_ADA = 6
MOE_TOPK = 2
LRU_C = 8.0
REL_BUCKETS = 32
REL_MAX_DIST = 128
TOPK_MAX = 256

LANES = 128
SUBLANES = 8
Q_BLOCK = 128
KEY_CHUNK = 256
HEAD_GROUP = 8
CONV_HALO = 32
VMEM_LIMIT = 56 * 1024 * 1024
MASK_NEG = -1e30
INT_MIN = -2 ** 31


def _cparams(*sem):
    return pltpu.CompilerParams(dimension_semantics=sem, vmem_limit_bytes=VMEM_LIMIT)


def _sigmoid(x):
    return 1.0 / (1.0 + jnp.exp(-x))


def _silu(x):
    return x * _sigmoid(x)


def _gelu_tanh(x):
    return 0.5 * x * (1.0 + jnp.tanh(math.sqrt(2.0 / math.pi) * (x + 0.044715 * (x * x * x))))


def _rms(x, g):
    return x * lax.rsqrt(jnp.mean(x * x, axis=-1, keepdims=True) + NORM_EPS) * g


def _mm_kernel(*refs, n_w, n_ex, n_out, nk, cast_w, epilogue, a_fn):
    a_ref = refs[0]
    w_refs = refs[1:1 + n_w]
    ex_refs = refs[1 + n_w:1 + n_w + n_ex]
    out_refs = refs[1 + n_w + n_ex:1 + n_w + n_ex + n_out]
    scr_refs = refs[1 + n_w + n_ex + n_out:]
    a = a_fn(a_ref[...])
    if cast_w:
        i, j = pl.program_id(1), pl.program_id(0)

        @pl.when(i == 0)
        def _():
            for scr, w in zip(scr_refs, w_refs):
                scr[...] = w[...].astype(BF16)

        accs = [jnp.dot(a, scr[...], preferred_element_type=F32) for scr in scr_refs]
        epilogue(accs, ex_refs, out_refs, (i, j))
        return
    ids = (pl.program_id(0), pl.program_id(1))
    if nk == 1:
        accs = [jnp.dot(a, w[...], preferred_element_type=F32) for w in w_refs]
        epilogue(accs, ex_refs, out_refs, ids)
        return
    k = pl.program_id(2)

    @pl.when(k == 0)
    def _():
        for acc in scr_refs:
            acc[...] = jnp.zeros_like(acc)

    for acc, w in zip(scr_refs, w_refs):
        acc[...] += jnp.dot(a, w[...], preferred_element_type=F32)

    @pl.when(k == nk - 1)
    def _():
        epilogue([acc[...] for acc in scr_refs], ex_refs, out_refs, ids)


def _identity(v):
    return v


def _mm(a, a_spec, ws, extras, outs, epilogue, *, grid, name, acc_shapes=(), cast_w=False, a_fn=_identity):
    nk = grid[2] if len(grid) == 3 else 1
    in_specs = [a_spec] + [s for _, s in ws] + [s for _, s in extras]
    out_specs = [s for _, s in outs]
    if cast_w:
        assert nk == 1
        swap = lambda s: pl.BlockSpec(s.block_shape, lambda j, i, f=s.index_map: f(i, j))
        in_specs, out_specs = [swap(s) for s in in_specs], [swap(s) for s in out_specs]
        grid, sem = (grid[1], grid[0]), ("parallel", "arbitrary")
        scratch = [pltpu.VMEM(tuple(d for d in s.block_shape if d is not None), BF16) for _, s in ws]
    else:
        sem = ("parallel", "parallel") + (("arbitrary",) if nk > 1 else ())
        scratch = [pltpu.VMEM(s, F32) for s in acc_shapes] if nk > 1 else []
    kern = functools.partial(_mm_kernel, n_w=len(ws), n_ex=len(extras), n_out=len(outs), nk=nk, cast_w=cast_w,
                             epilogue=epilogue, a_fn=a_fn)
    return pl.pallas_call(
        kern, grid=grid, in_specs=in_specs, out_specs=out_specs, out_shape=[o for o, _ in outs],
        scratch_shapes=scratch, compiler_params=_cparams(*sem), name=name,
    )(a, *[w for w, _ in ws], *[e for e, _ in extras])


def _ep_resid(accs, ex, outs, ids):
    x_ref, gate_ref = ex
    outs[0][...] = x_ref[...] + gate_ref[0] * accs[0]


def _matmul_resid(a, w, x, gate, seq, *, name, tm=1024, tn=512, tk=4096):
    t, kdim = a.shape
    n = w.shape[1]
    tm, tn, tk = min(tm, seq), min(tn, n), min(tk, kdim)
    nk = kdim // tk
    grid = (t // tm, n // tn) + ((nk,) if nk > 1 else ())
    rows_per_batch = seq // tm
    out = _mm(
        a, pl.BlockSpec((tm, tk), lambda i, j, *k: (i, k[0] if k else 0)),
        [(w, pl.BlockSpec((tk, tn), lambda i, j, *k: (k[0] if k else 0, j)))],
        [(x, pl.BlockSpec((tm, tn), lambda i, j, *k: (i, j))),
         (gate, pl.BlockSpec((1, 1, tn), lambda i, j, *k: (i // rows_per_batch, 0, j)))],
        [(jax.ShapeDtypeStruct((t, n), F32), pl.BlockSpec((tm, tn), lambda i, j, *k: (i, j)))],
        _ep_resid, grid=grid, acc_shapes=[(tm, tn)], name=name, cast_w=w.dtype == F32)
    return out[0]


def _swiglu_act(h, w13, *, tm=1024, tn=256):
    t, d = h.shape
    f = w13.shape[1] // 2
    tm, tn = min(tm, t), min(tn, f)
    nf = f // tn

    def epilogue(accs, ex, outs, ids):
        outs[0][...] = (_silu(accs[0]) * accs[1]).astype(BF16)

    out = _mm(
        h, pl.BlockSpec((tm, d), lambda i, j: (i, 0)),
        [(w13, pl.BlockSpec((d, tn), lambda i, j: (0, j))),
         (w13, pl.BlockSpec((d, tn), lambda i, j: (0, j + nf)))],
        [],
        [(jax.ShapeDtypeStruct((t, f), BF16), pl.BlockSpec((tm, tn), lambda i, j: (i, j)))],
        epilogue, grid=(t // tm, nf), name="ffn_swiglu", cast_w=True)
    return out[0]


MOE_ROW_TILE = 512
MOE_DMA_TILE = 256
SEG_PAD = SUBLANES


class _SegLayout:
    def __init__(self, d):
        self.n_seg = d // LANES
        self.slab = -(-self.n_seg // SUBLANES) * SUBLANES
        self.pitch = self.slab + SEG_PAD

    def seg(self, s, n_rows):
        return pl.ds(s, n_rows, stride=self.pitch)

    def slab_of(self, row):
        return pl.ds(pl.multiple_of(row * self.pitch, SUBLANES), self.slab)

    def store(self, ref, value):
        n_rows = value.shape[0]
        for s in range(self.n_seg):
            ref[self.seg(s, n_rows), :] = value[:, s * LANES:(s + 1) * LANES]
        for s in range(self.n_seg, self.pitch):
            ref[self.seg(s, n_rows), :] = jnp.zeros((n_rows, LANES), ref.dtype)

    def row_copy(self, src, src_row, dst, dst_row, sem):
        return pltpu.make_async_copy(src.at[self.slab_of(src_row)], dst.at[self.slab_of(dst_row)], sem)


def _dispatch_kernel(pos_ref, h_ref, zero_hbm, out_hbm, sem, *, n_tok, tile, lay):
    del zero_hbm
    base = pl.program_id(0) * tile

    def copies(i):
        return [lay.row_copy(h_ref, i, out_hbm, pos_ref[k * n_tok + base + i], sem) for k in range(MOE_TOPK)]

    def start(i, carry):
        for cp in copies(i):
            cp.start()
        return carry

    def wait(i, carry):
        for cp in copies(i):
            cp.wait()
        return carry

    lax.fori_loop(0, tile, start, 0)
    lax.fori_loop(0, tile, wait, 0)


def _moe_dispatch(h_seg, pos, n_rows, lay):
    t = h_seg.shape[0] // lay.pitch
    tile = min(MOE_DMA_TILE, t)
    any_spec = pl.BlockSpec(memory_space=pl.ANY)
    return pl.pallas_call(
        functools.partial(_dispatch_kernel, n_tok=t, tile=tile, lay=lay),
        grid_spec=pltpu.PrefetchScalarGridSpec(
            num_scalar_prefetch=1, grid=(t // tile,),
            in_specs=[pl.BlockSpec((tile * lay.pitch, LANES), lambda i, p: (i, 0)), any_spec], out_specs=any_spec,
            scratch_shapes=[pltpu.SemaphoreType.DMA(())]),
        out_shape=jax.ShapeDtypeStruct((n_rows * lay.pitch, LANES), h_seg.dtype),
        input_output_aliases={2: 0},
        compiler_params=_cparams("arbitrary"), name="moe_dispatch")(
            pos, h_seg, jnp.zeros((n_rows * lay.pitch, LANES), h_seg.dtype))


def _grouped_swiglu_kernel(te_ref, nu_ref, a_ref, wg_ref, wu_ref, out_ref, a_scr, *, tm, lay):
    active = pl.program_id(1) < nu_ref[0]

    @pl.when(active)
    def _():
        for s in range(lay.n_seg):
            a_scr[:, s * LANES:(s + 1) * LANES] = a_ref[lay.seg(s, tm), :].astype(BF16)
        a = a_scr[...]
        g = jnp.dot(a, wg_ref[...], preferred_element_type=F32)
        u = jnp.dot(a, wu_ref[...], preferred_element_type=F32)
        out_ref[...] = (_silu(g) * u).astype(BF16)

    @pl.when(jnp.logical_not(active))
    def _():
        out_ref[...] = jnp.zeros_like(out_ref)


def _grouped_swiglu(a_seg, w13, tile_expert, n_used, lay, *, tn=512):
    rows, kdim = a_seg.shape[0] // lay.pitch, w13.shape[1]
    f = w13.shape[2] // 2
    tm, tn = MOE_ROW_TILE, min(tn, f)
    nb = f // tn
    return pl.pallas_call(
        functools.partial(_grouped_swiglu_kernel, tm=tm, lay=lay),
        grid_spec=pltpu.PrefetchScalarGridSpec(
            num_scalar_prefetch=2, grid=(nb, rows // tm),
            in_specs=[pl.BlockSpec((tm * lay.pitch, LANES), lambda j, r, te, nu: (r, 0)),
                      pl.BlockSpec((None, kdim, tn), lambda j, r, te, nu: (te[r], 0, j)),
                      pl.BlockSpec((None, kdim, tn), lambda j, r, te, nu: (te[r], 0, j + nb))],
            out_specs=pl.BlockSpec((tm, tn), lambda j, r, te, nu: (r, j)),
            scratch_shapes=[pltpu.VMEM((tm, kdim), BF16)]),
        out_shape=jax.ShapeDtypeStruct((rows, f), BF16),
        compiler_params=_cparams("parallel", "parallel"), name="moe_swiglu")(
            tile_expert, n_used, a_seg, w13, w13)


def _grouped_down_kernel(te_ref, nu_ref, a_ref, w_ref, out_ref, acc_ref, *, nk, lay):
    active = pl.program_id(0) < nu_ref[0]
    k = pl.program_id(1)

    @pl.when(active & (k == 0))
    def _():
        acc_ref[...] = jnp.zeros_like(acc_ref)

    @pl.when(active)
    def _():
        acc_ref[...] += jnp.dot(a_ref[...], w_ref[...], preferred_element_type=F32)

    @pl.when(active & (k == nk - 1))
    def _():
        lay.store(out_ref, acc_ref[...])

    @pl.when(jnp.logical_not(active) & (k == nk - 1))
    def _():
        out_ref[...] = jnp.zeros_like(out_ref)


def _grouped_down(a, w2, tile_expert, n_used, lay, *, tk=512):
    rows, kdim = a.shape
    d = w2.shape[2]
    tm, tk = MOE_ROW_TILE, min(tk, kdim)
    nk = kdim // tk
    return pl.pallas_call(
        functools.partial(_grouped_down_kernel, nk=nk, lay=lay),
        grid_spec=pltpu.PrefetchScalarGridSpec(
            num_scalar_prefetch=2, grid=(rows // tm, nk),
            in_specs=[pl.BlockSpec((tm, tk), lambda r, k, te, nu: (r, k)),
                      pl.BlockSpec((None, tk, d), lambda r, k, te, nu: (te[r], k, 0))],
            out_specs=pl.BlockSpec((tm * lay.pitch, LANES), lambda r, k, te, nu: (r, 0)),
            scratch_shapes=[pltpu.VMEM((tm, d), F32)]),
        out_shape=jax.ShapeDtypeStruct((rows * lay.pitch, LANES), F32),
        compiler_params=_cparams("parallel", "arbitrary"), name="moe_down")(tile_expert, n_used, a, w2)


def _combine_kernel(pos_ref, y_hbm, x_ref, route_ref, gate_ref, out_ref, buf0, buf1, sem, *, n_tok, tile, lay):
    base = pl.program_id(0) * tile
    bufs = (buf0, buf1)

    def copies(i):
        return [lay.row_copy(y_hbm, pos_ref[k * n_tok + base + i], bufs[k], i, sem) for k in range(MOE_TOPK)]

    def start(i, carry):
        for cp in copies(i):
            cp.start()
        return carry

    def wait(i, carry):
        for cp in copies(i):
            cp.wait()
        return carry

    lax.fori_loop(0, tile, start, 0)
    lax.fori_loop(0, tile, wait, 0)
    route = route_ref[...]
    w0 = route[:, ROUTE_WEIGHT:ROUTE_WEIGHT + 1]
    w1 = route[:, ROUTE_WEIGHT + 1:ROUTE_WEIGHT + 2]
    for s in range(lay.n_seg):
        cols = slice(s * LANES, (s + 1) * LANES)
        y = w0 * buf0[lay.seg(s, tile), :] + w1 * buf1[lay.seg(s, tile), :]
        out_ref[:, cols] = x_ref[:, cols] + gate_ref[0][:, cols] * y


def _moe_combine(y_seg, pos, x, route, gate, seq, lay):
    t, d = x.shape
    tile = min(MOE_DMA_TILE, seq)
    rpb = seq // tile
    row = pl.BlockSpec((tile, d), lambda i, p: (i, 0))
    buf = pltpu.VMEM((tile * lay.pitch, LANES), F32)
    return pl.pallas_call(
        functools.partial(_combine_kernel, n_tok=t, tile=tile, lay=lay),
        grid_spec=pltpu.PrefetchScalarGridSpec(
            num_scalar_prefetch=1, grid=(t // tile,),
            in_specs=[pl.BlockSpec(memory_space=pl.ANY), row,
                      pl.BlockSpec((tile, LANES), lambda i, p: (i, 0)),
                      pl.BlockSpec((1, 1, d), lambda i, p: (i // rpb, 0, 0))],
            out_specs=row,
            scratch_shapes=[buf, buf, pltpu.SemaphoreType.DMA(())]),
        out_shape=jax.ShapeDtypeStruct((t, d), F32),
        compiler_params=_cparams("arbitrary"), name="moe_combine")(pos, y_seg, x, route, gate)


def _ada_mod(c, ada_w, ada_b, ada_table):
    bsz, d = c.shape
    b = 16
    c = jnp.pad(c, ((0, b - bsz), (0, 0)))
    depth = ada_table.shape[0]
    n = ada_w.shape[1]
    tn = min(512, n)

    def epilogue(accs, ex, outs, ids):
        bias_ref, tab_ref = ex
        outs[0][...] = (accs[0] + bias_ref[...])[None] + tab_ref[...]

    out = _mm(
        c, pl.BlockSpec((b, d), lambda i, j: (0, 0)),
        [(ada_w, pl.BlockSpec((d, tn), lambda i, j: (0, j)))],
        [(ada_b.reshape(1, n), pl.BlockSpec((1, tn), lambda i, j: (0, j))),
         (ada_table.reshape(depth, 1, n), pl.BlockSpec((depth, 1, tn), lambda i, j: (0, 0, j)))],
        [(jax.ShapeDtypeStruct((depth, b, n), F32), pl.BlockSpec((depth, b, tn), lambda i, j: (0, 0, j)))],
        epilogue, grid=(1, n // tn), name="ada_mod", cast_w=True, a_fn=lambda v: _silu(v).astype(BF16))
    return out[0][:, :bsz].reshape(depth, bsz, N_ADA, d)


def _norm_kernel(*refs, modulate, n_experts):
    if not modulate:
        x_ref, g_ref, out_ref = refs
        out_ref[...] = _rms(x_ref[...], g_ref[...])
        return
    x_ref, g_ref, sh_ref, sc_ref = refs[:4]
    h = _rms(x_ref[...], g_ref[...]) * (1.0 + sc_ref[0]) + sh_ref[0]
    if not n_experts:
        refs[4][...] = h.astype(BF16)
        return
    r_ref, hseg_ref, route_ref, count_ref, carry_ref = refs[4:]

    @pl.when(pl.program_id(0) == 0)
    def _():
        carry_ref[...] = jnp.zeros_like(carry_ref)

    _SegLayout(h.shape[1]).store(hseg_ref, h)
    logits = jnp.dot(h, r_ref[...], preferred_element_type=F32, precision=lax.Precision.HIGHEST)
    lane = lax.broadcasted_iota(jnp.int32, logits.shape, 1)
    lg = jnp.where(lane < n_experts, logits, -jnp.inf)
    m1 = jnp.max(lg, axis=1, keepdims=True)
    i1 = jnp.min(jnp.where(lg == m1, lane, LANES), axis=1, keepdims=True)
    lg2 = jnp.where(lane == i1, -jnp.inf, lg)
    m2 = jnp.max(lg2, axis=1, keepdims=True)
    i2 = jnp.min(jnp.where(lg2 == m2, lane, LANES), axis=1, keepdims=True)
    e2 = jnp.exp(m2 - m1)
    w1 = 1.0 / (1.0 + e2)
    sel = jnp.where((lane == i1) | (lane == i2), 1.0, 0.0)
    tm = sel.shape[0]
    lower = lax.broadcasted_iota(jnp.int32, (tm, tm), 0) >= lax.broadcasted_iota(jnp.int32, (tm, tm), 1)
    cum = jnp.dot(jnp.where(lower, 1.0, 0.0).astype(BF16), sel.astype(BF16), preferred_element_type=F32)
    rank = cum - sel + carry_ref[...]
    r1 = jnp.sum(jnp.where(lane == i1, rank, 0.0), axis=1, keepdims=True)
    r2 = jnp.sum(jnp.where(lane == i2, rank, 0.0), axis=1, keepdims=True)
    carry_ref[...] += jnp.sum(sel, axis=0, keepdims=True)
    count_ref[...] = carry_ref[...]
    cols = (i1.astype(F32), i2.astype(F32), r1, r2, w1, e2 * w1)
    table = jnp.zeros_like(logits)
    for k, col in enumerate(cols):
        table = jnp.where(lane == k, col, table)
    route_ref[...] = table


ROUTE_EXPERT, ROUTE_RANK, ROUTE_WEIGHT = 0, 2, 4


def _norm_mod(x, g, shift, scale, seq, router=None, *, tm=256):
    t, d = x.shape
    tm = min(tm, seq)
    rpb = seq // tm
    row = pl.BlockSpec((tm, d), lambda i: (i, 0))
    vec = pl.BlockSpec((1, d), lambda i: (0, 0))
    per_batch = pl.BlockSpec((1, 1, d), lambda i: (i // rpb, 0, 0))
    ins = [x, g.reshape(1, d), shift, scale]
    in_specs = [row, vec, per_batch, per_batch]
    out_shape = [jax.ShapeDtypeStruct((t, d), BF16)]
    out_specs = [row]
    n_experts = 0
    scratch = []
    if router is not None:
        n_experts = router.shape[1]
        pitch = _SegLayout(d).pitch
        out_shape = [jax.ShapeDtypeStruct((t * pitch, LANES), F32)]
        out_specs = [pl.BlockSpec((tm * pitch, LANES), lambda i: (i, 0))]
        ins.append(jnp.pad(router, ((0, 0), (0, LANES - n_experts))))
        in_specs.append(pl.BlockSpec((d, LANES), lambda i: (0, 0)))
        out_shape += [jax.ShapeDtypeStruct((t, LANES), F32), jax.ShapeDtypeStruct((1, LANES), F32)]
        out_specs += [pl.BlockSpec((tm, LANES), lambda i: (i, 0)), pl.BlockSpec((1, LANES), lambda i: (0, 0))]
        scratch = [pltpu.VMEM((1, LANES), F32)]
    res = pl.pallas_call(
        functools.partial(_norm_kernel, modulate=True, n_experts=n_experts),
        grid=(t // tm,), in_specs=in_specs, out_specs=out_specs, out_shape=out_shape, scratch_shapes=scratch,
        compiler_params=_cparams("arbitrary" if router is not None else "parallel"),
        name="norm_router" if router is not None else "norm_mod")(*ins)
    return res if router is not None else res[0]


def _final_norm(x, g, *, tm=256):
    t, d = x.shape
    tm = min(tm, t)
    row = pl.BlockSpec((tm, d), lambda i: (i, 0))
    return pl.pallas_call(
        functools.partial(_norm_kernel, modulate=False, n_experts=0),
        grid=(t // tm,), in_specs=[row, pl.BlockSpec((1, d), lambda i: (0, 0))], out_specs=row,
        out_shape=jax.ShapeDtypeStruct((t, d), F32), compiler_params=_cparams("parallel"),
        name="final_norm")(x, g.reshape(1, d))


def _conv_kernel(*refs, width, ts, tc, has_bias, has_mul):
    cur_ref, halo_ref, w_ref = refs[:3]
    rest = list(refs[3:])
    b_ref = rest.pop(0) if has_bias else None
    mul_ref = rest.pop(0) if has_mul else None
    out_ref, buf = rest
    first = pl.program_id(1) == 0
    buf[0:CONV_HALO, :] = jnp.where(first, 0.0, halo_ref[0])
    buf[CONV_HALO:CONV_HALO + ts, :] = cur_ref[0]
    rows = 32
    for r in range(0, ts, rows):
        acc = None
        for k in range(width):
            term = w_ref[k:k + 1, :] * buf[pl.ds(CONV_HALO + r - (width - 1) + k, rows), :]
            acc = term if acc is None else acc + term
        if has_bias:
            acc = acc + b_ref[...]
        if has_mul:
            acc = acc * mul_ref[0, r:r + rows, :].astype(F32)
        out_ref[0, r:r + rows, :] = acc.astype(out_ref.dtype)


def _causal_conv(x, conv_w, bias=None, mul=None, out_dtype=F32, *, ts=256, tc=512):
    b, s, ch = x.shape
    width = conv_w.shape[0]
    assert width - 1 <= CONV_HALO
    ts, tc = min(ts, s), min(tc, ch)
    hpb = ts // CONV_HALO
    blk = pl.BlockSpec((1, ts, tc), lambda bi, si, ci: (bi, si, ci))
    ins = [x, x, conv_w]
    in_specs = [blk,
                pl.BlockSpec((1, CONV_HALO, tc), lambda bi, si, ci: (bi, jnp.maximum(si * hpb - 1, 0), ci)),
                pl.BlockSpec((width, tc), lambda bi, si, ci: (0, ci))]
    if bias is not None:
        ins.append(bias.reshape(1, ch))
        in_specs.append(pl.BlockSpec((1, tc), lambda bi, si, ci: (0, ci)))
    if mul is not None:
        ins.append(mul)
        in_specs.append(blk)
    return pl.pallas_call(
        functools.partial(_conv_kernel, width=width, ts=ts, tc=tc, has_bias=bias is not None,
                          has_mul=mul is not None),
        grid=(b, s // ts, ch // tc), in_specs=in_specs, out_specs=blk,
        out_shape=jax.ShapeDtypeStruct((b, s, ch), out_dtype),
        scratch_shapes=[pltpu.VMEM((CONV_HALO + ts, tc), F32)],
        compiler_params=_cparams("parallel", "parallel", "parallel"), name=f"causal_conv{width}")(*ins)


def _ln_silu_kernel(x_ref, g_ref, b_ref, out_ref):
    x = x_ref[...]
    mu = jnp.mean(x, axis=-1, keepdims=True)
    xc = x - mu
    var = jnp.mean(xc * xc, axis=-1, keepdims=True)
    y = xc * lax.rsqrt(var + NORM_EPS) * g_ref[...] + b_ref[...]
    out_ref[...] = _silu(y).astype(BF16)


def _ln_silu(x, g, b, *, tm=256):
    t, d = x.shape
    tm = min(tm, t)
    row = pl.BlockSpec((tm, d), lambda i: (i, 0))
    vec = pl.BlockSpec((1, d), lambda i: (0, 0))
    return pl.pallas_call(
        _ln_silu_kernel, grid=(t // tm,), in_specs=[row, vec, vec], out_specs=row,
        out_shape=jax.ShapeDtypeStruct((t, d), BF16), compiler_params=_cparams("parallel"), name="ln_silu")(
            x, g.reshape(1, d), b.reshape(1, d))


def _rglru_kernel(xc_ref, gate_ref, wa_ref, wx_ref, ba_ref, bx_ref, lam_ref, out_ref,
                  a_scr, u_scr, h_scr, *, ts, n_heads, blk, cw):
    @pl.when(pl.program_id(1) == 0)
    def _():
        h_scr[...] = jnp.zeros_like(h_scr)

    neg_lam = -lam_ref[...]
    softplus = jnp.maximum(neg_lam, 0.0) + jnp.log1p(jnp.exp(-jnp.abs(neg_lam)))
    for hd in range(n_heads):
        sl = slice(hd * blk, (hd + 1) * blk)
        xh = xc_ref[:, sl]
        xb = xh.astype(BF16)
        r = _sigmoid(jnp.dot(xb, wa_ref[hd], preferred_element_type=F32) + ba_ref[:, sl])
        i_g = _sigmoid(jnp.dot(xb, wx_ref[hd], preferred_element_type=F32) + bx_ref[:, sl])
        log_a = -LRU_C * r * softplus[:, sl]
        a_scr[:, sl] = jnp.exp(log_a)
        u_scr[:, sl] = jnp.sqrt(1.0 - jnp.exp(2.0 * log_a)) * (i_g * xh)

    row = lax.broadcasted_iota(jnp.int32, (SUBLANES, cw), 0)
    width = n_heads * blk
    for c0 in range(0, width, cw):
        def body(g, h_prev, c0=c0):
            r0 = pl.multiple_of(g * SUBLANES, SUBLANES)
            a8 = a_scr[pl.ds(r0, SUBLANES), c0:c0 + cw]
            b8 = u_scr[pl.ds(r0, SUBLANES), c0:c0 + cw]
            for d in (1, 2, 4):
                keep = row >= d
                b8 = jnp.where(keep, a8 * pltpu.roll(b8, d, axis=0) + b8, b8)
                a8 = jnp.where(keep, a8 * pltpu.roll(a8, d, axis=0), a8)
            h8 = b8 + a8 * h_prev
            u_scr[pl.ds(r0, SUBLANES), c0:c0 + cw] = h8
            return h8[SUBLANES - 1:SUBLANES, :]

        h_scr[:, c0:c0 + cw] = lax.fori_loop(0, ts // SUBLANES, body, h_scr[:, c0:c0 + cw])
    out_ref[...] = (u_scr[...] * gate_ref[...].astype(F32)).astype(BF16)


def _rglru(xc, gate, w_a, w_x, b_a, b_x, lam, seq, *, ts=256):
    t, width = xc.shape
    n_heads, blk, _ = w_a.shape
    ts = min(ts, seq)
    nts = seq // ts
    row = pl.BlockSpec((ts, width), lambda b, j: (b * nts + j, 0))
    wsp = pl.BlockSpec((n_heads, blk, blk), lambda b, j: (0, 0, 0))
    vec = pl.BlockSpec((1, width), lambda b, j: (0, 0))
    return pl.pallas_call(
        functools.partial(_rglru_kernel, ts=ts, n_heads=n_heads, blk=blk, cw=min(1024, width)),
        grid=(t // seq, nts), in_specs=[row, row, wsp, wsp, vec, vec, vec], out_specs=row,
        out_shape=jax.ShapeDtypeStruct((t, width), BF16),
        scratch_shapes=[pltpu.VMEM((ts, width), F32), pltpu.VMEM((ts, width), F32), pltpu.VMEM((1, width), F32)],
        compiler_params=_cparams("parallel", "arbitrary"), name="rglru")(
            xc, gate, w_a, w_x, b_a.reshape(1, width), b_x.reshape(1, width), lam.reshape(1, width))


def _bias_tile_kernel(rb_ref, out_ref):
    hd = pl.program_id(0)
    shape = (2 * Q_BLOCK, Q_BLOCK)
    dist = Q_BLOCK + lax.broadcasted_iota(jnp.int32, shape, 1) - lax.broadcasted_iota(jnp.int32, shape, 0)
    dist = jnp.maximum(dist, 0)
    max_exact = REL_BUCKETS // 2
    large = max_exact + (jnp.log(jnp.maximum(dist, 1).astype(F32) / max_exact)
                         / math.log(REL_MAX_DIST / max_exact) * (REL_BUCKETS - max_exact)).astype(jnp.int32)
    bucket = jnp.where(dist < max_exact, dist, jnp.minimum(large, REL_BUCKETS - 1))
    tile = jnp.zeros(shape, F32)
    for bkt in range(REL_BUCKETS):
        tile = jnp.where(bucket == bkt, rb_ref[bkt, hd], tile)
    out_ref[0] = tile - rb_ref[REL_BUCKETS - 1, hd]


def _bias_tile(rel_bias):
    n_heads = rel_bias.shape[1]
    return pl.pallas_call(
        _bias_tile_kernel, grid=(n_heads,),
        in_specs=[pl.BlockSpec(memory_space=pltpu.SMEM)],
        out_specs=pl.BlockSpec((1, 2 * Q_BLOCK, Q_BLOCK), lambda h: (h, 0, 0)),
        out_shape=jax.ShapeDtypeStruct((n_heads, 2 * Q_BLOCK, Q_BLOCK), F32),
        compiler_params=_cparams("parallel"), name="dsa_bias_tile")(rel_bias)


def _dsa_kernel(qidx_ref, wcol_ref, kidx_ref, ckv_ref, ckvt_ref, qlat_ref, bias_ref, wuv_ref, out_ref,
                score_ref, key_ref, madd_ref, z_ref, acc_ref, *, n_slab, k_sel, idx_heads, kv_dim, head_dim):
    qb = pl.program_id(1)
    cols = HEAD_GROUP * Q_BLOCK
    sub = KEY_CHUNK // SUBLANES
    nt = (((1,), (1,)), ((), ()))
    n_chunks = (qb + 2) // 2

    @pl.when(pl.program_id(2) == 0)
    def _select():
        def chunk_body(c, carry):
            k0 = pl.multiple_of(c * KEY_CHUNK, KEY_CHUNK)
            keys = kidx_ref[0, pl.ds(k0, KEY_CHUNK), :]

            def head_body(hg, acc):
                h0 = pl.multiple_of(hg * HEAD_GROUP, HEAD_GROUP)
                q = qidx_ref[0, pl.ds(h0, HEAD_GROUP)].reshape(cols, LANES)
                dots = lax.dot_general(q, keys, nt, preferred_element_type=F32)
                dots = jnp.maximum(dots, 0.0).reshape(HEAD_GROUP, Q_BLOCK, KEY_CHUNK)
                return acc + jnp.sum(dots * wcol_ref[0, pl.ds(h0, HEAD_GROUP)], axis=0)

            acc = lax.fori_loop(0, idx_heads // HEAD_GROUP, head_body, jnp.zeros((Q_BLOCK, KEY_CHUNK), F32),
                                unroll=2)
            score_ref[pl.ds(k0, KEY_CHUNK), :] = acc.T
            return carry

        lax.fori_loop(0, n_chunks, chunk_body, 0)

        def zero_body(c, carry):
            score_ref[pl.ds(pl.multiple_of(c * KEY_CHUNK, KEY_CHUNK), KEY_CHUNK), :] = jnp.zeros(
                (KEY_CHUNK, Q_BLOCK), F32)
            return carry

        lax.fori_loop(n_chunks, n_slab // 2, zero_body, 0)

        shape = (n_slab * LANES, Q_BLOCK)
        bits = pltpu.bitcast(score_ref[...], jnp.int32)
        key = jnp.where(bits >= 0, bits, bits ^ jnp.int32(0x7FFFFFFF))
        causal = lax.broadcasted_iota(jnp.int32, shape, 0) <= qb * Q_BLOCK + lax.broadcasted_iota(jnp.int32, shape, 1)
        key = jnp.where(causal, key, INT_MIN)
        key_ref[...] = key

        def bit_body(it, thr):
            cand = thr + lax.shift_left(jnp.int32(1), 31 - it)
            cand_b = jnp.broadcast_to(cand, (SUBLANES, Q_BLOCK))[None]

            def count_body(c, cnt):
                kc = key_ref[pl.ds(pl.multiple_of(c * 2 * KEY_CHUNK, 2 * KEY_CHUNK), 2 * KEY_CHUNK), :]
                return cnt + jnp.sum(jnp.where(kc.reshape(2 * sub, SUBLANES, Q_BLOCK) >= cand_b, 1.0, 0.0), axis=0)

            cnt = lax.fori_loop(0, (n_chunks + 1) // 2, count_body, jnp.zeros((SUBLANES, Q_BLOCK), F32))
            return jnp.where(jnp.sum(cnt, axis=0, keepdims=True) >= k_sel, cand, thr)

        thr = lax.fori_loop(0, 32, bit_body, jnp.full((1, Q_BLOCK), INT_MIN, jnp.int32))
        madd_ref[0:Q_BLOCK, :] = jnp.full((Q_BLOCK, Q_BLOCK), MASK_NEG, F32)
        madd_ref[Q_BLOCK:, :] = jnp.where(causal & (key >= thr), 0.0, MASK_NEG)

    q = qlat_ref[0].reshape(cols, kv_dim)
    n_far = qb // 2
    near_row = qb * Q_BLOCK

    def logits_of(kv):
        return lax.dot_general(kv, q, nt, preferred_element_type=F32)

    def per_head(m):
        return jnp.concatenate([m] * HEAD_GROUP, axis=1)

    def col_max(z):
        return jnp.max(z.reshape(sub, SUBLANES, cols), axis=0)

    def far_logits(c, m8):
        r0 = pl.multiple_of(Q_BLOCK + c * KEY_CHUNK, Q_BLOCK)
        kv = ckv_ref[0, pl.ds(r0, KEY_CHUNK), :]
        row = r0 + lax.broadcasted_iota(jnp.int32, (KEY_CHUNK, Q_BLOCK), 0)
        madd = jnp.where(row < near_row, madd_ref[pl.ds(r0, KEY_CHUNK), :], MASK_NEG)
        z = logits_of(kv) + per_head(madd)
        z_ref[c] = z
        return jnp.maximum(m8, col_max(z))

    m8 = lax.fori_loop(0, n_far, far_logits, jnp.full((SUBLANES, cols), MASK_NEG, F32))
    r0 = pl.multiple_of(near_row, Q_BLOCK)
    bias = jnp.concatenate([bias_ref[hh] for hh in range(HEAD_GROUP)], axis=1)
    z = (logits_of(ckv_ref[0, pl.ds(r0, KEY_CHUNK), :]) + per_head(madd_ref[pl.ds(r0, KEY_CHUNK), :])) + bias
    z_ref[n_far] = z
    m8 = jnp.maximum(m8, col_max(z))
    m8 = jnp.broadcast_to(jnp.max(m8, axis=0, keepdims=True), (SUBLANES, cols))[None]
    acc_ref[...] = jnp.zeros(acc_ref.shape, F32)

    def weigh(c, l8):
        s0 = jnp.where(c < n_far, 2 * c + 1, qb)
        kvt = jnp.concatenate([ckvt_ref[0, s0], ckvt_ref[0, s0 + 1]], axis=1)
        p = jnp.exp(z_ref[c].reshape(sub, SUBLANES, cols) - m8)
        acc_ref[...] += jnp.dot(kvt, p.reshape(KEY_CHUNK, cols).astype(BF16), preferred_element_type=F32)
        return l8 + jnp.sum(p, axis=0)

    l8 = lax.fori_loop(0, n_far + 1, weigh, jnp.zeros((SUBLANES, cols), F32))
    inv = 1.0 / jnp.broadcast_to(jnp.sum(l8, axis=0, keepdims=True), (SUBLANES, cols))
    o_lat = (acc_ref[...].reshape(kv_dim // SUBLANES, SUBLANES, cols) * inv[None]).reshape(kv_dim, cols)
    o_lat = o_lat.astype(BF16)
    for hh in range(HEAD_GROUP):
        o_h = jnp.dot(wuv_ref[hh], o_lat[:, hh * Q_BLOCK:(hh + 1) * Q_BLOCK], preferred_element_type=F32)
        out_ref[:, hh * head_dim:(hh + 1) * head_dim] = o_h.T.astype(BF16)


def _dsa_core(qidx, wcol, kidx, ckv_pad, ckvt_pad, qlat, bias, w_uvt, *, bsz, seq, k_sel):
    n_qb = seq // Q_BLOCK
    idx_heads = qidx.shape[1]
    n_heads, kv_dim = qlat.shape[1], qlat.shape[3]
    head_dim = w_uvt.shape[1]
    n_groups = n_heads // HEAD_GROUP
    cols = HEAD_GROUP * Q_BLOCK
    blk_q = lambda b, i, g: (b * n_qb + i, 0, 0, 0)
    per_batch = lambda b, i, g: (b, 0, 0)
    return pl.pallas_call(
        functools.partial(_dsa_kernel, n_slab=n_qb, k_sel=k_sel, idx_heads=idx_heads, kv_dim=kv_dim,
                          head_dim=head_dim),
        grid=(bsz, n_qb, n_groups),
        in_specs=[
            pl.BlockSpec((1, idx_heads, Q_BLOCK, LANES), blk_q),
            pl.BlockSpec((1, idx_heads, Q_BLOCK, 1), blk_q),
            pl.BlockSpec((1, seq, LANES), per_batch),
            pl.BlockSpec((1, seq + Q_BLOCK, kv_dim), per_batch),
            pl.BlockSpec((1, n_qb + 1, kv_dim, Q_BLOCK), lambda b, i, g: (b, 0, 0, 0)),
            pl.BlockSpec((1, HEAD_GROUP, Q_BLOCK, kv_dim), lambda b, i, g: (b * n_qb + i, g, 0, 0)),
            pl.BlockSpec((HEAD_GROUP, KEY_CHUNK, Q_BLOCK), lambda b, i, g: (g, 0, 0)),
            pl.BlockSpec((HEAD_GROUP, head_dim, kv_dim), lambda b, i, g: (g, 0, 0)),
        ],
        out_specs=pl.BlockSpec((Q_BLOCK, HEAD_GROUP * head_dim), lambda b, i, g: (b * n_qb + i, g)),
        out_shape=jax.ShapeDtypeStruct((bsz * seq, n_heads * head_dim), BF16),
        scratch_shapes=[
            pltpu.VMEM((seq, Q_BLOCK), F32),
            pltpu.VMEM((seq, Q_BLOCK), jnp.int32),
            pltpu.VMEM((seq + Q_BLOCK, Q_BLOCK), F32),
            pltpu.VMEM((n_qb // 2 + 1, KEY_CHUNK, cols), F32),
            pltpu.VMEM((kv_dim, cols), F32),
        ],
        compiler_params=_cparams("parallel", "parallel", "arbitrary"), name="dsa_core")(
            qidx, wcol, kidx, ckv_pad, ckvt_pad, qlat, bias, w_uvt)


def _dsa_attention(h, x, gate, rel_bias, w_in, g_cq, g_ckv, w_uq, w_uk, w_uv, w_qidx, w_out, *, bsz, seq):
    t, d = h.shape
    q_lora, kv_lora = g_cq.shape[0], g_ckv.shape[0]
    n_heads, head_dim = w_uk.shape[1], w_uk.shape[2]
    idx_dim = LANES
    idx_heads = w_qidx.shape[1] // idx_dim
    k_sel = min(TOPK_MAX, seq // 4)
    n_qb = seq // Q_BLOCK
    tm = min(512, seq)
    qpt = tm // Q_BLOCK

    w_in = w_in.astype(BF16)
    splits = (0, q_lora, q_lora + kv_lora, q_lora + kv_lora + idx_dim, w_in.shape[1])
    w_parts = [w_in[:, splits[i]:splits[i + 1]] for i in range(4)]
    head_scale = idx_heads ** -0.5 * idx_dim ** -0.5

    def in_epilogue(accs, ex, outs, ids):
        outs[0][...] = _rms(accs[0], ex[0][...]).astype(BF16)
        outs[1][...] = _rms(accs[1], ex[1][...]).astype(BF16)
        outs[2][...] = accs[2].astype(BF16)
        outs[3][...] = accs[3] * head_scale

    full = lambda n: pl.BlockSpec((d, n), lambda i, j: (0, 0))
    rowsp = lambda n: pl.BlockSpec((tm, n), lambda i, j: (i, 0))
    vecsp = lambda n: pl.BlockSpec((1, n), lambda i, j: (0, 0))
    widths = [w.shape[1] for w in w_parts]
    c_q, c_kv, k_idx, w_head = _mm(
        h, pl.BlockSpec((tm, d), lambda i, j: (i, 0)),
        [(w, full(n)) for w, n in zip(w_parts, widths)],
        [(g_cq.reshape(1, q_lora), vecsp(q_lora)), (g_ckv.reshape(1, kv_lora), vecsp(kv_lora))],
        [(jax.ShapeDtypeStruct((t, n), dt), rowsp(n)) for n, dt in zip(widths, (BF16, BF16, BF16, F32))],
        in_epilogue, grid=(t // tm, 1), name="dsa_in_proj")

    hpt = 4
    tn = hpt * idx_dim

    def qidx_epilogue(accs, ex, outs, ids):
        for hh in range(hpt):
            outs[0][:, hh] = accs[0][:, hh * idx_dim:(hh + 1) * idx_dim].reshape(qpt, Q_BLOCK, idx_dim).astype(BF16)

    qidx = _mm(
        c_q, pl.BlockSpec((tm, q_lora), lambda i, j: (i, 0)),
        [(w_qidx, pl.BlockSpec((q_lora, tn), lambda i, j: (0, j)))], [],
        [(jax.ShapeDtypeStruct((t // Q_BLOCK, idx_heads, Q_BLOCK, idx_dim), BF16),
          pl.BlockSpec((qpt, hpt, Q_BLOCK, idx_dim), lambda i, j: (i, j, 0, 0)))],
        qidx_epilogue, grid=(t // tm, idx_heads // hpt), name="dsa_qidx", cast_w=True)[0]

    w_ukt = jnp.transpose(w_uk, (1, 2, 0)).astype(BF16)
    logit_scale = head_dim ** -0.5

    def qlat_kernel(cq_ref, wuq_ref, wuk_ref, out_ref):
        qg = jnp.dot(cq_ref[...], wuq_ref[...], preferred_element_type=F32).astype(BF16)
        for hh in range(HEAD_GROUP):
            ql = jnp.dot(qg[:, hh * head_dim:(hh + 1) * head_dim], wuk_ref[hh], preferred_element_type=F32)
            out_ref[:, hh] = (ql * logit_scale).reshape(qpt, Q_BLOCK, kv_lora).astype(BF16)

    qlat = pl.pallas_call(
        qlat_kernel, grid=(t // tm, n_heads // HEAD_GROUP),
        in_specs=[pl.BlockSpec((tm, q_lora), lambda i, g: (i, 0)),
                  pl.BlockSpec((q_lora, HEAD_GROUP * head_dim), lambda i, g: (0, g)),
                  pl.BlockSpec((HEAD_GROUP, head_dim, kv_lora), lambda i, g: (g, 0, 0))],
        out_specs=pl.BlockSpec((qpt, HEAD_GROUP, Q_BLOCK, kv_lora), lambda i, g: (i, g, 0, 0)),
        out_shape=jax.ShapeDtypeStruct((t // Q_BLOCK, n_heads, Q_BLOCK, kv_lora), BF16),
        compiler_params=_cparams("parallel", "parallel"), name="dsa_qlat")(c_q, w_uq.astype(BF16), w_ukt)

    wcol = w_head.reshape(bsz * n_qb, Q_BLOCK, idx_heads).transpose(0, 2, 1)[..., None]
    ckv_pad = jnp.pad(c_kv.reshape(bsz, seq, kv_lora), ((0, 0), (Q_BLOCK, 0), (0, 0)))
    ckvt_pad = ckv_pad.reshape(bsz, n_qb + 1, Q_BLOCK, kv_lora).transpose(0, 1, 3, 2)
    w_uvt = jnp.transpose(w_uv, (1, 2, 0)).astype(BF16)
    o = _dsa_core(qidx, wcol, k_idx.reshape(bsz, seq, idx_dim), ckv_pad, ckvt_pad, qlat,
                  _bias_tile(rel_bias.astype(F32)), w_uvt, bsz=bsz, seq=seq, k_sel=k_sel)
    return _matmul_resid(o, w_out, x, gate, seq, name="dsa_out_proj")


def _split_proj(h, w, n_parts, epilogue, out_dtypes, *, name, tm=1024, tn=256):
    t, d = h.shape
    n = w.shape[1] // n_parts
    tm, tn = min(tm, t), min(tn, n)
    nb = n // tn
    out_sp = pl.BlockSpec((tm, tn), lambda i, j: (i, j))
    return _mm(
        h, pl.BlockSpec((tm, d), lambda i, j: (i, 0)),
        [(w, pl.BlockSpec((d, tn), lambda i, j, p=p: (0, j + p * nb))) for p in range(n_parts)], [],
        [(jax.ShapeDtypeStruct((t, n), dt), out_sp) for dt in out_dtypes],
        epilogue, grid=(t // tm, nb), name=name, cast_w=True)


def _short_conv_mixer(h, x, gate, w_in, conv_w, w_out, *, bsz, seq):
    def epilogue(accs, ex, outs, ids):
        outs[0][...] = accs[0].astype(BF16)
        outs[1][...] = accs[1] * accs[2]

    gate_b, cx = _split_proj(h, w_in, 3, epilogue, (BF16, F32), name="sconv_in_proj", tm=512)
    ch = cx.shape[1]
    y = _causal_conv(cx.reshape(bsz, seq, ch), conv_w, mul=gate_b.reshape(bsz, seq, ch), out_dtype=BF16)
    return _matmul_resid(y.reshape(bsz * seq, ch), w_out, x, gate, seq, name="sconv_out_proj")


def _conformer_mixer(h, x, gate, w_in, conv_w, conv_b, ln_g, ln_b, w_out, *, bsz, seq):
    def epilogue(accs, ex, outs, ids):
        outs[0][...] = accs[0] * _sigmoid(accs[1])

    u = _split_proj(h, w_in, 2, epilogue, (F32,), name="conf_in_proj")[0]
    ch = u.shape[1]
    u = _causal_conv(u.reshape(bsz, seq, ch), conv_w, bias=conv_b)
    u = _ln_silu(u.reshape(bsz * seq, ch), ln_g, ln_b)
    return _matmul_resid(u, w_out, x, gate, seq, name="conf_out_proj")


def _rglru_mixer(h, x, gate, w_in, conv_w, conv_b, w_a, b_a, w_x, b_x, lam, w_out, *, bsz, seq):
    def epilogue(accs, ex, outs, ids):
        outs[0][...] = _gelu_tanh(accs[0]).astype(BF16)
        outs[1][...] = accs[1]

    gate_br, x_br = _split_proj(h, w_in, 2, epilogue, (BF16, F32), name="lru_in_proj")
    width = x_br.shape[1]
    xc = _causal_conv(x_br.reshape(bsz, seq, width), conv_w, bias=conv_b)
    y = _rglru(xc.reshape(bsz * seq, width), gate_br, w_a.astype(BF16), w_x.astype(BF16), b_a, b_x, lam, seq)
    return _matmul_resid(y, w_out, x, gate, seq, name="lru_out_proj")


def _dense_ffn(h, x, gate, w13, w2, *, seq):
    act = _swiglu_act(h, w13)
    return _matmul_resid(act, w2.astype(BF16), x, gate, seq, name="ffn_down")


def _moe_ffn(h_seg, route, counts, x, gate, w13, w2, *, seq):
    t = x.shape[0]
    n_e = w13.shape[0]
    tile = MOE_ROW_TILE
    n_rows = MOE_TOPK * t + n_e * tile
    counts = counts[0, :n_e].astype(jnp.int32)
    padded = (counts + tile - 1) // tile * tile
    ends = jnp.cumsum(padded)
    starts = ends - padded
    experts = route[:, ROUTE_EXPERT:ROUTE_EXPERT + MOE_TOPK].astype(jnp.int32)
    ranks = route[:, ROUTE_RANK:ROUTE_RANK + MOE_TOPK].astype(jnp.int32)
    pos = (starts[experts] + ranks).T.reshape(MOE_TOPK * t)
    tile_start = jnp.arange(n_rows // tile, dtype=jnp.int32) * tile
    tile_expert = jnp.minimum(jnp.sum(tile_start[:, None] >= ends[None, :], axis=1), n_e - 1).astype(jnp.int32)
    n_used = (ends[-1:] // tile).astype(jnp.int32)

    lay = _SegLayout(x.shape[1])
    hs = _moe_dispatch(h_seg, pos, n_rows, lay)
    act = _grouped_swiglu(hs, w13.astype(BF16), tile_expert, n_used, lay)
    y = _grouped_down(act, w2.astype(BF16), tile_expert, n_used, lay)
    return _moe_combine(y, pos, x, route, gate, seq, lay)


def kernel(x, c, ada_w, ada_b, ada_table, norm_mix, norm_ffn, norm_final, rel_bias, att_w_in, att_g_cq, att_g_ckv, att_w_uq, att_w_uk, att_w_uv, att_w_qidx, att_w_out, sconv_w_in, sconv_conv_w, sconv_w_out, conf_w_in, conf_conv_w, conf_conv_b, conf_ln_g, conf_ln_b, conf_w_out, lru_w_in, lru_conv_w, lru_conv_b, lru_w_a, lru_b_a, lru_w_x, lru_b_x, lru_lambda, lru_w_out, ffn_w13, ffn_w2, moe_router, moe_w13, moe_w2):
    bsz, seq, d = x.shape
    depth = ada_table.shape[0]
    xf = x.reshape(bsz * seq, d)
    mods = _ada_mod(c, ada_w, ada_b, ada_table)
    for i in range(depth):
        shift_m, scale_m, gate_m, shift_f, scale_f, gate_f = (mods[i][:, k:k + 1, :] for k in range(N_ADA))
        h = _norm_mod(xf, norm_mix[i], shift_m, scale_m, seq)
        kind, j = i % 4, i // 4
        if kind == 0:
            xf = _dsa_attention(h, xf, gate_m, rel_bias, att_w_in[j], att_g_cq[j], att_g_ckv[j], att_w_uq[j],
                                att_w_uk[j], att_w_uv[j], att_w_qidx[j], att_w_out[j], bsz=bsz, seq=seq)
        elif kind == 1:
            xf = _short_conv_mixer(h, xf, gate_m, sconv_w_in[j], sconv_conv_w[j], sconv_w_out[j], bsz=bsz, seq=seq)
        elif kind == 2:
            xf = _conformer_mixer(h, xf, gate_m, conf_w_in[j], conf_conv_w[j], conf_conv_b[j], conf_ln_g[j],
                                  conf_ln_b[j], conf_w_out[j], bsz=bsz, seq=seq)
        else:
            xf = _rglru_mixer(h, xf, gate_m, lru_w_in[j], lru_conv_w[j], lru_conv_b[j], lru_w_a[j], lru_b_a[j],
                              lru_w_x[j], lru_b_x[j], lru_lambda[j], lru_w_out[j], bsz=bsz, seq=seq)
        if i % 2 == 0:
            h = _norm_mod(xf, norm_ffn[i], shift_f, scale_f, seq)
            xf = _dense_ffn(h, xf, gate_f, ffn_w13[i // 2], ffn_w2[i // 2], seq=seq)
        else:
            h, route, counts = _norm_mod(xf, norm_ffn[i], shift_f, scale_f, seq, router=moe_router[i // 2])
            xf = _moe_ffn(h, route, counts, xf, gate_f, moe_w13[i // 2], moe_w2[i // 2], seq=seq)
    return _final_norm(xf, norm_final).reshape(bsz, seq, d)
```

```python
import functools
import math

import jax
import jax.numpy as jnp
from jax import lax
from jax.experimental import pallas as pl
from jax.experimental.pallas import tpu as pltpu

F32 = jnp.float32
BF16 = jnp.bfloat16

NORM_EPS = 1e-6
N_ADA = 6
MOE_TOPK = 2
LRU_C = 8.0
REL_BUCKETS = 32
REL_MAX_DIST = 128
TOPK_MAX = 256

LANES = 128
SUBLANES = 8
Q_BLOCK = 128
KEY_CHUNK = 256
HEAD_GROUP = 8
CONV_HALO = 32
VMEM_LIMIT = 56 * 1024 * 1024
MASK_NEG = -1e30
INT_MIN = -2 ** 31


def _cparams(*sem):
    return pltpu.CompilerParams(dimension_semantics=sem, vmem_limit_bytes=VMEM_LIMIT)


def _sigmoid(x):
    return 1.0 / (1.0 + jnp.exp(-x))


def _silu(x):
    return x * _sigmoid(x)


def _gelu_tanh(x):
    return 0.5 * x * (1.0 + jnp.tanh(math.sqrt(2.0 / math.pi) * (x + 0.044715 * (x * x * x))))


def _rms(x, g):
    return x * lax.rsqrt(jnp.mean(x * x, axis=-1, keepdims=True) + NORM_EPS) * g


def _mm_kernel(*refs, n_w, n_ex, n_out, nk, cast_w, epilogue, a_fn):
    a_ref = refs[0]
    w_refs = refs[1:1 + n_w]
    ex_refs = refs[1 + n_w:1 + n_w + n_ex]
    out_refs = refs[1 + n_w + n_ex:1 + n_w + n_ex + n_out]
    scr_refs = refs[1 + n_w + n_ex + n_out:]
    a = a_fn(a_ref[...])
    if cast_w:
        i, j = pl.program_id(1), pl.program_id(0)

        @pl.when(i == 0)
        def _():
            for scr, w in zip(scr_refs, w_refs):
                scr[...] = w[...].astype(BF16)

        accs = [jnp.dot(a, scr[...], preferred_element_type=F32) for scr in scr_refs]
        epilogue(accs, ex_refs, out_refs, (i, j))
        return
    ids = (pl.program_id(0), pl.program_id(1))
    if nk == 1:
        accs = [jnp.dot(a, w[...], preferred_element_type=F32) for w in w_refs]
        epilogue(accs, ex_refs, out_refs, ids)
        return
    k = pl.program_id(2)

    @pl.when(k == 0)
    def _():
        for acc in scr_refs:
            acc[...] = jnp.zeros_like(acc)

    for acc, w in zip(scr_refs, w_refs):
        acc[...] += jnp.dot(a, w[...], preferred_element_type=F32)

    @pl.when(k == nk - 1)
    def _():
        epilogue([acc[...] for acc in scr_refs], ex_refs, out_refs, ids)


def _identity(v):
    return v


def _mm(a, a_spec, ws, extras, outs, epilogue, *, grid, name, acc_shapes=(), cast_w=False, a_fn=_identity):
    nk = grid[2] if len(grid) == 3 else 1
    in_specs = [a_spec] + [s for _, s in ws] + [s for _, s in extras]
    out_specs = [s for _, s in outs]
    if cast_w:
        assert nk == 1
        swap = lambda s: pl.BlockSpec(s.block_shape, lambda j, i, f=s.index_map: f(i, j))
        in_specs, out_specs = [swap(s) for s in in_specs], [swap(s) for s in out_specs]
        grid, sem = (grid[1], grid[0]), ("parallel", "arbitrary")
        scratch = [pltpu.VMEM(tuple(d for d in s.block_shape if d is not None), BF16) for _, s in ws]
    else:
        sem = ("parallel", "parallel") + (("arbitrary",) if nk > 1 else ())
        scratch = [pltpu.VMEM(s, F32) for s in acc_shapes] if nk > 1 else []
    kern = functools.partial(_mm_kernel, n_w=len(ws), n_ex=len(extras), n_out=len(outs), nk=nk, cast_w=cast_w,
                             epilogue=epilogue, a_fn=a_fn)
    return pl.pallas_call(
        kern, grid=grid, in_specs=in_specs, out_specs=out_specs, out_shape=[o for o, _ in outs],
        scratch_shapes=scratch, compiler_params=_cparams(*sem), name=name,
    )(a, *[w for w, _ in ws], *[e for e, _ in extras])


def _ep_resid(accs, ex, outs, ids):
    x_ref, gate_ref = ex
    outs[0][...] = x_ref[...] + gate_ref[0] * accs[0]


def _matmul_resid(a, w, x, gate, seq, *, name, tm=1024, tn=512, tk=4096):
    t, kdim = a.shape
    n = w.shape[1]
    tm, tn, tk = min(tm, seq), min(tn, n), min(tk, kdim)
    nk = kdim // tk
    grid = (t // tm, n // tn) + ((nk,) if nk > 1 else ())
    rows_per_batch = seq // tm
    out = _mm(
        a, pl.BlockSpec((tm, tk), lambda i, j, *k: (i, k[0] if k else 0)),
        [(w, pl.BlockSpec((tk, tn), lambda i, j, *k: (k[0] if k else 0, j)))],
        [(x, pl.BlockSpec((tm, tn), lambda i, j, *k: (i, j))),
         (gate, pl.BlockSpec((1, 1, tn), lambda i, j, *k: (i // rows_per_batch, 0, j)))],
        [(jax.ShapeDtypeStruct((t, n), F32), pl.BlockSpec((tm, tn), lambda i, j, *k: (i, j)))],
        _ep_resid, grid=grid, acc_shapes=[(tm, tn)], name=name, cast_w=w.dtype == F32)
    return out[0]


def _swiglu_act(h, w13, layer, *, tm=512, tn=512):
    t, d = h.shape
    f = w13.shape[2] // 2
    tm, tn = min(tm, t), min(tn, f)
    nf = f // tn

    def epilogue(accs, ex, outs, ids):
        outs[0][...] = (_silu(accs[0]) * accs[1]).astype(BF16)

    out = _mm(
        h, pl.BlockSpec((tm, d), lambda i, j: (i, 0)),
        [(w13, pl.BlockSpec((None, d, tn), lambda i, j: (layer, 0, j))),
         (w13, pl.BlockSpec((None, d, tn), lambda i, j: (layer, 0, j + nf)))],
        [],
        [(jax.ShapeDtypeStruct((t, f), BF16), pl.BlockSpec((tm, tn), lambda i, j: (i, j)))],
        epilogue, grid=(t // tm, nf), name="ffn_swiglu", cast_w=True)
    return out[0]


MOE_ROW_TILE = 512
MOE_DMA_TILE = 256
SEG_PAD = SUBLANES


class _SegLayout:
    def __init__(self, d):
        self.n_seg = d // LANES
        self.slab = -(-self.n_seg // SUBLANES) * SUBLANES
        self.pitch = self.slab + SEG_PAD

    def seg(self, s, n_rows):
        return pl.ds(s, n_rows, stride=self.pitch)

    def slab_of(self, row):
        return pl.ds(pl.multiple_of(row * self.pitch, SUBLANES), self.slab)

    def store(self, ref, value):
        n_rows = value.shape[0]
        for s in range(self.n_seg):
            ref[self.seg(s, n_rows), :] = value[:, s * LANES:(s + 1) * LANES]
        for s in range(self.n_seg, self.pitch):
            ref[self.seg(s, n_rows), :] = jnp.zeros((n_rows, LANES), ref.dtype)

    def row_copy(self, src, src_row, dst, dst_row, sem):
        return pltpu.make_async_copy(src.at[self.slab_of(src_row)], dst.at[self.slab_of(dst_row)], sem)


def _dispatch_kernel(pos_ref, h_ref, zero_hbm, out_hbm, sem, *, n_tok, tile, lay):
    del zero_hbm
    base = pl.program_id(0) * tile

    def copies(i):
        return [lay.row_copy(h_ref, i, out_hbm, pos_ref[k * n_tok + base + i], sem) for k in range(MOE_TOPK)]

    def start(i, carry):
        for cp in copies(i):
            cp.start()
        return carry

    def wait(i, carry):
        for cp in copies(i):
            cp.wait()
        return carry

    lax.fori_loop(0, tile, start, 0)
    lax.fori_loop(0, tile, wait, 0)


def _moe_dispatch(h_seg, pos, n_rows, lay):
    t = h_seg.shape[0] // lay.pitch
    tile = min(MOE_DMA_TILE, t)
    any_spec = pl.BlockSpec(memory_space=pl.ANY)
    return pl.pallas_call(
        functools.partial(_dispatch_kernel, n_tok=t, tile=tile, lay=lay),
        grid_spec=pltpu.PrefetchScalarGridSpec(
            num_scalar_prefetch=1, grid=(t // tile,),
            in_specs=[pl.BlockSpec((tile * lay.pitch, LANES), lambda i, p: (i, 0)), any_spec], out_specs=any_spec,
            scratch_shapes=[pltpu.SemaphoreType.DMA(())]),
        out_shape=jax.ShapeDtypeStruct((n_rows * lay.pitch, LANES), h_seg.dtype),
        input_output_aliases={2: 0},
        compiler_params=_cparams("arbitrary"), name="moe_dispatch")(
            pos, h_seg, jnp.zeros((n_rows * lay.pitch, LANES), h_seg.dtype))


def _grouped_swiglu_kernel(te_ref, nu_ref, a_ref, wg_ref, wu_ref, out_ref, a_scr, *, tm, lay):
    active = pl.program_id(1) < nu_ref[0]

    @pl.when(active)
    def _():
        for s in range(lay.n_seg):
            a_scr[:, s * LANES:(s + 1) * LANES] = a_ref[lay.seg(s, tm), :].astype(BF16)
        a = a_scr[...]
        g = jnp.dot(a, wg_ref[...], preferred_element_type=F32)
        u = jnp.dot(a, wu_ref[...], preferred_element_type=F32)
        out_ref[...] = (_silu(g) * u).astype(BF16)

    @pl.when(jnp.logical_not(active))
    def _():
        out_ref[...] = jnp.zeros_like(out_ref)


def _grouped_swiglu(a_seg, w13, tile_expert, n_used, lay, *, tn=512):
    rows, kdim = a_seg.shape[0] // lay.pitch, w13.shape[1]
    f = w13.shape[2] // 2
    tm, tn = MOE_ROW_TILE, min(tn, f)
    nb = f // tn
    return pl.pallas_call(
        functools.partial(_grouped_swiglu_kernel, tm=tm, lay=lay),
        grid_spec=pltpu.PrefetchScalarGridSpec(
            num_scalar_prefetch=2, grid=(nb, rows // tm),
            in_specs=[pl.BlockSpec((tm * lay.pitch, LANES), lambda j, r, te, nu: (r, 0)),
                      pl.BlockSpec((None, kdim, tn), lambda j, r, te, nu: (te[r], 0, j)),
                      pl.BlockSpec((None, kdim, tn), lambda j, r, te, nu: (te[r], 0, j + nb))],
            out_specs=pl.BlockSpec((tm, tn), lambda j, r, te, nu: (r, j)),
            scratch_shapes=[pltpu.VMEM((tm, kdim), BF16)]),
        out_shape=jax.ShapeDtypeStruct((rows, f), BF16),
        compiler_params=_cparams("parallel", "parallel"), name="moe_swiglu")(
            tile_expert, n_used, a_seg, w13, w13)


def _grouped_down_kernel(te_ref, nu_ref, a_ref, w_ref, out_ref, acc_ref, *, nk, lay):
    active = pl.program_id(0) < nu_ref[0]
    k = pl.program_id(1)

    @pl.when(active & (k == 0))
    def _():
        acc_ref[...] = jnp.zeros_like(acc_ref)

    @pl.when(active)
    def _():
        acc_ref[...] += jnp.dot(a_ref[...], w_ref[...], preferred_element_type=F32)

    @pl.when(active & (k == nk - 1))
    def _():
        lay.store(out_ref, acc_ref[...])

    @pl.when(jnp.logical_not(active) & (k == nk - 1))
    def _():
        out_ref[...] = jnp.zeros_like(out_ref)


def _grouped_down(a, w2, tile_expert, n_used, lay, *, tk=1024):
    rows, kdim = a.shape
    d = w2.shape[2]
    tm, tk = MOE_ROW_TILE, min(tk, kdim)
    nk = kdim // tk
    return pl.pallas_call(
        functools.partial(_grouped_down_kernel, nk=nk, lay=lay),
        grid_spec=pltpu.PrefetchScalarGridSpec(
            num_scalar_prefetch=2, grid=(rows // tm, nk),
            in_specs=[pl.BlockSpec((tm, tk), lambda r, k, te, nu: (r, k)),
                      pl.BlockSpec((None, tk, d), lambda r, k, te, nu: (te[r], k, 0))],
            out_specs=pl.BlockSpec((tm * lay.pitch, LANES), lambda r, k, te, nu: (r, 0)),
            scratch_shapes=[pltpu.VMEM((tm, d), F32)]),
        out_shape=jax.ShapeDtypeStruct((rows * lay.pitch, LANES), F32),
        compiler_params=_cparams("parallel", "arbitrary"), name="moe_down")(tile_expert, n_used, a, w2)


def _combine_kernel(pos_ref, y_hbm, x_ref, route_ref, gate_ref, out_ref, buf0, buf1, sem, *, n_tok, tile, lay):
    base = pl.program_id(0) * tile
    bufs = (buf0, buf1)

    def copies(i):
        return [lay.row_copy(y_hbm, pos_ref[k * n_tok + base + i], bufs[k], i, sem) for k in range(MOE_TOPK)]

    def start(i, carry):
        for cp in copies(i):
            cp.start()
        return carry

    def wait(i, carry):
        for cp in copies(i):
            cp.wait()
        return carry

    lax.fori_loop(0, tile, start, 0)
    lax.fori_loop(0, tile, wait, 0)
    route = route_ref[...]
    w0 = route[:, ROUTE_WEIGHT:ROUTE_WEIGHT + 1]
    w1 = route[:, ROUTE_WEIGHT + 1:ROUTE_WEIGHT + 2]
    for s in range(lay.n_seg):
        cols = slice(s * LANES, (s + 1) * LANES)
        y = w0 * buf0[lay.seg(s, tile), :] + w1 * buf1[lay.seg(s, tile), :]
        out_ref[:, cols] = x_ref[:, cols] + gate_ref[0][:, cols] * y


def _moe_combine(y_seg, pos, x, route, gate, seq, lay):
    t, d = x.shape
    tile = min(MOE_DMA_TILE, seq)
    rpb = seq // tile
    row = pl.BlockSpec((tile, d), lambda i, p: (i, 0))
    buf = pltpu.VMEM((tile * lay.pitch, LANES), F32)
    return pl.pallas_call(
        functools.partial(_combine_kernel, n_tok=t, tile=tile, lay=lay),
        grid_spec=pltpu.PrefetchScalarGridSpec(
            num_scalar_prefetch=1, grid=(t // tile,),
            in_specs=[pl.BlockSpec(memory_space=pl.ANY), row,
                      pl.BlockSpec((tile, LANES), lambda i, p: (i, 0)),
                      pl.BlockSpec((1, 1, d), lambda i, p: (i // rpb, 0, 0))],
            out_specs=row,
            scratch_shapes=[buf, buf, pltpu.SemaphoreType.DMA(())]),
        out_shape=jax.ShapeDtypeStruct((t, d), F32),
        compiler_params=_cparams("arbitrary"), name="moe_combine")(pos, y_seg, x, route, gate)


def _ada_mod(c, ada_w, ada_b, ada_table):
    bsz, d = c.shape
    b = 16
    c = jnp.pad(c, ((0, b - bsz), (0, 0)))
    depth = ada_table.shape[0]
    n = ada_w.shape[1]
    tn = min(512, n)

    def epilogue(accs, ex, outs, ids):
        bias_ref, tab_ref = ex
        outs[0][...] = (accs[0] + bias_ref[...])[None] + tab_ref[...]

    out = _mm(
        c, pl.BlockSpec((b, d), lambda i, j: (0, 0)),
        [(ada_w, pl.BlockSpec((d, tn), lambda i, j: (0, j)))],
        [(ada_b.reshape(1, n), pl.BlockSpec((1, tn), lambda i, j: (0, j))),
         (ada_table.reshape(depth, 1, n), pl.BlockSpec((depth, 1, tn), lambda i, j: (0, 0, j)))],
        [(jax.ShapeDtypeStruct((depth, b, n), F32), pl.BlockSpec((depth, b, tn), lambda i, j: (0, 0, j)))],
        epilogue, grid=(1, n // tn), name="ada_mod", cast_w=True, a_fn=lambda v: _silu(v).astype(BF16))
    return out[0][:, :bsz].reshape(depth, bsz, N_ADA, d)


def _norm_kernel(*refs, modulate, n_experts):
    if not modulate:
        x_ref, g_ref, out_ref = refs
        out_ref[...] = _rms(x_ref[...], g_ref[...])
        return
    x_ref, g_ref, sh_ref, sc_ref = refs[:4]
    h = _rms(x_ref[...], g_ref[...]) * (1.0 + sc_ref[0]) + sh_ref[0]
    if not n_experts:
        refs[4][...] = h.astype(BF16)
        return
    r_ref, hseg_ref, route_ref, count_ref, carry_ref = refs[4:]

    @pl.when(pl.program_id(0) == 0)
    def _():
        carry_ref[...] = jnp.zeros_like(carry_ref)

    _SegLayout(h.shape[1]).store(hseg_ref, h)
    logits = jnp.dot(h, r_ref[...], preferred_element_type=F32, precision=lax.Precision.HIGHEST)
    lane = lax.broadcasted_iota(jnp.int32, logits.shape, 1)
    lg = jnp.where(lane < n_experts, logits, -jnp.inf)
    m1 = jnp.max(lg, axis=1, keepdims=True)
    i1 = jnp.min(jnp.where(lg == m1, lane, LANES), axis=1, keepdims=True)
    lg2 = jnp.where(lane == i1, -jnp.inf, lg)
    m2 = jnp.max(lg2, axis=1, keepdims=True)
    i2 = jnp.min(jnp.where(lg2 == m2, lane, LANES), axis=1, keepdims=True)
    e2 = jnp.exp(m2 - m1)
    w1 = 1.0 / (1.0 + e2)
    sel = jnp.where((lane == i1) | (lane == i2), 1.0, 0.0)
    tm = sel.shape[0]
    lower = lax.broadcasted_iota(jnp.int32, (tm, tm), 0) >= lax.broadcasted_iota(jnp.int32, (tm, tm), 1)
    cum = jnp.dot(jnp.where(lower, 1.0, 0.0).astype(BF16), sel.astype(BF16), preferred_element_type=F32)
    rank = cum - sel + carry_ref[...]
    r1 = jnp.sum(jnp.where(lane == i1, rank, 0.0), axis=1, keepdims=True)
    r2 = jnp.sum(jnp.where(lane == i2, rank, 0.0), axis=1, keepdims=True)
    carry_ref[...] += jnp.sum(sel, axis=0, keepdims=True)
    count_ref[...] = carry_ref[...]
    cols = (i1.astype(F32), i2.astype(F32), r1, r2, w1, e2 * w1)
    table = jnp.zeros_like(logits)
    for k, col in enumerate(cols):
        table = jnp.where(lane == k, col, table)
    route_ref[...] = table


ROUTE_EXPERT, ROUTE_RANK, ROUTE_WEIGHT = 0, 2, 4


def _norm_mod(x, g, shift, scale, seq, router=None, *, tm=256):
    t, d = x.shape
    tm = min(tm, seq)
    rpb = seq // tm
    row = pl.BlockSpec((tm, d), lambda i: (i, 0))
    vec = pl.BlockSpec((1, d), lambda i: (0, 0))
    per_batch = pl.BlockSpec((1, 1, d), lambda i: (i // rpb, 0, 0))
    ins = [x, g.reshape(1, d), shift, scale]
    in_specs = [row, vec, per_batch, per_batch]
    out_shape = [jax.ShapeDtypeStruct((t, d), BF16)]
    out_specs = [row]
    n_experts = 0
    scratch = []
    if router is not None:
        n_experts = router.shape[1]
        pitch = _SegLayout(d).pitch
        out_shape = [jax.ShapeDtypeStruct((t * pitch, LANES), F32)]
        out_specs = [pl.BlockSpec((tm * pitch, LANES), lambda i: (i, 0))]
        ins.append(jnp.pad(router, ((0, 0), (0, LANES - n_experts))))
        in_specs.append(pl.BlockSpec((d, LANES), lambda i: (0, 0)))
        out_shape += [jax.ShapeDtypeStruct((t, LANES), F32), jax.ShapeDtypeStruct((1, LANES), F32)]
        out_specs += [pl.BlockSpec((tm, LANES), lambda i: (i, 0)), pl.BlockSpec((1, LANES), lambda i: (0, 0))]
        scratch = [pltpu.VMEM((1, LANES), F32)]
    res = pl.pallas_call(
        functools.partial(_norm_kernel, modulate=True, n_experts=n_experts),
        grid=(t // tm,), in_specs=in_specs, out_specs=out_specs, out_shape=out_shape, scratch_shapes=scratch,
        compiler_params=_cparams("arbitrary" if router is not None else "parallel"),
        name="norm_router" if router is not None else "norm_mod")(*ins)
    return res if router is not None else res[0]


def _final_norm(x, g, *, tm=256):
    t, d = x.shape
    tm = min(tm, t)
    row = pl.BlockSpec((tm, d), lambda i: (i, 0))
    return pl.pallas_call(
        functools.partial(_norm_kernel, modulate=False, n_experts=0),
        grid=(t // tm,), in_specs=[row, pl.BlockSpec((1, d), lambda i: (0, 0))], out_specs=row,
        out_shape=jax.ShapeDtypeStruct((t, d), F32), compiler_params=_cparams("parallel"),
        name="final_norm")(x, g.reshape(1, d))


def _conv_kernel(*refs, width, ts, tc, has_bias, has_mul):
    cur_ref, halo_ref, w_ref = refs[:3]
    rest = list(refs[3:])
    b_ref = rest.pop(0) if has_bias else None
    mul_ref = rest.pop(0) if has_mul else None
    out_ref, buf = rest
    first = pl.program_id(1) == 0
    buf[0:CONV_HALO, :] = jnp.where(first, 0.0, halo_ref[0])
    buf[CONV_HALO:CONV_HALO + ts, :] = cur_ref[0]
    rows = 32
    for r in range(0, ts, rows):
        acc = None
        for k in range(width):
            term = w_ref[k:k + 1, :] * buf[pl.ds(CONV_HALO + r - (width - 1) + k, rows), :]
            acc = term if acc is None else acc + term
        if has_bias:
            acc = acc + b_ref[...]
        if has_mul:
            acc = acc * mul_ref[0, r:r + rows, :].astype(F32)
        out_ref[0, r:r + rows, :] = acc.astype(out_ref.dtype)


def _causal_conv(x, conv_w, bias=None, mul=None, out_dtype=F32, *, ts=256, tc=512):
    b, s, ch = x.shape
    width = conv_w.shape[0]
    assert width - 1 <= CONV_HALO
    ts, tc = min(ts, s), min(tc, ch)
    hpb = ts // CONV_HALO
    blk = pl.BlockSpec((1, ts, tc), lambda bi, si, ci: (bi, si, ci))
    ins = [x, x, conv_w]
    in_specs = [blk,
                pl.BlockSpec((1, CONV_HALO, tc), lambda bi, si, ci: (bi, jnp.maximum(si * hpb - 1, 0), ci)),
                pl.BlockSpec((width, tc), lambda bi, si, ci: (0, ci))]
    if bias is not None:
        ins.append(bias.reshape(1, ch))
        in_specs.append(pl.BlockSpec((1, tc), lambda bi, si, ci: (0, ci)))
    if mul is not None:
        ins.append(mul)
        in_specs.append(blk)
    return pl.pallas_call(
        functools.partial(_conv_kernel, width=width, ts=ts, tc=tc, has_bias=bias is not None,
                          has_mul=mul is not None),
        grid=(b, s // ts, ch // tc), in_specs=in_specs, out_specs=blk,
        out_shape=jax.ShapeDtypeStruct((b, s, ch), out_dtype),
        scratch_shapes=[pltpu.VMEM((CONV_HALO + ts, tc), F32)],
        compiler_params=_cparams("parallel", "parallel", "parallel"), name=f"causal_conv{width}")(*ins)


def _ln_silu_kernel(x_ref, g_ref, b_ref, out_ref):
    x = x_ref[...]
    mu = jnp.mean(x, axis=-1, keepdims=True)
    xc = x - mu
    var = jnp.mean(xc * xc, axis=-1, keepdims=True)
    y = xc * lax.rsqrt(var + NORM_EPS) * g_ref[...] + b_ref[...]
    out_ref[...] = _silu(y).astype(BF16)


def _ln_silu(x, g, b, *, tm=256):
    t, d = x.shape
    tm = min(tm, t)
    row = pl.BlockSpec((tm, d), lambda i: (i, 0))
    vec = pl.BlockSpec((1, d), lambda i: (0, 0))
    return pl.pallas_call(
        _ln_silu_kernel, grid=(t // tm,), in_specs=[row, vec, vec], out_specs=row,
        out_shape=jax.ShapeDtypeStruct((t, d), BF16), compiler_params=_cparams("parallel"), name="ln_silu")(
            x, g.reshape(1, d), b.reshape(1, d))


def _rglru_kernel(xc_ref, gate_ref, wa_ref, wx_ref, ba_ref, bx_ref, lam_ref, out_ref,
                  a_scr, u_scr, h_scr, *, ts, n_heads, blk, cw):
    @pl.when(pl.program_id(1) == 0)
    def _():
        h_scr[...] = jnp.zeros_like(h_scr)

    neg_lam = -lam_ref[...]
    softplus = jnp.maximum(neg_lam, 0.0) + jnp.log1p(jnp.exp(-jnp.abs(neg_lam)))
    for hd in range(n_heads):
        sl = slice(hd * blk, (hd + 1) * blk)
        xh = xc_ref[:, sl]
        xb = xh.astype(BF16)
        r = _sigmoid(jnp.dot(xb, wa_ref[hd], preferred_element_type=F32) + ba_ref[:, sl])
        i_g = _sigmoid(jnp.dot(xb, wx_ref[hd], preferred_element_type=F32) + bx_ref[:, sl])
        log_a = -LRU_C * r * softplus[:, sl]
        a_scr[:, sl] = jnp.exp(log_a)
        u_scr[:, sl] = jnp.sqrt(1.0 - jnp.exp(2.0 * log_a)) * (i_g * xh)

    row = lax.broadcasted_iota(jnp.int32, (SUBLANES, cw), 0)
    width = n_heads * blk
    for c0 in range(0, width, cw):
        def body(g, h_prev, c0=c0):
            r0 = pl.multiple_of(g * SUBLANES, SUBLANES)
            a8 = a_scr[pl.ds(r0, SUBLANES), c0:c0 + cw]
            b8 = u_scr[pl.ds(r0, SUBLANES), c0:c0 + cw]
            for d in (1, 2, 4):
                keep = row >= d
                b8 = jnp.where(keep, a8 * pltpu.roll(b8, d, axis=0) + b8, b8)
                a8 = jnp.where(keep, a8 * pltpu.roll(a8, d, axis=0), a8)
            h8 = b8 + a8 * h_prev
            u_scr[pl.ds(r0, SUBLANES), c0:c0 + cw] = h8
            return h8[SUBLANES - 1:SUBLANES, :]

        h_scr[:, c0:c0 + cw] = lax.fori_loop(0, ts // SUBLANES, body, h_scr[:, c0:c0 + cw])
    out_ref[...] = (u_scr[...] * gate_ref[...].astype(F32)).astype(BF16)


def _rglru(xc, gate, w_a, w_x, b_a, b_x, lam, seq, *, ts=256):
    t, width = xc.shape
    n_heads, blk, _ = w_a.shape
    ts = min(ts, seq)
    nts = seq // ts
    row = pl.BlockSpec((ts, width), lambda b, j: (b * nts + j, 0))
    wsp = pl.BlockSpec((n_heads, blk, blk), lambda b, j: (0, 0, 0))
    vec = pl.BlockSpec((1, width), lambda b, j: (0, 0))
    return pl.pallas_call(
        functools.partial(_rglru_kernel, ts=ts, n_heads=n_heads, blk=blk, cw=min(1024, width)),
        grid=(t // seq, nts), in_specs=[row, row, wsp, wsp, vec, vec, vec], out_specs=row,
        out_shape=jax.ShapeDtypeStruct((t, width), BF16),
        scratch_shapes=[pltpu.VMEM((ts, width), F32), pltpu.VMEM((ts, width), F32), pltpu.VMEM((1, width), F32)],
        compiler_params=_cparams("parallel", "arbitrary"), name="rglru")(
            xc, gate, w_a, w_x, b_a.reshape(1, width), b_x.reshape(1, width), lam.reshape(1, width))


def _bias_tile_kernel(rb_ref, out_ref):
    hd = pl.program_id(0)
    shape = (2 * Q_BLOCK, Q_BLOCK)
    dist = Q_BLOCK + lax.broadcasted_iota(jnp.int32, shape, 1) - lax.broadcasted_iota(jnp.int32, shape, 0)
    dist = jnp.maximum(dist, 0)
    max_exact = REL_BUCKETS // 2
    large = max_exact + (jnp.log(jnp.maximum(dist, 1).astype(F32) / max_exact)
                         / math.log(REL_MAX_DIST / max_exact) * (REL_BUCKETS - max_exact)).astype(jnp.int32)
    bucket = jnp.where(dist < max_exact, dist, jnp.minimum(large, REL_BUCKETS - 1))
    tile = jnp.zeros(shape, F32)
    for bkt in range(REL_BUCKETS):
        tile = jnp.where(bucket == bkt, rb_ref[bkt, hd], tile)
    out_ref[0] = tile - rb_ref[REL_BUCKETS - 1, hd]


def _bias_tile(rel_bias):
    n_heads = rel_bias.shape[1]
    return pl.pallas_call(
        _bias_tile_kernel, grid=(n_heads,),
        in_specs=[pl.BlockSpec(memory_space=pltpu.SMEM)],
        out_specs=pl.BlockSpec((1, 2 * Q_BLOCK, Q_BLOCK), lambda h: (h, 0, 0)),
        out_shape=jax.ShapeDtypeStruct((n_heads, 2 * Q_BLOCK, Q_BLOCK), F32),
        compiler_params=_cparams("parallel"), name="dsa_bias_tile")(rel_bias)


def _dsa_kernel(qidx_ref, wcol_ref, kidx_ref, ckv_ref, ckvt_ref, qlat_ref, bias_ref, wuv_ref, out_ref,
                score_ref, key_ref, madd_ref, z_ref, acc_ref, *, n_slab, k_sel, idx_heads, kv_dim, head_dim):
    qb = pl.program_id(1)
    cols = HEAD_GROUP * Q_BLOCK
    sub = KEY_CHUNK // SUBLANES
    nt = (((1,), (1,)), ((), ()))
    n_chunks = (qb + 2) // 2

    @pl.when(pl.program_id(2) == 0)
    def _select():
        def chunk_body(c, carry):
            k0 = pl.multiple_of(c * KEY_CHUNK, KEY_CHUNK)
            keys = kidx_ref[0, pl.ds(k0, KEY_CHUNK), :]

            def head_body(hg, acc):
                h0 = pl.multiple_of(hg * HEAD_GROUP, HEAD_GROUP)
                q = qidx_ref[0, pl.ds(h0, HEAD_GROUP)].reshape(cols, LANES)
                dots = lax.dot_general(q, keys, nt, preferred_element_type=F32)
                dots = jnp.maximum(dots, 0.0).reshape(HEAD_GROUP, Q_BLOCK, KEY_CHUNK)
                return acc + jnp.sum(dots * wcol_ref[0, pl.ds(h0, HEAD_GROUP)], axis=0)

            acc = lax.fori_loop(0, idx_heads // HEAD_GROUP, head_body, jnp.zeros((Q_BLOCK, KEY_CHUNK), F32),
                                unroll=2)
            score_ref[pl.ds(k0, KEY_CHUNK), :] = acc.T
            return carry

        lax.fori_loop(0, n_chunks, chunk_body, 0)

        def zero_body(c, carry):
            score_ref[pl.ds(pl.multiple_of(c * KEY_CHUNK, KEY_CHUNK), KEY_CHUNK), :] = jnp.zeros(
                (KEY_CHUNK, Q_BLOCK), F32)
            return carry

        lax.fori_loop(n_chunks, n_slab // 2, zero_body, 0)

        shape = (n_slab * LANES, Q_BLOCK)
        bits = pltpu.bitcast(score_ref[...], jnp.int32)
        key = jnp.where(bits >= 0, bits, bits ^ jnp.int32(0x7FFFFFFF))
        causal = lax.broadcasted_iota(jnp.int32, shape, 0) <= qb * Q_BLOCK + lax.broadcasted_iota(jnp.int32, shape, 1)
        key = jnp.where(causal, key, INT_MIN)
        key_ref[...] = key

        def bit_body(it, thr):
            cand = thr + lax.shift_left(jnp.int32(1), 31 - it)
            cand_b = jnp.broadcast_to(cand, (SUBLANES, Q_BLOCK))[None, None]
            lanes_of_sum = (SUBLANES, SUBLANES, Q_BLOCK)

            def count_body(c, cnt):
                kc = key_ref[pl.ds(pl.multiple_of(c * 2 * KEY_CHUNK, 2 * KEY_CHUNK), 2 * KEY_CHUNK), :]
                hit = jnp.where(kc.reshape((2 * sub // SUBLANES,) + lanes_of_sum) >= cand_b, 1.0, 0.0)
                return cnt + jnp.sum(hit, axis=0)

            cnt = lax.fori_loop(0, (n_chunks + 1) // 2, count_body, jnp.zeros(lanes_of_sum, F32))
            cnt = jnp.sum(jnp.sum(cnt, axis=0), axis=0, keepdims=True)
            return jnp.where(cnt >= k_sel, cand, thr)

        thr = lax.fori_loop(0, 32, bit_body, jnp.full((1, Q_BLOCK), INT_MIN, jnp.int32))
        madd_ref[0:Q_BLOCK, :] = jnp.full((Q_BLOCK, Q_BLOCK), MASK_NEG, F32)
        madd_ref[Q_BLOCK:, :] = jnp.where(causal & (key >= thr), 0.0, MASK_NEG)

    q = qlat_ref[0].reshape(cols, kv_dim)
    n_far = qb // 2
    near_row = qb * Q_BLOCK

    def logits_of(kv):
        return lax.dot_general(kv, q, nt, preferred_element_type=F32)

    def per_head(m):
        return jnp.concatenate([m] * HEAD_GROUP, axis=1)

    def col_max(z):
        return jnp.max(z.reshape(sub, SUBLANES, cols), axis=0)

    def far_logits(pair, m8):
        for c in (2 * pair, 2 * pair + 1):
            r0 = pl.multiple_of(Q_BLOCK + c * KEY_CHUNK, Q_BLOCK)
            kv = ckv_ref[0, pl.ds(r0, KEY_CHUNK), :]
            row = r0 + lax.broadcasted_iota(jnp.int32, (KEY_CHUNK, Q_BLOCK), 0)
            madd = jnp.where(row < near_row, madd_ref[pl.ds(r0, KEY_CHUNK), :], MASK_NEG)
            z = logits_of(kv) + per_head(madd)
            z_ref[c] = z
            m8 = jnp.maximum(m8, col_max(z))
        return m8

    m8 = lax.fori_loop(0, (n_far + 1) // 2, far_logits, jnp.full((SUBLANES, cols), MASK_NEG, F32))
    r0 = pl.multiple_of(near_row, Q_BLOCK)
    bias = jnp.concatenate([bias_ref[hh] for hh in range(HEAD_GROUP)], axis=1)
    z = (logits_of(ckv_ref[0, pl.ds(r0, KEY_CHUNK), :]) + per_head(madd_ref[pl.ds(r0, KEY_CHUNK), :])) + bias
    z_ref[n_far] = z
    z_ref[n_far + 1] = jnp.full((KEY_CHUNK, cols), MASK_NEG, F32)
    m8 = jnp.maximum(m8, col_max(z))
    m8 = jnp.broadcast_to(jnp.max(m8, axis=0, keepdims=True), (SUBLANES, cols))[None]
    acc_ref[...] = jnp.zeros(acc_ref.shape, F32)

    def weigh(pair, l8):
        for c in (2 * pair, 2 * pair + 1):
            s0 = jnp.where(c < n_far, 2 * c + 1, qb)
            kvt = jnp.concatenate([ckvt_ref[0, s0], ckvt_ref[0, s0 + 1]], axis=1)
            p = jnp.exp(z_ref[c].reshape(sub, SUBLANES, cols) - m8)
            acc_ref[...] += jnp.dot(kvt, p.reshape(KEY_CHUNK, cols).astype(BF16), preferred_element_type=F32)
            l8 = l8 + jnp.sum(p, axis=0)
        return l8

    l8 = lax.fori_loop(0, (n_far + 2) // 2, weigh, jnp.zeros((SUBLANES, cols), F32))
    inv = 1.0 / jnp.broadcast_to(jnp.sum(l8, axis=0, keepdims=True), (SUBLANES, cols))
    o_lat = (acc_ref[...].reshape(kv_dim // SUBLANES, SUBLANES, cols) * inv[None]).reshape(kv_dim, cols)
    o_lat = o_lat.astype(BF16)
    for hh in range(HEAD_GROUP):
        o_h = jnp.dot(wuv_ref[hh], o_lat[:, hh * Q_BLOCK:(hh + 1) * Q_BLOCK], preferred_element_type=F32)
        out_ref[:, hh * head_dim:(hh + 1) * head_dim] = o_h.T.astype(BF16)


def _dsa_core(qidx, wcol, kidx, ckv_pad, ckvt_pad, qlat, bias, w_uvt, *, bsz, seq, k_sel):
    n_qb = seq // Q_BLOCK
    idx_heads = qidx.shape[1]
    n_heads, kv_dim = qlat.shape[1], qlat.shape[3]
    head_dim = w_uvt.shape[1]
    n_groups = n_heads // HEAD_GROUP
    cols = HEAD_GROUP * Q_BLOCK
    blk_q = lambda b, i, g: (b * n_qb + i, 0, 0, 0)
    per_batch = lambda b, i, g: (b, 0, 0)
    return pl.pallas_call(
        functools.partial(_dsa_kernel, n_slab=n_qb, k_sel=k_sel, idx_heads=idx_heads, kv_dim=kv_dim,
                          head_dim=head_dim),
        grid=(bsz, n_qb, n_groups),
        in_specs=[
            pl.BlockSpec((1, idx_heads, Q_BLOCK, LANES), blk_q),
            pl.BlockSpec((1, idx_heads, Q_BLOCK, 1), blk_q),
            pl.BlockSpec((1, seq, LANES), per_batch),
            pl.BlockSpec((1, seq + Q_BLOCK, kv_dim), per_batch),
            pl.BlockSpec((1, n_qb + 1, kv_dim, Q_BLOCK), lambda b, i, g: (b, 0, 0, 0)),
            pl.BlockSpec((1, HEAD_GROUP, Q_BLOCK, kv_dim), lambda b, i, g: (b * n_qb + i, g, 0, 0)),
            pl.BlockSpec((HEAD_GROUP, KEY_CHUNK, Q_BLOCK), lambda b, i, g: (g, 0, 0)),
            pl.BlockSpec((HEAD_GROUP, head_dim, kv_dim), lambda b, i, g: (g, 0, 0)),
        ],
        out_specs=pl.BlockSpec((Q_BLOCK, HEAD_GROUP * head_dim), lambda b, i, g: (b * n_qb + i, g)),
        out_shape=jax.ShapeDtypeStruct((bsz * seq, n_heads * head_dim), BF16),
        scratch_shapes=[
            pltpu.VMEM((seq, Q_BLOCK), F32),
            pltpu.VMEM((seq, Q_BLOCK), jnp.int32),
            pltpu.VMEM((seq + Q_BLOCK, Q_BLOCK), F32),
            pltpu.VMEM((n_qb // 2 + 1, KEY_CHUNK, cols), F32),
            pltpu.VMEM((kv_dim, cols), F32),
        ],
        compiler_params=_cparams("parallel", "parallel", "arbitrary"), name="dsa_core")(
            qidx, wcol, kidx, ckv_pad, ckvt_pad, qlat, bias, w_uvt)


def _dsa_attention(h, x, gate, rel_bias, w_in, g_cq, g_ckv, w_uq, w_uk, w_uv, w_qidx, w_out, *, bsz, seq):
    t, d = h.shape
    q_lora, kv_lora = g_cq.shape[0], g_ckv.shape[0]
    n_heads, head_dim = w_uk.shape[1], w_uk.shape[2]
    idx_dim = LANES
    idx_heads = w_qidx.shape[1] // idx_dim
    k_sel = min(TOPK_MAX, seq // 4)
    n_qb = seq // Q_BLOCK
    tm = min(512, seq)
    qpt = tm // Q_BLOCK

    w_in = w_in.astype(BF16)
    splits = (0, q_lora, q_lora + kv_lora, q_lora + kv_lora + idx_dim, w_in.shape[1])
    w_parts = [w_in[:, splits[i]:splits[i + 1]] for i in range(4)]
    head_scale = idx_heads ** -0.5 * idx_dim ** -0.5

    def in_epilogue(accs, ex, outs, ids):
        outs[0][...] = _rms(accs[0], ex[0][...]).astype(BF16)
        outs[1][...] = _rms(accs[1], ex[1][...]).astype(BF16)
        outs[2][...] = accs[2].astype(BF16)
        outs[3][...] = accs[3] * head_scale

    full = lambda n: pl.BlockSpec((d, n), lambda i, j: (0, 0))
    rowsp = lambda n: pl.BlockSpec((tm, n), lambda i, j: (i, 0))
    vecsp = lambda n: pl.BlockSpec((1, n), lambda i, j: (0, 0))
    widths = [w.shape[1] for w in w_parts]
    c_q, c_kv, k_idx, w_head = _mm(
        h, pl.BlockSpec((tm, d), lambda i, j: (i, 0)),
        [(w, full(n)) for w, n in zip(w_parts, widths)],
        [(g_cq.reshape(1, q_lora), vecsp(q_lora)), (g_ckv.reshape(1, kv_lora), vecsp(kv_lora))],
        [(jax.ShapeDtypeStruct((t, n), dt), rowsp(n)) for n, dt in zip(widths, (BF16, BF16, BF16, F32))],
        in_epilogue, grid=(t // tm, 1), name="dsa_in_proj")

    hpt = 4
    tn = hpt * idx_dim

    def qidx_epilogue(accs, ex, outs, ids):
        for hh in range(hpt):
            outs[0][:, hh] = accs[0][:, hh * idx_dim:(hh + 1) * idx_dim].reshape(qpt, Q_BLOCK, idx_dim).astype(BF16)

    qidx = _mm(
        c_q, pl.BlockSpec((tm, q_lora), lambda i, j: (i, 0)),
        [(w_qidx, pl.BlockSpec((q_lora, tn), lambda i, j: (0, j)))], [],
        [(jax.ShapeDtypeStruct((t // Q_BLOCK, idx_heads, Q_BLOCK, idx_dim), BF16),
          pl.BlockSpec((qpt, hpt, Q_BLOCK, idx_dim), lambda i, j: (i, j, 0, 0)))],
        qidx_epilogue, grid=(t // tm, idx_heads // hpt), name="dsa_qidx", cast_w=True)[0]

    w_ukt = jnp.transpose(w_uk, (1, 2, 0)).astype(BF16)
    logit_scale = head_dim ** -0.5

    def qlat_kernel(cq_ref, wuq_ref, wuk_ref, out_ref):
        qg = jnp.dot(cq_ref[...], wuq_ref[...], preferred_element_type=F32).astype(BF16)
        for hh in range(HEAD_GROUP):
            ql = jnp.dot(qg[:, hh * head_dim:(hh + 1) * head_dim], wuk_ref[hh], preferred_element_type=F32)
            out_ref[:, hh] = (ql * logit_scale).reshape(qpt, Q_BLOCK, kv_lora).astype(BF16)

    qlat = pl.pallas_call(
        qlat_kernel, grid=(t // tm, n_heads // HEAD_GROUP),
        in_specs=[pl.BlockSpec((tm, q_lora), lambda i, g: (i, 0)),
                  pl.BlockSpec((q_lora, HEAD_GROUP * head_dim), lambda i, g: (0, g)),
                  pl.BlockSpec((HEAD_GROUP, head_dim, kv_lora), lambda i, g: (g, 0, 0))],
        out_specs=pl.BlockSpec((qpt, HEAD_GROUP, Q_BLOCK, kv_lora), lambda i, g: (i, g, 0, 0)),
        out_shape=jax.ShapeDtypeStruct((t // Q_BLOCK, n_heads, Q_BLOCK, kv_lora), BF16),
        compiler_params=_cparams("parallel", "parallel"), name="dsa_qlat")(c_q, w_uq.astype(BF16), w_ukt)

    wcol = w_head.reshape(bsz * n_qb, Q_BLOCK, idx_heads).transpose(0, 2, 1)[..., None]
    ckv_pad = jnp.pad(c_kv.reshape(bsz, seq, kv_lora), ((0, 0), (Q_BLOCK, 0), (0, 0)))
    ckvt_pad = ckv_pad.reshape(bsz, n_qb + 1, Q_BLOCK, kv_lora).transpose(0, 1, 3, 2)
    w_uvt = jnp.transpose(w_uv, (1, 2, 0)).astype(BF16)
    o = _dsa_core(qidx, wcol, k_idx.reshape(bsz, seq, idx_dim), ckv_pad, ckvt_pad, qlat,
                  _bias_tile(rel_bias.astype(F32)), w_uvt, bsz=bsz, seq=seq, k_sel=k_sel)
    return _matmul_resid(o, w_out, x, gate, seq, name="dsa_out_proj")


def _split_proj(h, w, n_parts, epilogue, out_dtypes, *, name, tm=512, tn=512):
    t, d = h.shape
    n = w.shape[1] // n_parts
    tm, tn = min(tm, t), min(tn, n)
    nb = n // tn
    out_sp = pl.BlockSpec((tm, tn), lambda i, j: (i, j))
    return _mm(
        h, pl.BlockSpec((tm, d), lambda i, j: (i, 0)),
        [(w, pl.BlockSpec((d, tn), lambda i, j, p=p: (0, j + p * nb))) for p in range(n_parts)], [],
        [(jax.ShapeDtypeStruct((t, n), dt), out_sp) for dt in out_dtypes],
        epilogue, grid=(t // tm, nb), name=name, cast_w=True)


def _short_conv_mixer(h, x, gate, w_in, conv_w, w_out, *, bsz, seq):
    def epilogue(accs, ex, outs, ids):
        outs[0][...] = accs[0].astype(BF16)
        outs[1][...] = accs[1] * accs[2]

    gate_b, cx = _split_proj(h, w_in, 3, epilogue, (BF16, F32), name="sconv_in_proj", tn=256)
    ch = cx.shape[1]
    y = _causal_conv(cx.reshape(bsz, seq, ch), conv_w, mul=gate_b.reshape(bsz, seq, ch), out_dtype=BF16)
    return _matmul_resid(y.reshape(bsz * seq, ch), w_out, x, gate, seq, name="sconv_out_proj")


def _conformer_mixer(h, x, gate, w_in, conv_w, conv_b, ln_g, ln_b, w_out, *, bsz, seq):
    def epilogue(accs, ex, outs, ids):
        outs[0][...] = accs[0] * _sigmoid(accs[1])

    u = _split_proj(h, w_in, 2, epilogue, (F32,), name="conf_in_proj")[0]
    ch = u.shape[1]
    u = _causal_conv(u.reshape(bsz, seq, ch), conv_w, bias=conv_b)
    u = _ln_silu(u.reshape(bsz * seq, ch), ln_g, ln_b)
    return _matmul_resid(u, w_out, x, gate, seq, name="conf_out_proj")


def _rglru_mixer(h, x, gate, w_in, conv_w, conv_b, w_a, b_a, w_x, b_x, lam, w_out, *, bsz, seq):
    def epilogue(accs, ex, outs, ids):
        outs[0][...] = _gelu_tanh(accs[0]).astype(BF16)
        outs[1][...] = accs[1]

    gate_br, x_br = _split_proj(h, w_in, 2, epilogue, (BF16, F32), name="lru_in_proj")
    width = x_br.shape[1]
    xc = _causal_conv(x_br.reshape(bsz, seq, width), conv_w, bias=conv_b)
    y = _rglru(xc.reshape(bsz * seq, width), gate_br, w_a.astype(BF16), w_x.astype(BF16), b_a, b_x, lam, seq)
    return _matmul_resid(y, w_out, x, gate, seq, name="lru_out_proj")


def _dense_ffn(h, x, gate, w13, w2, layer, *, seq):
    act = _swiglu_act(h, w13, layer)
    return _matmul_resid(act, w2[layer].astype(BF16), x, gate, seq, name="ffn_down", tm=512, tk=w2.shape[1])


def _moe_ffn(h_seg, route, counts, x, gate, w13, w2, *, seq):
    t = x.shape[0]
    n_e = w13.shape[0]
    tile = MOE_ROW_TILE
    n_rows = MOE_TOPK * t + n_e * tile
    counts = counts[0, :n_e].astype(jnp.int32)
    padded = (counts + tile - 1) // tile * tile
    ends = jnp.cumsum(padded)
    starts = ends - padded
    experts = route[:, ROUTE_EXPERT:ROUTE_EXPERT + MOE_TOPK].astype(jnp.int32)
    ranks = route[:, ROUTE_RANK:ROUTE_RANK + MOE_TOPK].astype(jnp.int32)
    pos = (starts[experts] + ranks).T.reshape(MOE_TOPK * t)
    tile_start = jnp.arange(n_rows // tile, dtype=jnp.int32) * tile
    tile_expert = jnp.minimum(jnp.sum(tile_start[:, None] >= ends[None, :], axis=1), n_e - 1).astype(jnp.int32)
    n_used = (ends[-1:] // tile).astype(jnp.int32)

    lay = _SegLayout(x.shape[1])
    hs = _moe_dispatch(h_seg, pos, n_rows, lay)
    act = _grouped_swiglu(hs, w13.astype(BF16), tile_expert, n_used, lay)
    y = _grouped_down(act, w2.astype(BF16), tile_expert, n_used, lay)
    return _moe_combine(y, pos, x, route, gate, seq, lay)


def kernel(x, c, ada_w, ada_b, ada_table, norm_mix, norm_ffn, norm_final, rel_bias, att_w_in, att_g_cq, att_g_ckv, att_w_uq, att_w_uk, att_w_uv, att_w_qidx, att_w_out, sconv_w_in, sconv_conv_w, sconv_w_out, conf_w_in, conf_conv_w, conf_conv_b, conf_ln_g, conf_ln_b, conf_w_out, lru_w_in, lru_conv_w, lru_conv_b, lru_w_a, lru_b_a, lru_w_x, lru_b_x, lru_lambda, lru_w_out, ffn_w13, ffn_w2, moe_router, moe_w13, moe_w2):
    bsz, seq, d = x.shape
    depth = ada_table.shape[0]
    xf = x.reshape(bsz * seq, d)
    mods = _ada_mod(c, ada_w, ada_b, ada_table)
    for i in range(depth):
        shift_m, scale_m, gate_m, shift_f, scale_f, gate_f = (mods[i][:, k:k + 1, :] for k in range(N_ADA))
        h = _norm_mod(xf, norm_mix[i], shift_m, scale_m, seq)
        kind, j = i % 4, i // 4
        if kind == 0:
            xf = _dsa_attention(h, xf, gate_m, rel_bias, att_w_in[j], att_g_cq[j], att_g_ckv[j], att_w_uq[j],
                                att_w_uk[j], att_w_uv[j], att_w_qidx[j], att_w_out[j], bsz=bsz, seq=seq)
        elif kind == 1:
            xf = _short_conv_mixer(h, xf, gate_m, sconv_w_in[j], sconv_conv_w[j], sconv_w_out[j], bsz=bsz, seq=seq)
        elif kind == 2:
            xf = _conformer_mixer(h, xf, gate_m, conf_w_in[j], conf_conv_w[j], conf_conv_b[j], conf_ln_g[j],
                                  conf_ln_b[j], conf_w_out[j], bsz=bsz, seq=seq)
        else:
            xf = _rglru_mixer(h, xf, gate_m, lru_w_in[j], lru_conv_w[j], lru_conv_b[j], lru_w_a[j], lru_b_a[j],
                              lru_w_x[j], lru_b_x[j], lru_lambda[j], lru_w_out[j], bsz=bsz, seq=seq)
        if i % 2 == 0:
            h = _norm_mod(xf, norm_ffn[i], shift_f, scale_f, seq)
            xf = _dense_ffn(h, xf, gate_f, ffn_w13, ffn_w2, i // 2, seq=seq)
        else:
            h, route, counts = _norm_mod(xf, norm_ffn[i], shift_f, scale_f, seq, router=moe_router[i // 2])
            xf = _moe_ffn(h, route, counts, xf, gate_f, moe_w13[i // 2], moe_w2[i // 2], seq=seq)
    return _final_norm(xf, norm_final).reshape(bsz, seq, d)
```

```python
import functools
import math

import jax
import jax.numpy as jnp
from jax import lax
from jax.experimental import pallas as pl
from jax.experimental.pallas import tpu as pltpu

F32 = jnp.float32
BF16 = jnp.bfloat16

NORM_EPS = 1e-6
N_ADA = 6
MOE_TOPK = 2
LRU_C = 8.0
REL_BUCKETS = 32
REL_MAX_DIST = 128
TOPK_MAX = 256

LANES = 128
SUBLANES = 8
Q_BLOCK = 128
KEY_CHUNK = 256
HEAD_GROUP = 8
CONV_HALO = 32
VMEM_LIMIT = 56 * 1024 * 1024
MASK_NEG = -1e30
INT_MIN = -2 ** 31


def _cparams(*sem):
    return pltpu.CompilerParams(dimension_semantics=sem, vmem_limit_bytes=VMEM_LIMIT)


def _sigmoid(x):
    return 1.0 / (1.0 + jnp.exp(-x))


def _silu(x):
    return x * _sigmoid(x)


def _gelu_tanh(x):
    return 0.5 * x * (1.0 + jnp.tanh(math.sqrt(2.0 / math.pi) * (x + 0.044715 * (x * x * x))))


def _rms(x, g):
    return x * lax.rsqrt(jnp.mean(x * x, axis=-1, keepdims=True) + NORM_EPS) * g


def _mm_kernel(*refs, n_w, n_ex, n_out, nk, cast_w, epilogue, a_fn):
    a_ref = refs[0]
    w_refs = refs[1:1 + n_w]
    ex_refs = refs[1 + n_w:1 + n_w + n_ex]
    out_refs = refs[1 + n_w + n_ex:1 + n_w + n_ex + n_out]
    scr_refs = refs[1 + n_w + n_ex + n_out:]
    a = a_fn(a_ref[...])
    if cast_w:
        i, j = pl.program_id(1), pl.program_id(0)

        @pl.when(i == 0)
        def _():
            for scr, w in zip(scr_refs, w_refs):
                scr[...] = w[...].astype(BF16)

        accs = [jnp.dot(a, scr[...], preferred_element_type=F32) for scr in scr_refs]
        epilogue(accs, ex_refs, out_refs, (i, j))
        return
    ids = (pl.program_id(0), pl.program_id(1))
    if nk == 1:
        accs = [jnp.dot(a, w[...], preferred_element_type=F32) for w in w_refs]
        epilogue(accs, ex_refs, out_refs, ids)
        return
    k = pl.program_id(2)

    @pl.when(k == 0)
    def _():
        for acc in scr_refs:
            acc[...] = jnp.zeros_like(acc)

    for acc, w in zip(scr_refs, w_refs):
        acc[...] += jnp.dot(a, w[...], preferred_element_type=F32)

    @pl.when(k == nk - 1)
    def _():
        epilogue([acc[...] for acc in scr_refs], ex_refs, out_refs, ids)


def _identity(v):
    return v


def _mm(a, a_spec, ws, extras, outs, epilogue, *, grid, name, acc_shapes=(), cast_w=False, a_fn=_identity):
    nk = grid[2] if len(grid) == 3 else 1
    in_specs = [a_spec] + [s for _, s in ws] + [s for _, s in extras]
    out_specs = [s for _, s in outs]
    if cast_w:
        assert nk == 1
        swap = lambda s: pl.BlockSpec(s.block_shape, lambda j, i, f=s.index_map: f(i, j))
        in_specs, out_specs = [swap(s) for s in in_specs], [swap(s) for s in out_specs]
        grid, sem = (grid[1], grid[0]), ("parallel", "arbitrary")
        scratch = [pltpu.VMEM(tuple(d for d in s.block_shape if d is not None), BF16) for _, s in ws]
    else:
        sem = ("parallel", "parallel") + (("arbitrary",) if nk > 1 else ())
        scratch = [pltpu.VMEM(s, F32) for s in acc_shapes] if nk > 1 else []
    kern = functools.partial(_mm_kernel, n_w=len(ws), n_ex=len(extras), n_out=len(outs), nk=nk, cast_w=cast_w,
                             epilogue=epilogue, a_fn=a_fn)
    return pl.pallas_call(
        kern, grid=grid, in_specs=in_specs, out_specs=out_specs, out_shape=[o for o, _ in outs],
        scratch_shapes=scratch, compiler_params=_cparams(*sem), name=name,
    )(a, *[w for w, _ in ws], *[e for e, _ in extras])


def _ep_resid(accs, ex, outs, ids):
    x_ref, gate_ref = ex
    outs[0][...] = x_ref[...] + gate_ref[0] * accs[0]


def _matmul_resid(a, w, x, gate, seq, *, name, layer=None, tm=1024, tn=512, tk=4096):
    t, kdim = a.shape
    n = w.shape[-1]
    if layer is None:
        w_spec = lambda tk, tn: pl.BlockSpec((tk, tn), lambda i, j, *k: (k[0] if k else 0, j))
    else:
        w_spec = lambda tk, tn: pl.BlockSpec((None, tk, tn), lambda i, j, *k: (layer, k[0] if k else 0, j))
    tm, tn, tk = min(tm, seq), min(tn, n), min(tk, kdim)
    nk = kdim // tk
    grid = (t // tm, n // tn) + ((nk,) if nk > 1 else ())
    rows_per_batch = seq // tm
    out = _mm(
        a, pl.BlockSpec((tm, tk), lambda i, j, *k: (i, k[0] if k else 0)),
        [(w, w_spec(tk, tn))],
        [(x, pl.BlockSpec((tm, tn), lambda i, j, *k: (i, j))),
         (gate, pl.BlockSpec((1, 1, tn), lambda i, j, *k: (i // rows_per_batch, 0, j)))],
        [(jax.ShapeDtypeStruct((t, n), F32), pl.BlockSpec((tm, tn), lambda i, j, *k: (i, j)))],
        _ep_resid, grid=grid, acc_shapes=[(tm, tn)], name=name, cast_w=w.dtype == F32)
    return out[0]


def _swiglu_act(h, w13, layer, *, tm=512, tn=512):
    t, d = h.shape
    f = w13.shape[2] // 2
    tm, tn = min(tm, t), min(tn, f)
    nf = f // tn

    def epilogue(accs, ex, outs, ids):
        outs[0][...] = (_silu(accs[0]) * accs[1]).astype(BF16)

    out = _mm(
        h, pl.BlockSpec((tm, d), lambda i, j: (i, 0)),
        [(w13, pl.BlockSpec((None, d, tn), lambda i, j: (layer, 0, j))),
         (w13, pl.BlockSpec((None, d, tn), lambda i, j: (layer, 0, j + nf)))],
        [],
        [(jax.ShapeDtypeStruct((t, f), BF16), pl.BlockSpec((tm, tn), lambda i, j: (i, j)))],
        epilogue, grid=(t // tm, nf), name="ffn_swiglu", cast_w=True)
    return out[0]


MOE_ROW_TILE = 512
MOE_DMA_TILE = 256
SEG_PAD = SUBLANES


class _SegLayout:
    def __init__(self, d):
        self.n_seg = d // LANES
        self.slab = -(-self.n_seg // SUBLANES) * SUBLANES
        self.pitch = self.slab + SEG_PAD

    def seg(self, s, n_rows):
        return pl.ds(s, n_rows, stride=self.pitch)

    def slab_of(self, row):
        return pl.ds(pl.multiple_of(row * self.pitch, SUBLANES), self.slab)

    def store(self, ref, value):
        n_rows = value.shape[0]
        for s in range(self.n_seg):
            ref[self.seg(s, n_rows), :] = value[:, s * LANES:(s + 1) * LANES]
        for s in range(self.n_seg, self.pitch):
            ref[self.seg(s, n_rows), :] = jnp.zeros((n_rows, LANES), ref.dtype)

    def row_copy(self, src, src_row, dst, dst_row, sem):
        return pltpu.make_async_copy(src.at[self.slab_of(src_row)], dst.at[self.slab_of(dst_row)], sem)


def _dispatch_kernel(pos_ref, h_ref, zero_hbm, out_hbm, sem, *, n_tok, tile, lay):
    del zero_hbm
    base = pl.program_id(0) * tile

    def copies(i):
        return [lay.row_copy(h_ref, i, out_hbm, pos_ref[k * n_tok + base + i], sem) for k in range(MOE_TOPK)]

    def start(i, carry):
        for cp in copies(i):
            cp.start()
        return carry

    def wait(i, carry):
        for cp in copies(i):
            cp.wait()
        return carry

    lax.fori_loop(0, tile, start, 0)
    lax.fori_loop(0, tile, wait, 0)


def _moe_dispatch(h_seg, pos, n_rows, lay):
    t = h_seg.shape[0] // lay.pitch
    tile = min(MOE_DMA_TILE, t)
    any_spec = pl.BlockSpec(memory_space=pl.ANY)
    return pl.pallas_call(
        functools.partial(_dispatch_kernel, n_tok=t, tile=tile, lay=lay),
        grid_spec=pltpu.PrefetchScalarGridSpec(
            num_scalar_prefetch=1, grid=(t // tile,),
            in_specs=[pl.BlockSpec((tile * lay.pitch, LANES), lambda i, p: (i, 0)), any_spec], out_specs=any_spec,
            scratch_shapes=[pltpu.SemaphoreType.DMA(())]),
        out_shape=jax.ShapeDtypeStruct((n_rows * lay.pitch, LANES), h_seg.dtype),
        input_output_aliases={2: 0},
        compiler_params=_cparams("arbitrary"), name="moe_dispatch")(
            pos, h_seg, jnp.zeros((n_rows * lay.pitch, LANES), h_seg.dtype))


def _grouped_swiglu_kernel(te_ref, nu_ref, a_ref, wg_ref, wu_ref, out_ref, a_scr, *, tm, lay):
    active = pl.program_id(1) < nu_ref[0]

    @pl.when(active)
    def _():
        for s in range(lay.n_seg):
            a_scr[:, s * LANES:(s + 1) * LANES] = a_ref[lay.seg(s, tm), :].astype(BF16)
        a = a_scr[...]
        g = jnp.dot(a, wg_ref[...], preferred_element_type=F32)
        u = jnp.dot(a, wu_ref[...], preferred_element_type=F32)
        out_ref[...] = (_silu(g) * u).astype(BF16)

    @pl.when(jnp.logical_not(active))
    def _():
        out_ref[...] = jnp.zeros_like(out_ref)


def _grouped_swiglu(a_seg, w13, layer, tile_expert, n_used, lay, *, tn=512):
    rows, kdim = a_seg.shape[0] // lay.pitch, w13.shape[2]
    f = w13.shape[3] // 2
    tm, tn = MOE_ROW_TILE, min(tn, f)
    nb = f // tn
    return pl.pallas_call(
        functools.partial(_grouped_swiglu_kernel, tm=tm, lay=lay),
        grid_spec=pltpu.PrefetchScalarGridSpec(
            num_scalar_prefetch=2, grid=(nb, rows // tm),
            in_specs=[pl.BlockSpec((tm * lay.pitch, LANES), lambda j, r, te, nu: (r, 0)),
                      pl.BlockSpec((None, None, kdim, tn), lambda j, r, te, nu: (layer, te[r], 0, j)),
                      pl.BlockSpec((None, None, kdim, tn), lambda j, r, te, nu: (layer, te[r], 0, j + nb))],
            out_specs=pl.BlockSpec((tm, tn), lambda j, r, te, nu: (r, j)),
            scratch_shapes=[pltpu.VMEM((tm, kdim), BF16)]),
        out_shape=jax.ShapeDtypeStruct((rows, f), BF16),
        compiler_params=_cparams("parallel", "parallel"), name="moe_swiglu")(
            tile_expert, n_used, a_seg, w13, w13)


def _grouped_down_kernel(te_ref, nu_ref, a_ref, w_ref, out_ref, acc_ref, *, nk, lay):
    active = pl.program_id(0) < nu_ref[0]
    k = pl.program_id(1)

    @pl.when(active & (k == 0))
    def _():
        acc_ref[...] = jnp.zeros_like(acc_ref)

    @pl.when(active)
    def _():
        acc_ref[...] += jnp.dot(a_ref[...], w_ref[...], preferred_element_type=F32)

    @pl.when(active & (k == nk - 1))
    def _():
        lay.store(out_ref, acc_ref[...])

    @pl.when(jnp.logical_not(active) & (k == nk - 1))
    def _():
        out_ref[...] = jnp.zeros_like(out_ref)


def _grouped_down(a, w2, layer, tile_expert, n_used, lay, *, tk=1024):
    rows, kdim = a.shape
    d = w2.shape[3]
    tm, tk = MOE_ROW_TILE, min(tk, kdim)
    nk = kdim // tk
    return pl.pallas_call(
        functools.partial(_grouped_down_kernel, nk=nk, lay=lay),
        grid_spec=pltpu.PrefetchScalarGridSpec(
            num_scalar_prefetch=2, grid=(rows // tm, nk),
            in_specs=[pl.BlockSpec((tm, tk), lambda r, k, te, nu: (r, k)),
                      pl.BlockSpec((None, None, tk, d), lambda r, k, te, nu: (layer, te[r], k, 0))],
            out_specs=pl.BlockSpec((tm * lay.pitch, LANES), lambda r, k, te, nu: (r, 0)),
            scratch_shapes=[pltpu.VMEM((tm, d), F32)]),
        out_shape=jax.ShapeDtypeStruct((rows * lay.pitch, LANES), F32),
        compiler_params=_cparams("parallel", "arbitrary"), name="moe_down")(tile_expert, n_used, a, w2)


def _combine_kernel(pos_ref, y_hbm, x_ref, route_ref, gate_ref, out_ref, buf0, buf1, sem, *, n_tok, tile, lay):
    base = pl.program_id(0) * tile
    bufs = (buf0, buf1)

    def copies(i):
        return [lay.row_copy(y_hbm, pos_ref[k * n_tok + base + i], bufs[k], i, sem) for k in range(MOE_TOPK)]

    def start(i, carry):
        for cp in copies(i):
            cp.start()
        return carry

    def wait(i, carry):
        for cp in copies(i):
            cp.wait()
        return carry

    lax.fori_loop(0, tile, start, 0)
    lax.fori_loop(0, tile, wait, 0)
    route = route_ref[...]
    w0 = route[:, ROUTE_WEIGHT:ROUTE_WEIGHT + 1]
    w1 = route[:, ROUTE_WEIGHT + 1:ROUTE_WEIGHT + 2]
    for s in range(lay.n_seg):
        cols = slice(s * LANES, (s + 1) * LANES)
        y = w0 * buf0[lay.seg(s, tile), :] + w1 * buf1[lay.seg(s, tile), :]
        out_ref[:, cols] = x_ref[:, cols] + gate_ref[0][:, cols] * y


def _moe_combine(y_seg, pos, x, route, gate, seq, lay):
    t, d = x.shape
    tile = min(MOE_DMA_TILE, seq)
    rpb = seq // tile
    row = pl.BlockSpec((tile, d), lambda i, p: (i, 0))
    buf = pltpu.VMEM((tile * lay.pitch, LANES), F32)
    return pl.pallas_call(
        functools.partial(_combine_kernel, n_tok=t, tile=tile, lay=lay),
        grid_spec=pltpu.PrefetchScalarGridSpec(
            num_scalar_prefetch=1, grid=(t // tile,),
            in_specs=[pl.BlockSpec(memory_space=pl.ANY), row,
                      pl.BlockSpec((tile, LANES), lambda i, p: (i, 0)),
                      pl.BlockSpec((1, 1, d), lambda i, p: (i // rpb, 0, 0))],
            out_specs=row,
            scratch_shapes=[buf, buf, pltpu.SemaphoreType.DMA(())]),
        out_shape=jax.ShapeDtypeStruct((t, d), F32),
        compiler_params=_cparams("arbitrary"), name="moe_combine")(pos, y_seg, x, route, gate)


def _ada_mod(c, ada_w, ada_b, ada_table):
    bsz, d = c.shape
    b = 16
    c = jnp.pad(c, ((0, b - bsz), (0, 0)))
    depth = ada_table.shape[0]
    n = ada_w.shape[1]
    tn = min(512, n)

    def epilogue(accs, ex, outs, ids):
        bias_ref, tab_ref = ex
        outs[0][...] = (accs[0] + bias_ref[...])[None] + tab_ref[...]

    out = _mm(
        c, pl.BlockSpec((b, d), lambda i, j: (0, 0)),
        [(ada_w, pl.BlockSpec((d, tn), lambda i, j: (0, j)))],
        [(ada_b.reshape(1, n), pl.BlockSpec((1, tn), lambda i, j: (0, j))),
         (ada_table.reshape(depth, 1, n), pl.BlockSpec((depth, 1, tn), lambda i, j: (0, 0, j)))],
        [(jax.ShapeDtypeStruct((depth, b, n), F32), pl.BlockSpec((depth, b, tn), lambda i, j: (0, 0, j)))],
        epilogue, grid=(1, n // tn), name="ada_mod", cast_w=True, a_fn=lambda v: _silu(v).astype(BF16))
    return out[0][:, :bsz].reshape(depth, bsz, N_ADA, d)


def _norm_kernel(*refs, modulate, n_experts):
    if not modulate:
        x_ref, g_ref, out_ref = refs
        out_ref[...] = _rms(x_ref[...], g_ref[...])
        return
    x_ref, g_ref, sh_ref, sc_ref = refs[:4]
    h = _rms(x_ref[...], g_ref[...]) * (1.0 + sc_ref[0]) + sh_ref[0]
    if not n_experts:
        refs[4][...] = h.astype(BF16)
        return
    r_ref, hseg_ref, route_ref, count_ref, carry_ref = refs[4:]

    @pl.when(pl.program_id(0) == 0)
    def _():
        carry_ref[...] = jnp.zeros_like(carry_ref)

    _SegLayout(h.shape[1]).store(hseg_ref, h)
    logits = jnp.dot(h, r_ref[...], preferred_element_type=F32, precision=lax.Precision.HIGHEST)
    lane = lax.broadcasted_iota(jnp.int32, logits.shape, 1)
    lg = jnp.where(lane < n_experts, logits, -jnp.inf)
    m1 = jnp.max(lg, axis=1, keepdims=True)
    i1 = jnp.min(jnp.where(lg == m1, lane, LANES), axis=1, keepdims=True)
    lg2 = jnp.where(lane == i1, -jnp.inf, lg)
    m2 = jnp.max(lg2, axis=1, keepdims=True)
    i2 = jnp.min(jnp.where(lg2 == m2, lane, LANES), axis=1, keepdims=True)
    e2 = jnp.exp(m2 - m1)
    w1 = 1.0 / (1.0 + e2)
    sel = jnp.where((lane == i1) | (lane == i2), 1.0, 0.0)
    tm = sel.shape[0]
    lower = lax.broadcasted_iota(jnp.int32, (tm, tm), 0) >= lax.broadcasted_iota(jnp.int32, (tm, tm), 1)
    cum = jnp.dot(jnp.where(lower, 1.0, 0.0).astype(BF16), sel.astype(BF16), preferred_element_type=F32)
    rank = cum - sel + carry_ref[...]
    r1 = jnp.sum(jnp.where(lane == i1, rank, 0.0), axis=1, keepdims=True)
    r2 = jnp.sum(jnp.where(lane == i2, rank, 0.0), axis=1, keepdims=True)
    carry_ref[...] += jnp.sum(sel, axis=0, keepdims=True)
    count_ref[...] = carry_ref[...]
    cols = (i1.astype(F32), i2.astype(F32), r1, r2, w1, e2 * w1)
    table = jnp.zeros_like(logits)
    for k, col in enumerate(cols):
        table = jnp.where(lane == k, col, table)
    route_ref[...] = table


ROUTE_EXPERT, ROUTE_RANK, ROUTE_WEIGHT = 0, 2, 4


def _norm_mod(x, g, shift, scale, seq, router=None, *, tm=256):
    t, d = x.shape
    tm = min(tm, seq)
    rpb = seq // tm
    row = pl.BlockSpec((tm, d), lambda i: (i, 0))
    vec = pl.BlockSpec((1, d), lambda i: (0, 0))
    per_batch = pl.BlockSpec((1, 1, d), lambda i: (i // rpb, 0, 0))
    ins = [x, g.reshape(1, d), shift, scale]
    in_specs = [row, vec, per_batch, per_batch]
    out_shape = [jax.ShapeDtypeStruct((t, d), BF16)]
    out_specs = [row]
    n_experts = 0
    scratch = []
    if router is not None:
        n_experts = router.shape[1]
        pitch = _SegLayout(d).pitch
        out_shape = [jax.ShapeDtypeStruct((t * pitch, LANES), F32)]
        out_specs = [pl.BlockSpec((tm * pitch, LANES), lambda i: (i, 0))]
        ins.append(jnp.pad(router, ((0, 0), (0, LANES - n_experts))))
        in_specs.append(pl.BlockSpec((d, LANES), lambda i: (0, 0)))
        out_shape += [jax.ShapeDtypeStruct((t, LANES), F32), jax.ShapeDtypeStruct((1, LANES), F32)]
        out_specs += [pl.BlockSpec((tm, LANES), lambda i: (i, 0)), pl.BlockSpec((1, LANES), lambda i: (0, 0))]
        scratch = [pltpu.VMEM((1, LANES), F32)]
    res = pl.pallas_call(
        functools.partial(_norm_kernel, modulate=True, n_experts=n_experts),
        grid=(t // tm,), in_specs=in_specs, out_specs=out_specs, out_shape=out_shape, scratch_shapes=scratch,
        compiler_params=_cparams("arbitrary" if router is not None else "parallel"),
        name="norm_router" if router is not None else "norm_mod")(*ins)
    return res if router is not None else res[0]


def _final_norm(x, g, *, tm=256):
    t, d = x.shape
    tm = min(tm, t)
    row = pl.BlockSpec((tm, d), lambda i: (i, 0))
    return pl.pallas_call(
        functools.partial(_norm_kernel, modulate=False, n_experts=0),
        grid=(t // tm,), in_specs=[row, pl.BlockSpec((1, d), lambda i: (0, 0))], out_specs=row,
        out_shape=jax.ShapeDtypeStruct((t, d), F32), compiler_params=_cparams("parallel"),
        name="final_norm")(x, g.reshape(1, d))


def _conv_kernel(*refs, width, ts, tc, has_bias, has_mul):
    cur_ref, halo_ref, w_ref = refs[:3]
    rest = list(refs[3:])
    b_ref = rest.pop(0) if has_bias else None
    mul_ref = rest.pop(0) if has_mul else None
    out_ref, buf = rest
    first = pl.program_id(1) == 0
    buf[0:CONV_HALO, :] = jnp.where(first, 0.0, halo_ref[0])
    buf[CONV_HALO:CONV_HALO + ts, :] = cur_ref[0]
    rows = 32
    for r in range(0, ts, rows):
        acc = None
        for k in range(width):
            term = w_ref[k:k + 1, :] * buf[pl.ds(CONV_HALO + r - (width - 1) + k, rows), :]
            acc = term if acc is None else acc + term
        if has_bias:
            acc = acc + b_ref[...]
        if has_mul:
            acc = acc * mul_ref[0, r:r + rows, :].astype(F32)
        out_ref[0, r:r + rows, :] = acc.astype(out_ref.dtype)


def _causal_conv(x, conv_w, bias=None, mul=None, out_dtype=F32, *, ts=256, tc=512):
    b, s, ch = x.shape
    width = conv_w.shape[0]
    assert width - 1 <= CONV_HALO
    if width <= 4:
        tc *= 2
    ts, tc = min(ts, s), min(tc, ch)
    hpb = ts // CONV_HALO
    blk = pl.BlockSpec((1, ts, tc), lambda bi, si, ci: (bi, si, ci))
    ins = [x, x, conv_w]
    in_specs = [blk,
                pl.BlockSpec((1, CONV_HALO, tc), lambda bi, si, ci: (bi, jnp.maximum(si * hpb - 1, 0), ci)),
                pl.BlockSpec((width, tc), lambda bi, si, ci: (0, ci))]
    if bias is not None:
        ins.append(bias.reshape(1, ch))
        in_specs.append(pl.BlockSpec((1, tc), lambda bi, si, ci: (0, ci)))
    if mul is not None:
        ins.append(mul)
        in_specs.append(blk)
    return pl.pallas_call(
        functools.partial(_conv_kernel, width=width, ts=ts, tc=tc, has_bias=bias is not None,
                          has_mul=mul is not None),
        grid=(b, s // ts, ch // tc), in_specs=in_specs, out_specs=blk,
        out_shape=jax.ShapeDtypeStruct((b, s, ch), out_dtype),
        scratch_shapes=[pltpu.VMEM((CONV_HALO + ts, tc), F32)],
        compiler_params=_cparams("parallel", "parallel", "parallel"), name=f"causal_conv{width}")(*ins)


def _ln_silu_kernel(x_ref, g_ref, b_ref, out_ref):
    x = x_ref[...]
    mu = jnp.mean(x, axis=-1, keepdims=True)
    xc = x - mu
    var = jnp.mean(xc * xc, axis=-1, keepdims=True)
    y = xc * lax.rsqrt(var + NORM_EPS) * g_ref[...] + b_ref[...]
    out_ref[...] = _silu(y).astype(BF16)


def _ln_silu(x, g, b, *, tm=256):
    t, d = x.shape
    tm = min(tm, t)
    row = pl.BlockSpec((tm, d), lambda i: (i, 0))
    vec = pl.BlockSpec((1, d), lambda i: (0, 0))
    return pl.pallas_call(
        _ln_silu_kernel, grid=(t // tm,), in_specs=[row, vec, vec], out_specs=row,
        out_shape=jax.ShapeDtypeStruct((t, d), BF16), compiler_params=_cparams("parallel"), name="ln_silu")(
            x, g.reshape(1, d), b.reshape(1, d))


def _rglru_kernel(xc_ref, gate_ref, wa_ref, wx_ref, ba_ref, bx_ref, lam_ref, out_ref,
                  a_scr, u_scr, h_scr, *, ts, n_heads, blk, cw):
    @pl.when(pl.program_id(1) == 0)
    def _():
        h_scr[...] = jnp.zeros_like(h_scr)

    neg_lam = -lam_ref[...]
    softplus = jnp.maximum(neg_lam, 0.0) + jnp.log1p(jnp.exp(-jnp.abs(neg_lam)))
    for hd in range(n_heads):
        sl = slice(hd * blk, (hd + 1) * blk)
        xh = xc_ref[:, sl]
        xb = xh.astype(BF16)
        r = _sigmoid(jnp.dot(xb, wa_ref[hd], preferred_element_type=F32) + ba_ref[:, sl])
        i_g = _sigmoid(jnp.dot(xb, wx_ref[hd], preferred_element_type=F32) + bx_ref[:, sl])
        log_a = -LRU_C * r * softplus[:, sl]
        a_scr[:, sl] = jnp.exp(log_a)
        u_scr[:, sl] = jnp.sqrt(1.0 - jnp.exp(2.0 * log_a)) * (i_g * xh)

    row = lax.broadcasted_iota(jnp.int32, (SUBLANES, cw), 0)
    width = n_heads * blk
    for c0 in range(0, width, cw):
        def body(g, h_prev, c0=c0):
            r0 = pl.multiple_of(g * SUBLANES, SUBLANES)
            a8 = a_scr[pl.ds(r0, SUBLANES), c0:c0 + cw]
            b8 = u_scr[pl.ds(r0, SUBLANES), c0:c0 + cw]
            for d in (1, 2, 4):
                keep = row >= d
                b8 = jnp.where(keep, a8 * pltpu.roll(b8, d, axis=0) + b8, b8)
                a8 = jnp.where(keep, a8 * pltpu.roll(a8, d, axis=0), a8)
            h8 = b8 + a8 * h_prev
            u_scr[pl.ds(r0, SUBLANES), c0:c0 + cw] = h8
            return h8[SUBLANES - 1:SUBLANES, :]

        h_scr[:, c0:c0 + cw] = lax.fori_loop(0, ts // SUBLANES, body, h_scr[:, c0:c0 + cw])
    out_ref[...] = (u_scr[...] * gate_ref[...].astype(F32)).astype(BF16)


def _rglru(xc, gate, w_a, w_x, b_a, b_x, lam, seq, *, ts=256):
    t, width = xc.shape
    n_heads, blk, _ = w_a.shape
    ts = min(ts, seq)
    nts = seq // ts
    row = pl.BlockSpec((ts, width), lambda b, j: (b * nts + j, 0))
    wsp = pl.BlockSpec((n_heads, blk, blk), lambda b, j: (0, 0, 0))
    vec = pl.BlockSpec((1, width), lambda b, j: (0, 0))
    return pl.pallas_call(
        functools.partial(_rglru_kernel, ts=ts, n_heads=n_heads, blk=blk, cw=min(1024, width)),
        grid=(t // seq, nts), in_specs=[row, row, wsp, wsp, vec, vec, vec], out_specs=row,
        out_shape=jax.ShapeDtypeStruct((t, width), BF16),
        scratch_shapes=[pltpu.VMEM((ts, width), F32), pltpu.VMEM((ts, width), F32), pltpu.VMEM((1, width), F32)],
        compiler_params=_cparams("parallel", "arbitrary"), name="rglru")(
            xc, gate, w_a, w_x, b_a.reshape(1, width), b_x.reshape(1, width), lam.reshape(1, width))


def _bias_tile_kernel(rb_ref, out_ref):
    hd = pl.program_id(0)
    shape = (2 * Q_BLOCK, Q_BLOCK)
    dist = Q_BLOCK + lax.broadcasted_iota(jnp.int32, shape, 1) - lax.broadcasted_iota(jnp.int32, shape, 0)
    dist = jnp.maximum(dist, 0)
    max_exact = REL_BUCKETS // 2
    large = max_exact + (jnp.log(jnp.maximum(dist, 1).astype(F32) / max_exact)
                         / math.log(REL_MAX_DIST / max_exact) * (REL_BUCKETS - max_exact)).astype(jnp.int32)
    bucket = jnp.where(dist < max_exact, dist, jnp.minimum(large, REL_BUCKETS - 1))
    tile = jnp.zeros(shape, F32)
    for bkt in range(REL_BUCKETS):
        tile = jnp.where(bucket == bkt, rb_ref[bkt, hd], tile)
    out_ref[0] = tile - rb_ref[REL_BUCKETS - 1, hd]


def _bias_tile(rel_bias):
    n_heads = rel_bias.shape[1]
    return pl.pallas_call(
        _bias_tile_kernel, grid=(n_heads,),
        in_specs=[pl.BlockSpec(memory_space=pltpu.SMEM)],
        out_specs=pl.BlockSpec((1, 2 * Q_BLOCK, Q_BLOCK), lambda h: (h, 0, 0)),
        out_shape=jax.ShapeDtypeStruct((n_heads, 2 * Q_BLOCK, Q_BLOCK), F32),
        compiler_params=_cparams("parallel"), name="dsa_bias_tile")(rel_bias)


def _dsa_kernel(qidx_ref, wrow_ref, kidx_ref, ckv_ref, ckvt_ref, qlat_ref, bias_ref, wuv_ref, out_ref,
                score_ref, key_ref, madd_ref, z_ref, acc_ref, *, n_slab, k_sel, idx_heads, kv_dim, head_dim):
    qb = pl.program_id(1)
    cols = HEAD_GROUP * Q_BLOCK
    sub = KEY_CHUNK // SUBLANES
    nt = (((1,), (1,)), ((), ()))
    n_chunks = (qb + 2) // 2

    @pl.when(pl.program_id(2) == 0)
    def _select():
        def chunk_body(c, carry):
            k0 = pl.multiple_of(c * KEY_CHUNK, KEY_CHUNK)
            keys = kidx_ref[0, pl.ds(k0, KEY_CHUNK), :]

            def head_body(hg, acc):
                h0 = pl.multiple_of(hg * HEAD_GROUP, HEAD_GROUP)
                q = qidx_ref[0, pl.ds(h0, HEAD_GROUP)].reshape(cols, LANES)
                dots = lax.dot_general(keys, q, nt, preferred_element_type=F32)
                weighted = jnp.maximum(dots, 0.0) * wrow_ref[0, hg]
                for hh in range(HEAD_GROUP):
                    acc = acc + weighted[:, hh * Q_BLOCK:(hh + 1) * Q_BLOCK]
                return acc

            score_ref[pl.ds(k0, KEY_CHUNK), :] = lax.fori_loop(
                0, idx_heads // HEAD_GROUP, head_body, jnp.zeros((KEY_CHUNK, Q_BLOCK), F32), unroll=2)
            return carry

        lax.fori_loop(0, n_chunks, chunk_body, 0)

        def zero_body(c, carry):
            score_ref[pl.ds(pl.multiple_of(c * KEY_CHUNK, KEY_CHUNK), KEY_CHUNK), :] = jnp.zeros(
                (KEY_CHUNK, Q_BLOCK), F32)
            return carry

        lax.fori_loop(n_chunks, n_slab // 2, zero_body, 0)

        shape = (n_slab * LANES, Q_BLOCK)
        bits = pltpu.bitcast(score_ref[...], jnp.int32)
        key = jnp.where(bits >= 0, bits, bits ^ jnp.int32(0x7FFFFFFF))
        causal = lax.broadcasted_iota(jnp.int32, shape, 0) <= qb * Q_BLOCK + lax.broadcasted_iota(jnp.int32, shape, 1)
        key = jnp.where(causal, key, INT_MIN)
        key_ref[...] = key

        def bit_body(it, thr):
            cand = thr + lax.shift_left(jnp.int32(1), 31 - it)
            cand_b = jnp.broadcast_to(cand, (SUBLANES, Q_BLOCK))[None, None]
            lanes_of_sum = (SUBLANES, SUBLANES, Q_BLOCK)

            def count_body(c, cnt):
                kc = key_ref[pl.ds(pl.multiple_of(c * 2 * KEY_CHUNK, 2 * KEY_CHUNK), 2 * KEY_CHUNK), :]
                hit = jnp.where(kc.reshape((2 * sub // SUBLANES,) + lanes_of_sum) >= cand_b, 1.0, 0.0)
                return cnt + jnp.sum(hit, axis=0)

            cnt = lax.fori_loop(0, (n_chunks + 1) // 2, count_body, jnp.zeros(lanes_of_sum, F32))
            cnt = jnp.sum(jnp.sum(cnt, axis=0), axis=0, keepdims=True)
            return jnp.where(cnt >= k_sel, cand, thr)

        thr = lax.fori_loop(0, 32, bit_body, jnp.full((1, Q_BLOCK), INT_MIN, jnp.int32))
        madd_ref[0:Q_BLOCK, :] = jnp.full((Q_BLOCK, Q_BLOCK), MASK_NEG, F32)
        madd_ref[Q_BLOCK:, :] = jnp.where(causal & (key >= thr), 0.0, MASK_NEG)

    q = qlat_ref[0].reshape(cols, kv_dim)
    n_far = qb // 2
    near_row = qb * Q_BLOCK

    def logits_of(kv):
        return lax.dot_general(kv, q, nt, preferred_element_type=F32)

    def per_head(m):
        return jnp.concatenate([m] * HEAD_GROUP, axis=1)

    def col_max(z):
        return jnp.max(z.reshape(sub, SUBLANES, cols), axis=0)

    def far_logits(pair, m8):
        for c in (2 * pair, 2 * pair + 1):
            r0 = pl.multiple_of(Q_BLOCK + c * KEY_CHUNK, Q_BLOCK)
            kv = ckv_ref[0, pl.ds(r0, KEY_CHUNK), :]
            row = r0 + lax.broadcasted_iota(jnp.int32, (KEY_CHUNK, Q_BLOCK), 0)
            madd = jnp.where(row < near_row, madd_ref[pl.ds(r0, KEY_CHUNK), :], MASK_NEG)
            z = logits_of(kv) + per_head(madd)
            z_ref[c] = z
            m8 = jnp.maximum(m8, col_max(z))
        return m8

    m8 = lax.fori_loop(0, (n_far + 1) // 2, far_logits, jnp.full((SUBLANES, cols), MASK_NEG, F32))
    r0 = pl.multiple_of(near_row, Q_BLOCK)
    bias = jnp.concatenate([bias_ref[hh] for hh in range(HEAD_GROUP)], axis=1)
    z = (logits_of(ckv_ref[0, pl.ds(r0, KEY_CHUNK), :]) + per_head(madd_ref[pl.ds(r0, KEY_CHUNK), :])) + bias
    z_ref[n_far] = z
    z_ref[n_far + 1] = jnp.full((KEY_CHUNK, cols), MASK_NEG, F32)
    m8 = jnp.maximum(m8, col_max(z))
    m8 = jnp.broadcast_to(jnp.max(m8, axis=0, keepdims=True), (SUBLANES, cols))[None]
    acc_ref[...] = jnp.zeros(acc_ref.shape, F32)

    def weigh(pair, l8):
        for c in (2 * pair, 2 * pair + 1):
            s0 = jnp.where(c < n_far, 2 * c + 1, qb)
            kvt = jnp.concatenate([ckvt_ref[0, s0], ckvt_ref[0, s0 + 1]], axis=1)
            p = jnp.exp(z_ref[c].reshape(sub, SUBLANES, cols) - m8)
            acc_ref[...] += jnp.dot(kvt, p.reshape(KEY_CHUNK, cols).astype(BF16), preferred_element_type=F32)
            l8 = l8 + jnp.sum(p, axis=0)
        return l8

    l8 = lax.fori_loop(0, (n_far + 2) // 2, weigh, jnp.zeros((SUBLANES, cols), F32))
    inv = 1.0 / jnp.broadcast_to(jnp.sum(l8, axis=0, keepdims=True), (SUBLANES, cols))
    o_lat = (acc_ref[...].reshape(kv_dim // SUBLANES, SUBLANES, cols) * inv[None]).reshape(kv_dim, cols)
    o_lat = o_lat.astype(BF16)
    for hh in range(HEAD_GROUP):
        o_h = jnp.dot(wuv_ref[hh], o_lat[:, hh * Q_BLOCK:(hh + 1) * Q_BLOCK], preferred_element_type=F32)
        out_ref[:, hh * head_dim:(hh + 1) * head_dim] = o_h.T.astype(BF16)


def _dsa_core(qidx, wrow, kidx, ckv_pad, ckvt_pad, qlat, bias, w_uvt, *, bsz, seq, k_sel):
    n_qb = seq // Q_BLOCK
    idx_heads = qidx.shape[1]
    n_heads, kv_dim = qlat.shape[1], qlat.shape[3]
    head_dim = w_uvt.shape[1]
    n_groups = n_heads // HEAD_GROUP
    cols = HEAD_GROUP * Q_BLOCK
    blk_q = lambda b, i, g: (b * n_qb + i, 0, 0, 0)
    per_batch = lambda b, i, g: (b, 0, 0)
    return pl.pallas_call(
        functools.partial(_dsa_kernel, n_slab=n_qb, k_sel=k_sel, idx_heads=idx_heads, kv_dim=kv_dim,
                          head_dim=head_dim),
        grid=(bsz, n_qb, n_groups),
        in_specs=[
            pl.BlockSpec((1, idx_heads, Q_BLOCK, LANES), blk_q),
            pl.BlockSpec((1, idx_heads // HEAD_GROUP, 1, cols), blk_q),
            pl.BlockSpec((1, seq, LANES), per_batch),
            pl.BlockSpec((1, seq + Q_BLOCK, kv_dim), per_batch),
            pl.BlockSpec((1, n_qb + 1, kv_dim, Q_BLOCK), lambda b, i, g: (b, 0, 0, 0)),
            pl.BlockSpec((1, HEAD_GROUP, Q_BLOCK, kv_dim), lambda b, i, g: (b * n_qb + i, g, 0, 0)),
            pl.BlockSpec((HEAD_GROUP, KEY_CHUNK, Q_BLOCK), lambda b, i, g: (g, 0, 0)),
            pl.BlockSpec((HEAD_GROUP, head_dim, kv_dim), lambda b, i, g: (g, 0, 0)),
        ],
        out_specs=pl.BlockSpec((Q_BLOCK, HEAD_GROUP * head_dim), lambda b, i, g: (b * n_qb + i, g)),
        out_shape=jax.ShapeDtypeStruct((bsz * seq, n_heads * head_dim), BF16),
        scratch_shapes=[
            pltpu.VMEM((seq, Q_BLOCK), F32),
            pltpu.VMEM((seq, Q_BLOCK), jnp.int32),
            pltpu.VMEM((seq + Q_BLOCK, Q_BLOCK), F32),
            pltpu.VMEM((n_qb // 2 + 1, KEY_CHUNK, cols), F32),
            pltpu.VMEM((kv_dim, cols), F32),
        ],
        compiler_params=_cparams("parallel", "parallel", "arbitrary"), name="dsa_core")(
            qidx, wrow, kidx, ckv_pad, ckvt_pad, qlat, bias, w_uvt)


def _dsa_attention(h, x, gate, rel_bias, w_in, g_cq, g_ckv, w_uq, w_uk, w_uv, w_qidx, w_out, *, bsz, seq):
    t, d = h.shape
    q_lora, kv_lora = g_cq.shape[0], g_ckv.shape[0]
    n_heads, head_dim = w_uk.shape[1], w_uk.shape[2]
    idx_dim = LANES
    idx_heads = w_qidx.shape[1] // idx_dim
    k_sel = min(TOPK_MAX, seq // 4)
    n_qb = seq // Q_BLOCK
    tm = min(512, seq)
    qpt = tm // Q_BLOCK

    w_in = w_in.astype(BF16)
    splits = (0, q_lora, q_lora + kv_lora, q_lora + kv_lora + idx_dim, w_in.shape[1])
    w_parts = [w_in[:, splits[i]:splits[i + 1]] for i in range(4)]
    head_scale = idx_heads ** -0.5 * idx_dim ** -0.5

    def in_epilogue(accs, ex, outs, ids):
        outs[0][...] = _rms(accs[0], ex[0][...]).astype(BF16)
        outs[1][...] = _rms(accs[1], ex[1][...]).astype(BF16)
        outs[2][...] = accs[2].astype(BF16)
        outs[3][...] = accs[3] * head_scale

    full = lambda n: pl.BlockSpec((d, n), lambda i, j: (0, 0))
    rowsp = lambda n: pl.BlockSpec((tm, n), lambda i, j: (i, 0))
    vecsp = lambda n: pl.BlockSpec((1, n), lambda i, j: (0, 0))
    widths = [w.shape[1] for w in w_parts]
    c_q, c_kv, k_idx, w_head = _mm(
        h, pl.BlockSpec((tm, d), lambda i, j: (i, 0)),
        [(w, full(n)) for w, n in zip(w_parts, widths)],
        [(g_cq.reshape(1, q_lora), vecsp(q_lora)), (g_ckv.reshape(1, kv_lora), vecsp(kv_lora))],
        [(jax.ShapeDtypeStruct((t, n), dt), rowsp(n)) for n, dt in zip(widths, (BF16, BF16, BF16, F32))],
        in_epilogue, grid=(t // tm, 1), name="dsa_in_proj")

    hpt = 4
    tn = hpt * idx_dim
    tm_q = min(2048, seq)
    qpt_q = tm_q // Q_BLOCK

    def qidx_epilogue(accs, ex, outs, ids):
        for hh in range(hpt):
            q_h = accs[0][:, hh * idx_dim:(hh + 1) * idx_dim]
            outs[0][:, hh] = q_h.reshape(qpt_q, Q_BLOCK, idx_dim).astype(BF16)

    qidx = _mm(
        c_q, pl.BlockSpec((tm_q, q_lora), lambda i, j: (i, 0)),
        [(w_qidx, pl.BlockSpec((q_lora, tn), lambda i, j: (0, j)))], [],
        [(jax.ShapeDtypeStruct((t // Q_BLOCK, idx_heads, Q_BLOCK, idx_dim), BF16),
          pl.BlockSpec((qpt_q, hpt, Q_BLOCK, idx_dim), lambda i, j: (i, j, 0, 0)))],
        qidx_epilogue, grid=(t // tm_q, idx_heads // hpt), name="dsa_qidx", cast_w=True)[0]

    w_ukt = jnp.transpose(w_uk, (1, 2, 0)).astype(BF16)
    logit_scale = head_dim ** -0.5

    def qlat_kernel(cq_ref, wuq_ref, wuk_ref, out_ref):
        qg = jnp.dot(cq_ref[...], wuq_ref[...], preferred_element_type=F32).astype(BF16)
        for hh in range(HEAD_GROUP):
            ql = jnp.dot(qg[:, hh * head_dim:(hh + 1) * head_dim], wuk_ref[hh], preferred_element_type=F32)
            out_ref[:, hh] = (ql * logit_scale).reshape(qpt, Q_BLOCK, kv_lora).astype(BF16)

    qlat = pl.pallas_call(
        qlat_kernel, grid=(t // tm, n_heads // HEAD_GROUP),
        in_specs=[pl.BlockSpec((tm, q_lora), lambda i, g: (i, 0)),
                  pl.BlockSpec((q_lora, HEAD_GROUP * head_dim), lambda i, g: (0, g)),
                  pl.BlockSpec((HEAD_GROUP, head_dim, kv_lora), lambda i, g: (g, 0, 0))],
        out_specs=pl.BlockSpec((qpt, HEAD_GROUP, Q_BLOCK, kv_lora), lambda i, g: (i, g, 0, 0)),
        out_shape=jax.ShapeDtypeStruct((t // Q_BLOCK, n_heads, Q_BLOCK, kv_lora), BF16),
        compiler_params=_cparams("parallel", "parallel"), name="dsa_qlat")(c_q, w_uq.astype(BF16), w_ukt)

    wrow = w_head.reshape(bsz * n_qb, Q_BLOCK, idx_heads // HEAD_GROUP, HEAD_GROUP).transpose(0, 2, 3, 1)
    wrow = wrow.reshape(bsz * n_qb, idx_heads // HEAD_GROUP, 1, HEAD_GROUP * Q_BLOCK)
    ckv_pad = jnp.pad(c_kv.reshape(bsz, seq, kv_lora), ((0, 0), (Q_BLOCK, 0), (0, 0)))
    ckvt_pad = ckv_pad.reshape(bsz, n_qb + 1, Q_BLOCK, kv_lora).transpose(0, 1, 3, 2)
    w_uvt = jnp.transpose(w_uv, (1, 2, 0)).astype(BF16)
    o = _dsa_core(qidx, wrow, k_idx.reshape(bsz, seq, idx_dim), ckv_pad, ckvt_pad, qlat,
                  _bias_tile(rel_bias.astype(F32)), w_uvt, bsz=bsz, seq=seq, k_sel=k_sel)
    return _matmul_resid(o, w_out, x, gate, seq, name="dsa_out_proj")


def _split_proj(h, w, n_parts, epilogue, out_dtypes, *, name, tm=512, tn=512):
    t, d = h.shape
    n = w.shape[1] // n_parts
    tm, tn = min(tm, t), min(tn, n)
    nb = n // tn
    out_sp = pl.BlockSpec((tm, tn), lambda i, j: (i, j))
    return _mm(
        h, pl.BlockSpec((tm, d), lambda i, j: (i, 0)),
        [(w, pl.BlockSpec((d, tn), lambda i, j, p=p: (0, j + p * nb))) for p in range(n_parts)], [],
        [(jax.ShapeDtypeStruct((t, n), dt), out_sp) for dt in out_dtypes],
        epilogue, grid=(t // tm, nb), name=name, cast_w=True)


def _short_conv_mixer(h, x, gate, w_in, conv_w, w_out, *, bsz, seq):
    def epilogue(accs, ex, outs, ids):
        outs[0][...] = accs[0].astype(BF16)
        outs[1][...] = accs[1] * accs[2]

    gate_b, cx = _split_proj(h, w_in, 3, epilogue, (BF16, F32), name="sconv_in_proj", tn=256)
    ch = cx.shape[1]
    y = _causal_conv(cx.reshape(bsz, seq, ch), conv_w, mul=gate_b.reshape(bsz, seq, ch), out_dtype=BF16)
    return _matmul_resid(y.reshape(bsz * seq, ch), w_out, x, gate, seq, name="sconv_out_proj")


def _conformer_mixer(h, x, gate, w_in, conv_w, conv_b, ln_g, ln_b, w_out, *, bsz, seq):
    def epilogue(accs, ex, outs, ids):
        outs[0][...] = accs[0] * _sigmoid(accs[1])

    u = _split_proj(h, w_in, 2, epilogue, (F32,), name="conf_in_proj")[0]
    ch = u.shape[1]
    u = _causal_conv(u.reshape(bsz, seq, ch), conv_w, bias=conv_b)
    u = _ln_silu(u.reshape(bsz * seq, ch), ln_g, ln_b)
    return _matmul_resid(u, w_out, x, gate, seq, name="conf_out_proj")


def _rglru_mixer(h, x, gate, w_in, conv_w, conv_b, w_a, b_a, w_x, b_x, lam, w_out, *, bsz, seq):
    def epilogue(accs, ex, outs, ids):
        outs[0][...] = _gelu_tanh(accs[0]).astype(BF16)
        outs[1][...] = accs[1]

    gate_br, x_br = _split_proj(h, w_in, 2, epilogue, (BF16, F32), name="lru_in_proj")
    width = x_br.shape[1]
    xc = _causal_conv(x_br.reshape(bsz, seq, width), conv_w, bias=conv_b)
    y = _rglru(xc.reshape(bsz * seq, width), gate_br, w_a.astype(BF16), w_x.astype(BF16), b_a, b_x, lam, seq)
    return _matmul_resid(y, w_out, x, gate, seq, name="lru_out_proj")


def _dense_ffn(h, x, gate, w13, w2, layer, *, seq):
    act = _swiglu_act(h, w13, layer)
    return _matmul_resid(act, w2, x, gate, seq, name="ffn_down", layer=layer, tm=512, tk=w2.shape[1])


def _moe_ffn(h_seg, route, counts, x, gate, w13, w2, layer, *, seq):
    t = x.shape[0]
    n_e = w13.shape[1]
    tile = MOE_ROW_TILE
    n_rows = MOE_TOPK * t + n_e * tile
    counts = counts[0, :n_e].astype(jnp.int32)
    padded = (counts + tile - 1) // tile * tile
    ends = jnp.cumsum(padded)
    starts = ends - padded
    experts = route[:, ROUTE_EXPERT:ROUTE_EXPERT + MOE_TOPK].astype(jnp.int32)
    ranks = route[:, ROUTE_RANK:ROUTE_RANK + MOE_TOPK].astype(jnp.int32)
    pos = (starts[experts] + ranks).T.reshape(MOE_TOPK * t)
    tile_start = jnp.arange(n_rows // tile, dtype=jnp.int32) * tile
    tile_expert = jnp.minimum(jnp.sum(tile_start[:, None] >= ends[None, :], axis=1), n_e - 1).astype(jnp.int32)
    n_used = (ends[-1:] // tile).astype(jnp.int32)

    lay = _SegLayout(x.shape[1])
    hs = _moe_dispatch(h_seg, pos, n_rows, lay)
    act = _grouped_swiglu(hs, w13, layer, tile_expert, n_used, lay)
    y = _grouped_down(act, w2, layer, tile_expert, n_used, lay)
    return _moe_combine(y, pos, x, route, gate, seq, lay)


def kernel(x, c, ada_w, ada_b, ada_table, norm_mix, norm_ffn, norm_final, rel_bias, att_w_in, att_g_cq, att_g_ckv, att_w_uq, att_w_uk, att_w_uv, att_w_qidx, att_w_out, sconv_w_in, sconv_conv_w, sconv_w_out, conf_w_in, conf_conv_w, conf_conv_b, conf_ln_g, conf_ln_b, conf_w_out, lru_w_in, lru_conv_w, lru_conv_b, lru_w_a, lru_b_a, lru_w_x, lru_b_x, lru_lambda, lru_w_out, ffn_w13, ffn_w2, moe_router, moe_w13, moe_w2):
    bsz, seq, d = x.shape
    depth = ada_table.shape[0]
    xf = x.reshape(bsz * seq, d)
    mods = _ada_mod(c, ada_w, ada_b, ada_table)
    ffn_w2_b, moe_w13_b, moe_w2_b = ffn_w2.astype(BF16), moe_w13.astype(BF16), moe_w2.astype(BF16)
    for i in range(depth):
        shift_m, scale_m, gate_m, shift_f, scale_f, gate_f = (mods[i][:, k:k + 1, :] for k in range(N_ADA))
        h = _norm_mod(xf, norm_mix[i], shift_m, scale_m, seq)
        kind, j = i % 4, i // 4
        if kind == 0:
            xf = _dsa_attention(h, xf, gate_m, rel_bias, att_w_in[j], att_g_cq[j], att_g_ckv[j], att_w_uq[j],
                                att_w_uk[j], att_w_uv[j], att_w_qidx[j], att_w_out[j], bsz=bsz, seq=seq)
        elif kind == 1:
            xf = _short_conv_mixer(h, xf, gate_m, sconv_w_in[j], sconv_conv_w[j], sconv_w_out[j], bsz=bsz, seq=seq)
        elif kind == 2:
            xf = _conformer_mixer(h, xf, gate_m, conf_w_in[j], conf_conv_w[j], conf_conv_b[j], conf_ln_g[j],
                                  conf_ln_b[j], conf_w_out[j], bsz=bsz, seq=seq)
        else:
            xf = _rglru_mixer(h, xf, gate_m, lru_w_in[j], lru_conv_w[j], lru_conv_b[j], lru_w_a[j], lru_b_a[j],
                              lru_w_x[j], lru_b_x[j], lru_lambda[j], lru_w_out[j], bsz=bsz, seq=seq)
        if i % 2 == 0:
            h = _norm_mod(xf, norm_ffn[i], shift_f, scale_f, seq)
            xf = _dense_ffn(h, xf, gate_f, ffn_w13, ffn_w2_b, i // 2, seq=seq)
        else:
            h, route, counts = _norm_mod(xf, norm_ffn[i], shift_f, scale_f, seq, router=moe_router[i // 2])
            xf = _moe_ffn(h, route, counts, xf, gate_f, moe_w13_b, moe_w2_b, i // 2, seq=seq)
    return _final_norm(xf, norm_final).reshape(bsz, seq, d)
```

```python
import functools
import math

import jax
import jax.numpy as jnp
from jax import lax
from jax.experimental import pallas as pl
from jax.experimental.pallas import tpu as pltpu

F32 = jnp.float32
BF16 = jnp.bfloat16

NORM_EPS = 1e-6
N_ADA = 6
MOE_TOPK = 2
LRU_C = 8.0
REL_BUCKETS = 32
REL_MAX_DIST = 128
TOPK_MAX = 256

LANES = 128
SUBLANES = 8
Q_BLOCK = 128
KEY_CHUNK = 256
HEAD_GROUP = 8
CONV_HALO = 32
VMEM_LIMIT = 56 * 1024 * 1024
MASK_NEG = -1e30
INT_MIN = -2 ** 31


def _cparams(*sem):
    return pltpu.CompilerParams(dimension_semantics=sem, vmem_limit_bytes=VMEM_LIMIT)


def _sigmoid(x):
    return 1.0 / (1.0 + jnp.exp(-x))


def _silu(x):
    return x * _sigmoid(x)


def _gelu_tanh(x):
    return 0.5 * x * (1.0 + jnp.tanh(math.sqrt(2.0 / math.pi) * (x + 0.044715 * (x * x * x))))


def _rms(x, g):
    return x * lax.rsqrt(jnp.mean(x * x, axis=-1, keepdims=True) + NORM_EPS) * g


def _mm_kernel(*refs, n_w, n_ex, n_out, nk, cast_w, epilogue, a_fn):
    a_ref = refs[0]
    w_refs = refs[1:1 + n_w]
    ex_refs = refs[1 + n_w:1 + n_w + n_ex]
    out_refs = refs[1 + n_w + n_ex:1 + n_w + n_ex + n_out]
    scr_refs = refs[1 + n_w + n_ex + n_out:]
    a = a_fn(a_ref[...])
    if cast_w:
        i, j = pl.program_id(1), pl.program_id(0)

        @pl.when(i == 0)
        def _():
            for scr, w in zip(scr_refs, w_refs):
                scr[...] = w[...].astype(BF16)

        accs = [jnp.dot(a, scr[...], preferred_element_type=F32) for scr in scr_refs]
        epilogue(accs, ex_refs, out_refs, (i, j))
        return
    ids = (pl.program_id(0), pl.program_id(1))
    if nk == 1:
        accs = [jnp.dot(a, w[...], preferred_element_type=F32) for w in w_refs]
        epilogue(accs, ex_refs, out_refs, ids)
        return
    k = pl.program_id(2)

    @pl.when(k == 0)
    def _():
        for acc in scr_refs:
            acc[...] = jnp.zeros_like(acc)

    for acc, w in zip(scr_refs, w_refs):
        acc[...] += jnp.dot(a, w[...], preferred_element_type=F32)

    @pl.when(k == nk - 1)
    def _():
        epilogue([acc[...] for acc in scr_refs], ex_refs, out_refs, ids)


def _identity(v):
    return v


def _mm(a, a_spec, ws, extras, outs, epilogue, *, grid, name, acc_shapes=(), cast_w=False, a_fn=_identity):
    nk = grid[2] if len(grid) == 3 else 1
    in_specs = [a_spec] + [s for _, s in ws] + [s for _, s in extras]
    out_specs = [s for _, s in outs]
    if cast_w:
        assert nk == 1
        swap = lambda s: pl.BlockSpec(s.block_shape, lambda j, i, f=s.index_map: f(i, j))
        in_specs, out_specs = [swap(s) for s in in_specs], [swap(s) for s in out_specs]
        grid, sem = (grid[1], grid[0]), ("parallel", "arbitrary")
        scratch = [pltpu.VMEM(tuple(d for d in s.block_shape if d is not None), BF16) for _, s in ws]
    else:
        sem = ("parallel", "parallel") + (("arbitrary",) if nk > 1 else ())
        scratch = [pltpu.VMEM(s, F32) for s in acc_shapes] if nk > 1 else []
    kern = functools.partial(_mm_kernel, n_w=len(ws), n_ex=len(extras), n_out=len(outs), nk=nk, cast_w=cast_w,
                             epilogue=epilogue, a_fn=a_fn)
    return pl.pallas_call(
        kern, grid=grid, in_specs=in_specs, out_specs=out_specs, out_shape=[o for o, _ in outs],
        scratch_shapes=scratch, compiler_params=_cparams(*sem), name=name,
    )(a, *[w for w, _ in ws], *[e for e, _ in extras])


def _ep_resid(accs, ex, outs, ids):
    x_ref, gate_ref = ex
    outs[0][...] = x_ref[...] + gate_ref[0] * accs[0]


def _matmul_resid(a, w, x, gate, seq, *, name, layer=None, tm=1024, tn=512, tk=4096):
    t, kdim = a.shape
    n = w.shape[-1]
    if layer is None:
        w_spec = lambda tk, tn: pl.BlockSpec((tk, tn), lambda i, j, *k: (k[0] if k else 0, j))
    else:
        w_spec = lambda tk, tn: pl.BlockSpec((None, tk, tn), lambda i, j, *k: (layer, k[0] if k else 0, j))
    tm, tn, tk = min(tm, seq), min(tn, n), min(tk, kdim)
    nk = kdim // tk
    grid = (t // tm, n // tn) + ((nk,) if nk > 1 else ())
    rows_per_batch = seq // tm
    out = _mm(
        a, pl.BlockSpec((tm, tk), lambda i, j, *k: (i, k[0] if k else 0)),
        [(w, w_spec(tk, tn))],
        [(x, pl.BlockSpec((tm, tn), lambda i, j, *k: (i, j))),
         (gate, pl.BlockSpec((1, 1, tn), lambda i, j, *k: (i // rows_per_batch, 0, j)))],
        [(jax.ShapeDtypeStruct((t, n), F32), pl.BlockSpec((tm, tn), lambda i, j, *k: (i, j)))],
        _ep_resid, grid=grid, acc_shapes=[(tm, tn)], name=name, cast_w=w.dtype == F32)
    return out[0]


def _swiglu_act(h, w13, layer, *, tm=512, tn=512):
    t, d = h.shape
    f = w13.shape[2] // 2
    tm, tn = min(tm, t), min(tn, f)
    nf = f // tn

    def epilogue(accs, ex, outs, ids):
        outs[0][...] = (_silu(accs[0]) * accs[1]).astype(BF16)

    out = _mm(
        h, pl.BlockSpec((tm, d), lambda i, j: (i, 0)),
        [(w13, pl.BlockSpec((None, d, tn), lambda i, j: (layer, 0, j))),
         (w13, pl.BlockSpec((None, d, tn), lambda i, j: (layer, 0, j + nf)))],
        [],
        [(jax.ShapeDtypeStruct((t, f), BF16), pl.BlockSpec((tm, tn), lambda i, j: (i, j)))],
        epilogue, grid=(t // tm, nf), name="ffn_swiglu", cast_w=True)
    return out[0]


MOE_ROW_TILE = 512
MOE_DMA_TILE = 256
SEG_PAD = SUBLANES


class _SegLayout:
    def __init__(self, d):
        self.n_seg = d // LANES
        self.slab = -(-self.n_seg // SUBLANES) * SUBLANES
        self.pitch = self.slab + SEG_PAD

    def seg(self, s, n_rows):
        return pl.ds(s, n_rows, stride=self.pitch)

    def slab_of(self, row):
        return pl.ds(pl.multiple_of(row * self.pitch, SUBLANES), self.slab)

    def store(self, ref, value):
        n_rows = value.shape[0]
        for s in range(self.n_seg):
            ref[self.seg(s, n_rows), :] = value[:, s * LANES:(s + 1) * LANES]
        for s in range(self.n_seg, self.pitch):
            ref[self.seg(s, n_rows), :] = jnp.zeros((n_rows, LANES), ref.dtype)

    def row_copy(self, src, src_row, dst, dst_row, sem):
        return pltpu.make_async_copy(src.at[self.slab_of(src_row)], dst.at[self.slab_of(dst_row)], sem)


def _dispatch_kernel(pos_ref, h_ref, zero_hbm, out_hbm, sem, *, n_tok, tile, lay):
    del zero_hbm
    base = pl.program_id(0) * tile

    def copies(i):
        return [lay.row_copy(h_ref, i, out_hbm, pos_ref[k * n_tok + base + i], sem) for k in range(MOE_TOPK)]

    def start(i, carry):
        for cp in copies(i):
            cp.start()
        return carry

    def wait(i, carry):
        for cp in copies(i):
            cp.wait()
        return carry

    lax.fori_loop(0, tile, start, 0)
    lax.fori_loop(0, tile, wait, 0)


def _moe_dispatch(h_seg, pos, n_rows, lay):
    t = h_seg.shape[0] // lay.pitch
    tile = min(MOE_DMA_TILE, t)
    any_spec = pl.BlockSpec(memory_space=pl.ANY)
    return pl.pallas_call(
        functools.partial(_dispatch_kernel, n_tok=t, tile=tile, lay=lay),
        grid_spec=pltpu.PrefetchScalarGridSpec(
            num_scalar_prefetch=1, grid=(t // tile,),
            in_specs=[pl.BlockSpec((tile * lay.pitch, LANES), lambda i, p: (i, 0)), any_spec], out_specs=any_spec,
            scratch_shapes=[pltpu.SemaphoreType.DMA(())]),
        out_shape=jax.ShapeDtypeStruct((n_rows * lay.pitch, LANES), h_seg.dtype),
        input_output_aliases={2: 0},
        compiler_params=_cparams("arbitrary"), name="moe_dispatch")(
            pos, h_seg, jnp.zeros((n_rows * lay.pitch, LANES), h_seg.dtype))


def _grouped_swiglu_kernel(te_ref, nu_ref, a_ref, wg_ref, wu_ref, out_ref, a_scr, *, tm, lay):
    active = pl.program_id(1) < nu_ref[0]

    @pl.when(active)
    def _():
        for s in range(lay.n_seg):
            a_scr[:, s * LANES:(s + 1) * LANES] = a_ref[lay.seg(s, tm), :].astype(BF16)
        a = a_scr[...]
        g = jnp.dot(a, wg_ref[...], preferred_element_type=F32)
        u = jnp.dot(a, wu_ref[...], preferred_element_type=F32)
        out_ref[...] = (_silu(g) * u).astype(BF16)

    @pl.when(jnp.logical_not(active))
    def _():
        out_ref[...] = jnp.zeros_like(out_ref)


def _grouped_swiglu(a_seg, w13, layer, tile_expert, n_used, lay, *, tn=512):
    rows, kdim = a_seg.shape[0] // lay.pitch, w13.shape[2]
    f = w13.shape[3] // 2
    tm, tn = MOE_ROW_TILE, min(tn, f)
    nb = f // tn
    return pl.pallas_call(
        functools.partial(_grouped_swiglu_kernel, tm=tm, lay=lay),
        grid_spec=pltpu.PrefetchScalarGridSpec(
            num_scalar_prefetch=2, grid=(nb, rows // tm),
            in_specs=[pl.BlockSpec((tm * lay.pitch, LANES), lambda j, r, te, nu: (r, 0)),
                      pl.BlockSpec((None, None, kdim, tn), lambda j, r, te, nu: (layer, te[r], 0, j)),
                      pl.BlockSpec((None, None, kdim, tn), lambda j, r, te, nu: (layer, te[r], 0, j + nb))],
            out_specs=pl.BlockSpec((tm, tn), lambda j, r, te, nu: (r, j)),
            scratch_shapes=[pltpu.VMEM((tm, kdim), BF16)]),
        out_shape=jax.ShapeDtypeStruct((rows, f), BF16),
        compiler_params=_cparams("parallel", "parallel"), name="moe_swiglu")(
            tile_expert, n_used, a_seg, w13, w13)


def _grouped_down_kernel(te_ref, nu_ref, a_ref, w_ref, out_ref, acc_ref, *, nk, lay):
    active = pl.program_id(0) < nu_ref[0]
    k = pl.program_id(1)

    @pl.when(active & (k == 0))
    def _():
        acc_ref[...] = jnp.zeros_like(acc_ref)

    @pl.when(active)
    def _():
        acc_ref[...] += jnp.dot(a_ref[...], w_ref[...], preferred_element_type=F32)

    @pl.when(active & (k == nk - 1))
    def _():
        lay.store(out_ref, acc_ref[...])

    @pl.when(jnp.logical_not(active) & (k == nk - 1))
    def _():
        out_ref[...] = jnp.zeros_like(out_ref)


def _grouped_down(a, w2, layer, tile_expert, n_used, lay, *, tk=1024):
    rows, kdim = a.shape
    d = w2.shape[3]
    tm, tk = MOE_ROW_TILE, min(tk, kdim)
    nk = kdim // tk
    return pl.pallas_call(
        functools.partial(_grouped_down_kernel, nk=nk, lay=lay),
        grid_spec=pltpu.PrefetchScalarGridSpec(
            num_scalar_prefetch=2, grid=(rows // tm, nk),
            in_specs=[pl.BlockSpec((tm, tk), lambda r, k, te, nu: (r, k)),
                      pl.BlockSpec((None, None, tk, d), lambda r, k, te, nu: (layer, te[r], k, 0))],
            out_specs=pl.BlockSpec((tm * lay.pitch, LANES), lambda r, k, te, nu: (r, 0)),
            scratch_shapes=[pltpu.VMEM((tm, d), F32)]),
        out_shape=jax.ShapeDtypeStruct((rows * lay.pitch, LANES), F32),
        compiler_params=_cparams("parallel", "arbitrary"), name="moe_down")(tile_expert, n_used, a, w2)


def _combine_kernel(pos_ref, y_hbm, x_ref, route_ref, gate_ref, out_ref, buf0, buf1, sem, *, n_tok, tile, lay):
    base = pl.program_id(0) * tile
    bufs = (buf0, buf1)

    def copies(i):
        return [lay.row_copy(y_hbm, pos_ref[k * n_tok + base + i], bufs[k], i, sem) for k in range(MOE_TOPK)]

    def start(i, carry):
        for cp in copies(i):
            cp.start()
        return carry

    def wait(i, carry):
        for cp in copies(i):
            cp.wait()
        return carry

    lax.fori_loop(0, tile, start, 0)
    lax.fori_loop(0, tile, wait, 0)
    route = route_ref[...]
    w0 = route[:, ROUTE_WEIGHT:ROUTE_WEIGHT + 1]
    w1 = route[:, ROUTE_WEIGHT + 1:ROUTE_WEIGHT + 2]
    for s in range(lay.n_seg):
        cols = slice(s * LANES, (s + 1) * LANES)
        y = w0 * buf0[lay.seg(s, tile), :] + w1 * buf1[lay.seg(s, tile), :]
        out_ref[:, cols] = x_ref[:, cols] + gate_ref[0][:, cols] * y


def _moe_combine(y_seg, pos, x, route, gate, seq, lay):
    t, d = x.shape
    tile = min(MOE_DMA_TILE, seq)
    rpb = seq // tile
    row = pl.BlockSpec((tile, d), lambda i, p: (i, 0))
    buf = pltpu.VMEM((tile * lay.pitch, LANES), F32)
    return pl.pallas_call(
        functools.partial(_combine_kernel, n_tok=t, tile=tile, lay=lay),
        grid_spec=pltpu.PrefetchScalarGridSpec(
            num_scalar_prefetch=1, grid=(t // tile,),
            in_specs=[pl.BlockSpec(memory_space=pl.ANY), row,
                      pl.BlockSpec((tile, LANES), lambda i, p: (i, 0)),
                      pl.BlockSpec((1, 1, d), lambda i, p: (i // rpb, 0, 0))],
            out_specs=row,
            scratch_shapes=[buf, buf, pltpu.SemaphoreType.DMA(())]),
        out_shape=jax.ShapeDtypeStruct((t, d), F32),
        compiler_params=_cparams("arbitrary"), name="moe_combine")(pos, y_seg, x, route, gate)


def _ada_mod(c, ada_w, ada_b, ada_table):
    bsz, d = c.shape
    b = 16
    c = jnp.pad(c, ((0, b - bsz), (0, 0)))
    depth = ada_table.shape[0]
    n = ada_w.shape[1]
    tn = min(512, n)

    def epilogue(accs, ex, outs, ids):
        bias_ref, tab_ref = ex
        outs[0][...] = (accs[0] + bias_ref[...])[None] + tab_ref[...]

    out = _mm(
        c, pl.BlockSpec((b, d), lambda i, j: (0, 0)),
        [(ada_w, pl.BlockSpec((d, tn), lambda i, j: (0, j)))],
        [(ada_b.reshape(1, n), pl.BlockSpec((1, tn), lambda i, j: (0, j))),
         (ada_table.reshape(depth, 1, n), pl.BlockSpec((depth, 1, tn), lambda i, j: (0, 0, j)))],
        [(jax.ShapeDtypeStruct((depth, b, n), F32), pl.BlockSpec((depth, b, tn), lambda i, j: (0, 0, j)))],
        epilogue, grid=(1, n // tn), name="ada_mod", cast_w=True, a_fn=lambda v: _silu(v).astype(BF16))
    return out[0][:, :bsz].reshape(depth, bsz, N_ADA, d)


def _norm_kernel(*refs, modulate, n_experts):
    if not modulate:
        x_ref, g_ref, out_ref = refs
        out_ref[...] = _rms(x_ref[...], g_ref[...])
        return
    x_ref, g_ref, sh_ref, sc_ref = refs[:4]
    h = _rms(x_ref[...], g_ref[...]) * (1.0 + sc_ref[0]) + sh_ref[0]
    if not n_experts:
        refs[4][...] = h.astype(BF16)
        return
    r_ref, hseg_ref, route_ref, count_ref, carry_ref = refs[4:]

    @pl.when(pl.program_id(0) == 0)
    def _():
        carry_ref[...] = jnp.zeros_like(carry_ref)

    _SegLayout(h.shape[1]).store(hseg_ref, h)
    logits = jnp.dot(h, r_ref[...], preferred_element_type=F32, precision=lax.Precision.HIGHEST)
    lane = lax.broadcasted_iota(jnp.int32, logits.shape, 1)
    lg = jnp.where(lane < n_experts, logits, -jnp.inf)
    m1 = jnp.max(lg, axis=1, keepdims=True)
    i1 = jnp.min(jnp.where(lg == m1, lane, LANES), axis=1, keepdims=True)
    lg2 = jnp.where(lane == i1, -jnp.inf, lg)
    m2 = jnp.max(lg2, axis=1, keepdims=True)
    i2 = jnp.min(jnp.where(lg2 == m2, lane, LANES), axis=1, keepdims=True)
    e2 = jnp.exp(m2 - m1)
    w1 = 1.0 / (1.0 + e2)
    sel = jnp.where((lane == i1) | (lane == i2), 1.0, 0.0)
    tm = sel.shape[0]
    lower = lax.broadcasted_iota(jnp.int32, (tm, tm), 0) >= lax.broadcasted_iota(jnp.int32, (tm, tm), 1)
    cum = jnp.dot(jnp.where(lower, 1.0, 0.0).astype(BF16), sel.astype(BF16), preferred_element_type=F32)
    rank = cum - sel + carry_ref[...]
    r1 = jnp.sum(jnp.where(lane == i1, rank, 0.0), axis=1, keepdims=True)
    r2 = jnp.sum(jnp.where(lane == i2, rank, 0.0), axis=1, keepdims=True)
    carry_ref[...] += jnp.sum(sel, axis=0, keepdims=True)
    count_ref[...] = carry_ref[...]
    cols = (i1.astype(F32), i2.astype(F32), r1, r2, w1, e2 * w1)
    table = jnp.zeros_like(logits)
    for k, col in enumerate(cols):
        table = jnp.where(lane == k, col, table)
    route_ref[...] = table


ROUTE_EXPERT, ROUTE_RANK, ROUTE_WEIGHT = 0, 2, 4


def _norm_mod(x, g, shift, scale, seq, router=None, *, tm=256):
    t, d = x.shape
    tm = min(tm, seq)
    rpb = seq // tm
    row = pl.BlockSpec((tm, d), lambda i: (i, 0))
    vec = pl.BlockSpec((1, d), lambda i: (0, 0))
    per_batch = pl.BlockSpec((1, 1, d), lambda i: (i // rpb, 0, 0))
    ins = [x, g.reshape(1, d), shift, scale]
    in_specs = [row, vec, per_batch, per_batch]
    out_shape = [jax.ShapeDtypeStruct((t, d), BF16)]
    out_specs = [row]
    n_experts = 0
    scratch = []
    if router is not None:
        n_experts = router.shape[1]
        pitch = _SegLayout(d).pitch
        out_shape = [jax.ShapeDtypeStruct((t * pitch, LANES), F32)]
        out_specs = [pl.BlockSpec((tm * pitch, LANES), lambda i: (i, 0))]
        ins.append(jnp.pad(router, ((0, 0), (0, LANES - n_experts))))
        in_specs.append(pl.BlockSpec((d, LANES), lambda i: (0, 0)))
        out_shape += [jax.ShapeDtypeStruct((t, LANES), F32), jax.ShapeDtypeStruct((1, LANES), F32)]
        out_specs += [pl.BlockSpec((tm, LANES), lambda i: (i, 0)), pl.BlockSpec((1, LANES), lambda i: (0, 0))]
        scratch = [pltpu.VMEM((1, LANES), F32)]
    res = pl.pallas_call(
        functools.partial(_norm_kernel, modulate=True, n_experts=n_experts),
        grid=(t // tm,), in_specs=in_specs, out_specs=out_specs, out_shape=out_shape, scratch_shapes=scratch,
        compiler_params=_cparams("arbitrary" if router is not None else "parallel"),
        name="norm_router" if router is not None else "norm_mod")(*ins)
    return res if router is not None else res[0]


def _final_norm(x, g, *, tm=256):
    t, d = x.shape
    tm = min(tm, t)
    row = pl.BlockSpec((tm, d), lambda i: (i, 0))
    return pl.pallas_call(
        functools.partial(_norm_kernel, modulate=False, n_experts=0),
        grid=(t // tm,), in_specs=[row, pl.BlockSpec((1, d), lambda i: (0, 0))], out_specs=row,
        out_shape=jax.ShapeDtypeStruct((t, d), F32), compiler_params=_cparams("parallel"),
        name="final_norm")(x, g.reshape(1, d))


def _conv_kernel(*refs, width, ts, tc, has_bias, has_mul):
    cur_ref, halo_ref, w_ref = refs[:3]
    rest = list(refs[3:])
    b_ref = rest.pop(0) if has_bias else None
    mul_ref = rest.pop(0) if has_mul else None
    out_ref, buf = rest
    first = pl.program_id(1) == 0
    buf[0:CONV_HALO, :] = jnp.where(first, 0.0, halo_ref[0])
    buf[CONV_HALO:CONV_HALO + ts, :] = cur_ref[0]
    rows = 32
    for r in range(0, ts, rows):
        acc = None
        for k in range(width):
            term = w_ref[k:k + 1, :] * buf[pl.ds(CONV_HALO + r - (width - 1) + k, rows), :]
            acc = term if acc is None else acc + term
        if has_bias:
            acc = acc + b_ref[...]
        if has_mul:
            acc = acc * mul_ref[0, r:r + rows, :].astype(F32)
        out_ref[0, r:r + rows, :] = acc.astype(out_ref.dtype)


def _causal_conv(x, conv_w, bias=None, mul=None, out_dtype=F32, *, ts=256, tc=512):
    b, s, ch = x.shape
    width = conv_w.shape[0]
    assert width - 1 <= CONV_HALO
    if width <= 4:
        tc *= 2
    ts, tc = min(ts, s), min(tc, ch)
    hpb = ts // CONV_HALO
    blk = pl.BlockSpec((1, ts, tc), lambda bi, si, ci: (bi, si, ci))
    ins = [x, x, conv_w]
    in_specs = [blk,
                pl.BlockSpec((1, CONV_HALO, tc), lambda bi, si, ci: (bi, jnp.maximum(si * hpb - 1, 0), ci)),
                pl.BlockSpec((width, tc), lambda bi, si, ci: (0, ci))]
    if bias is not None:
        ins.append(bias.reshape(1, ch))
        in_specs.append(pl.BlockSpec((1, tc), lambda bi, si, ci: (0, ci)))
    if mul is not None:
        ins.append(mul)
        in_specs.append(blk)
    return pl.pallas_call(
        functools.partial(_conv_kernel, width=width, ts=ts, tc=tc, has_bias=bias is not None,
                          has_mul=mul is not None),
        grid=(b, s // ts, ch // tc), in_specs=in_specs, out_specs=blk,
        out_shape=jax.ShapeDtypeStruct((b, s, ch), out_dtype),
        scratch_shapes=[pltpu.VMEM((CONV_HALO + ts, tc), F32)],
        compiler_params=_cparams("parallel", "parallel", "parallel"), name=f"causal_conv{width}")(*ins)


def _ln_silu_kernel(x_ref, g_ref, b_ref, out_ref):
    x = x_ref[...]
    mu = jnp.mean(x, axis=-1, keepdims=True)
    xc = x - mu
    var = jnp.mean(xc * xc, axis=-1, keepdims=True)
    y = xc * lax.rsqrt(var + NORM_EPS) * g_ref[...] + b_ref[...]
    out_ref[...] = _silu(y).astype(BF16)


def _ln_silu(x, g, b, *, tm=256):
    t, d = x.shape
    tm = min(tm, t)
    row = pl.BlockSpec((tm, d), lambda i: (i, 0))
    vec = pl.BlockSpec((1, d), lambda i: (0, 0))
    return pl.pallas_call(
        _ln_silu_kernel, grid=(t // tm,), in_specs=[row, vec, vec], out_specs=row,
        out_shape=jax.ShapeDtypeStruct((t, d), BF16), compiler_params=_cparams("parallel"), name="ln_silu")(
            x, g.reshape(1, d), b.reshape(1, d))


def _rglru_kernel(xc_ref, gate_ref, wa_ref, wx_ref, ba_ref, bx_ref, lam_ref, out_ref,
                  a_scr, u_scr, h_scr, *, ts, n_heads, blk, cw):
    @pl.when(pl.program_id(1) == 0)
    def _():
        h_scr[...] = jnp.zeros_like(h_scr)

    neg_lam = -lam_ref[...]
    softplus = jnp.maximum(neg_lam, 0.0) + jnp.log1p(jnp.exp(-jnp.abs(neg_lam)))
    for hd in range(n_heads):
        sl = slice(hd * blk, (hd + 1) * blk)
        xh = xc_ref[:, sl]
        xb = xh.astype(BF16)
        r = _sigmoid(jnp.dot(xb, wa_ref[hd], preferred_element_type=F32) + ba_ref[:, sl])
        i_g = _sigmoid(jnp.dot(xb, wx_ref[hd], preferred_element_type=F32) + bx_ref[:, sl])
        log_a = -LRU_C * r * softplus[:, sl]
        a_scr[:, sl] = jnp.exp(log_a)
        u_scr[:, sl] = jnp.sqrt(1.0 - jnp.exp(2.0 * log_a)) * (i_g * xh)

    row = lax.broadcasted_iota(jnp.int32, (SUBLANES, cw), 0)
    width = n_heads * blk
    for c0 in range(0, width, cw):
        def body(g, h_prev, c0=c0):
            r0 = pl.multiple_of(g * SUBLANES, SUBLANES)
            a8 = a_scr[pl.ds(r0, SUBLANES), c0:c0 + cw]
            b8 = u_scr[pl.ds(r0, SUBLANES), c0:c0 + cw]
            for d in (1, 2, 4):
                keep = row >= d
                b8 = jnp.where(keep, a8 * pltpu.roll(b8, d, axis=0) + b8, b8)
                a8 = jnp.where(keep, a8 * pltpu.roll(a8, d, axis=0), a8)
            h8 = b8 + a8 * h_prev
            u_scr[pl.ds(r0, SUBLANES), c0:c0 + cw] = h8
            return h8[SUBLANES - 1:SUBLANES, :]

        h_scr[:, c0:c0 + cw] = lax.fori_loop(0, ts // SUBLANES, body, h_scr[:, c0:c0 + cw])
    out_ref[...] = (u_scr[...] * gate_ref[...].astype(F32)).astype(BF16)


def _rglru(xc, gate, w_a, w_x, b_a, b_x, lam, seq, *, ts=256):
    t, width = xc.shape
    n_heads, blk, _ = w_a.shape
    ts = min(ts, seq)
    nts = seq // ts
    row = pl.BlockSpec((ts, width), lambda b, j: (b * nts + j, 0))
    wsp = pl.BlockSpec((n_heads, blk, blk), lambda b, j: (0, 0, 0))
    vec = pl.BlockSpec((1, width), lambda b, j: (0, 0))
    return pl.pallas_call(
        functools.partial(_rglru_kernel, ts=ts, n_heads=n_heads, blk=blk, cw=min(1024, width)),
        grid=(t // seq, nts), in_specs=[row, row, wsp, wsp, vec, vec, vec], out_specs=row,
        out_shape=jax.ShapeDtypeStruct((t, width), BF16),
        scratch_shapes=[pltpu.VMEM((ts, width), F32), pltpu.VMEM((ts, width), F32), pltpu.VMEM((1, width), F32)],
        compiler_params=_cparams("parallel", "arbitrary"), name="rglru")(
            xc, gate, w_a, w_x, b_a.reshape(1, width), b_x.reshape(1, width), lam.reshape(1, width))


def _bias_tile_kernel(rb_ref, out_ref):
    hd = pl.program_id(0)
    shape = (2 * Q_BLOCK, Q_BLOCK)
    dist = Q_BLOCK + lax.broadcasted_iota(jnp.int32, shape, 1) - lax.broadcasted_iota(jnp.int32, shape, 0)
    dist = jnp.maximum(dist, 0)
    max_exact = REL_BUCKETS // 2
    large = max_exact + (jnp.log(jnp.maximum(dist, 1).astype(F32) / max_exact)
                         / math.log(REL_MAX_DIST / max_exact) * (REL_BUCKETS - max_exact)).astype(jnp.int32)
    bucket = jnp.where(dist < max_exact, dist, jnp.minimum(large, REL_BUCKETS - 1))
    tile = jnp.zeros(shape, F32)
    for bkt in range(REL_BUCKETS):
        tile = jnp.where(bucket == bkt, rb_ref[bkt, hd], tile)
    out_ref[0] = tile - rb_ref[REL_BUCKETS - 1, hd]


def _bias_tile(rel_bias):
    n_heads = rel_bias.shape[1]
    return pl.pallas_call(
        _bias_tile_kernel, grid=(n_heads,),
        in_specs=[pl.BlockSpec(memory_space=pltpu.SMEM)],
        out_specs=pl.BlockSpec((1, 2 * Q_BLOCK, Q_BLOCK), lambda h: (h, 0, 0)),
        out_shape=jax.ShapeDtypeStruct((n_heads, 2 * Q_BLOCK, Q_BLOCK), F32),
        compiler_params=_cparams("parallel"), name="dsa_bias_tile")(rel_bias)


def _dsa_kernel(qidx_ref, wrow_ref, kidx_ref, ckv_ref, ckvt_ref, qlat_ref, bias_ref, wuv_ref, out_ref,
                score_ref, key_ref, madd_ref, z_ref, acc_ref, *, n_slab, k_sel, idx_heads, kv_dim, head_dim):
    qb = pl.program_id(1)
    cols = HEAD_GROUP * Q_BLOCK
    sub = KEY_CHUNK // SUBLANES
    nt = (((1,), (1,)), ((), ()))
    n_chunks = (qb + 2) // 2

    @pl.when(pl.program_id(2) == 0)
    def _select():
        def chunk_body(c, carry):
            k0 = pl.multiple_of(c * KEY_CHUNK, KEY_CHUNK)
            keys = kidx_ref[0, pl.ds(k0, KEY_CHUNK), :]

            def head_body(hg, acc):
                h0 = pl.multiple_of(hg * HEAD_GROUP, HEAD_GROUP)
                q = qidx_ref[0, pl.ds(h0, HEAD_GROUP)].reshape(cols, LANES)
                dots = lax.dot_general(keys, q, nt, preferred_element_type=F32)
                weighted = jnp.maximum(dots, 0.0) * wrow_ref[0, hg]
                for hh in range(HEAD_GROUP):
                    acc = acc + weighted[:, hh * Q_BLOCK:(hh + 1) * Q_BLOCK]
                return acc

            score_ref[pl.ds(k0, KEY_CHUNK), :] = lax.fori_loop(
                0, idx_heads // HEAD_GROUP, head_body, jnp.zeros((KEY_CHUNK, Q_BLOCK), F32), unroll=True)
            return carry

        lax.fori_loop(0, n_chunks, chunk_body, 0)

        def zero_body(c, carry):
            score_ref[pl.ds(pl.multiple_of(c * KEY_CHUNK, KEY_CHUNK), KEY_CHUNK), :] = jnp.zeros(
                (KEY_CHUNK, Q_BLOCK), F32)
            return carry

        lax.fori_loop(n_chunks, n_slab // 2, zero_body, 0)

        shape = (n_slab * LANES, Q_BLOCK)
        bits = pltpu.bitcast(score_ref[...], jnp.int32)
        key = jnp.where(bits >= 0, bits, bits ^ jnp.int32(0x7FFFFFFF))
        causal = lax.broadcasted_iota(jnp.int32, shape, 0) <= qb * Q_BLOCK + lax.broadcasted_iota(jnp.int32, shape, 1)
        key = jnp.where(causal, key, INT_MIN)
        key_ref[...] = key

        def bit_body(it, thr):
            cand = thr + lax.shift_left(jnp.int32(1), 31 - it)
            cand_b = jnp.broadcast_to(cand, (SUBLANES, Q_BLOCK))[None, None]
            lanes_of_sum = (SUBLANES, SUBLANES, Q_BLOCK)

            def count_body(c, cnt):
                kc = key_ref[pl.ds(pl.multiple_of(c * 2 * KEY_CHUNK, 2 * KEY_CHUNK), 2 * KEY_CHUNK), :]
                hit = jnp.where(kc.reshape((2 * sub // SUBLANES,) + lanes_of_sum) >= cand_b, 1.0, 0.0)
                return cnt + jnp.sum(hit, axis=0)

            cnt = lax.fori_loop(0, (n_chunks + 1) // 2, count_body, jnp.zeros(lanes_of_sum, F32))
            cnt = jnp.sum(jnp.sum(cnt, axis=0), axis=0, keepdims=True)
            return jnp.where(cnt >= k_sel, cand, thr)

        thr = lax.fori_loop(0, 32, bit_body, jnp.full((1, Q_BLOCK), INT_MIN, jnp.int32))
        madd_ref[0:Q_BLOCK, :] = jnp.full((Q_BLOCK, Q_BLOCK), MASK_NEG, F32)
        madd_ref[Q_BLOCK:, :] = jnp.where(causal & (key >= thr), 0.0, MASK_NEG)

    q = qlat_ref[0].reshape(cols, kv_dim)
    n_far = qb // 2
    near_row = qb * Q_BLOCK

    def logits_of(kv):
        return lax.dot_general(kv, q, nt, preferred_element_type=F32)

    def per_head(m):
        return jnp.concatenate([m] * HEAD_GROUP, axis=1)

    def col_max(z):
        return jnp.max(z.reshape(sub, SUBLANES, cols), axis=0)

    def far_chunk(c, m8):
        r0 = pl.multiple_of(Q_BLOCK + c * KEY_CHUNK, Q_BLOCK)
        kv = ckv_ref[0, pl.ds(r0, KEY_CHUNK), :]
        row = r0 + lax.broadcasted_iota(jnp.int32, (KEY_CHUNK, Q_BLOCK), 0)
        madd = jnp.where(row < near_row, madd_ref[pl.ds(r0, KEY_CHUNK), :], MASK_NEG)
        z = logits_of(kv) + per_head(madd)
        z_ref[c] = z
        return jnp.maximum(m8, col_max(z))

    def in_pairs(n, one, init):
        carry = lax.fori_loop(0, n // 2, lambda pair, v: one(2 * pair + 1, one(2 * pair, v)), init)
        return lax.cond(n % 2 == 1, lambda v: one(n - 1, v), lambda v: v, carry)

    m8 = in_pairs(n_far, far_chunk, jnp.full((SUBLANES, cols), MASK_NEG, F32))
    r0 = pl.multiple_of(near_row, Q_BLOCK)
    bias = jnp.concatenate([bias_ref[hh] for hh in range(HEAD_GROUP)], axis=1)
    z = (logits_of(ckv_ref[0, pl.ds(r0, KEY_CHUNK), :]) + per_head(madd_ref[pl.ds(r0, KEY_CHUNK), :])) + bias
    z_ref[n_far] = z
    m8 = jnp.maximum(m8, col_max(z))
    m8 = jnp.broadcast_to(jnp.max(m8, axis=0, keepdims=True), (SUBLANES, cols))[None]
    acc_ref[...] = jnp.zeros(acc_ref.shape, F32)

    def weigh(c, l8):
        s0 = jnp.where(c < n_far, 2 * c + 1, qb)
        kvt = jnp.concatenate([ckvt_ref[0, s0], ckvt_ref[0, s0 + 1]], axis=1)
        p = jnp.exp(z_ref[c].reshape(sub, SUBLANES, cols) - m8)
        acc_ref[...] += jnp.dot(kvt, p.reshape(KEY_CHUNK, cols).astype(BF16), preferred_element_type=F32)
        return l8 + jnp.sum(p, axis=0)

    l8 = in_pairs(n_far + 1, weigh, jnp.zeros((SUBLANES, cols), F32))
    inv = 1.0 / jnp.broadcast_to(jnp.sum(l8, axis=0, keepdims=True), (SUBLANES, cols))
    o_lat = (acc_ref[...].reshape(kv_dim // SUBLANES, SUBLANES, cols) * inv[None]).reshape(kv_dim, cols)
    o_lat = o_lat.astype(BF16)
    for hh in range(HEAD_GROUP):
        o_h = jnp.dot(wuv_ref[hh], o_lat[:, hh * Q_BLOCK:(hh + 1) * Q_BLOCK], preferred_element_type=F32)
        out_ref[:, hh * head_dim:(hh + 1) * head_dim] = o_h.T.astype(BF16)


def _dsa_core(qidx, wrow, kidx, ckv_pad, ckvt_pad, qlat, bias, w_uvt, *, bsz, seq, k_sel):
    n_qb = seq // Q_BLOCK
    idx_heads = qidx.shape[1]
    n_heads, kv_dim = qlat.shape[1], qlat.shape[3]
    head_dim = w_uvt.shape[1]
    n_groups = n_heads // HEAD_GROUP
    cols = HEAD_GROUP * Q_BLOCK
    blk_q = lambda b, i, g: (b * n_qb + i, 0, 0, 0)
    per_batch = lambda b, i, g: (b, 0, 0)
    return pl.pallas_call(
        functools.partial(_dsa_kernel, n_slab=n_qb, k_sel=k_sel, idx_heads=idx_heads, kv_dim=kv_dim,
                          head_dim=head_dim),
        grid=(bsz, n_qb, n_groups),
        in_specs=[
            pl.BlockSpec((1, idx_heads, Q_BLOCK, LANES), blk_q),
            pl.BlockSpec((1, idx_heads // HEAD_GROUP, 1, cols), blk_q),
            pl.BlockSpec((1, seq, LANES), per_batch),
            pl.BlockSpec((1, seq + Q_BLOCK, kv_dim), per_batch),
            pl.BlockSpec((1, n_qb + 1, kv_dim, Q_BLOCK), lambda b, i, g: (b, 0, 0, 0)),
            pl.BlockSpec((1, HEAD_GROUP, Q_BLOCK, kv_dim), lambda b, i, g: (b * n_qb + i, g, 0, 0)),
            pl.BlockSpec((HEAD_GROUP, KEY_CHUNK, Q_BLOCK), lambda b, i, g: (g, 0, 0)),
            pl.BlockSpec((HEAD_GROUP, head_dim, kv_dim), lambda b, i, g: (g, 0, 0)),
        ],
        out_specs=pl.BlockSpec((Q_BLOCK, HEAD_GROUP * head_dim), lambda b, i, g: (b * n_qb + i, g)),
        out_shape=jax.ShapeDtypeStruct((bsz * seq, n_heads * head_dim), BF16),
        scratch_shapes=[
            pltpu.VMEM((seq, Q_BLOCK), F32),
            pltpu.VMEM((seq, Q_BLOCK), jnp.int32),
            pltpu.VMEM((seq + Q_BLOCK, Q_BLOCK), F32),
            pltpu.VMEM((n_qb // 2 + 1, KEY_CHUNK, cols), F32),
            pltpu.VMEM((kv_dim, cols), F32),
        ],
        compiler_params=_cparams("parallel", "parallel", "arbitrary"), name="dsa_core")(
            qidx, wrow, kidx, ckv_pad, ckvt_pad, qlat, bias, w_uvt)


def _dsa_attention(h, x, gate, rel_bias, w_in, g_cq, g_ckv, w_uq, w_uk, w_uv, w_qidx, w_out, *, bsz, seq):
    t, d = h.shape
    q_lora, kv_lora = g_cq.shape[0], g_ckv.shape[0]
    n_heads, head_dim = w_uk.shape[1], w_uk.shape[2]
    idx_dim = LANES
    idx_heads = w_qidx.shape[1] // idx_dim
    k_sel = min(TOPK_MAX, seq // 4)
    n_qb = seq // Q_BLOCK
    tm = min(512, seq)
    qpt = tm // Q_BLOCK

    w_in = w_in.astype(BF16)
    splits = (0, q_lora, q_lora + kv_lora, q_lora + kv_lora + idx_dim, w_in.shape[1])
    w_parts = [w_in[:, splits[i]:splits[i + 1]] for i in range(4)]
    head_scale = idx_heads ** -0.5 * idx_dim ** -0.5

    def in_epilogue(accs, ex, outs, ids):
        outs[0][...] = _rms(accs[0], ex[0][...]).astype(BF16)
        outs[1][...] = _rms(accs[1], ex[1][...]).astype(BF16)
        outs[2][...] = accs[2].astype(BF16)
        outs[3][...] = accs[3] * head_scale

    full = lambda n: pl.BlockSpec((d, n), lambda i, j: (0, 0))
    rowsp = lambda n: pl.BlockSpec((tm, n), lambda i, j: (i, 0))
    vecsp = lambda n: pl.BlockSpec((1, n), lambda i, j: (0, 0))
    widths = [w.shape[1] for w in w_parts]
    c_q, c_kv, k_idx, w_head = _mm(
        h, pl.BlockSpec((tm, d), lambda i, j: (i, 0)),
        [(w, full(n)) for w, n in zip(w_parts, widths)],
        [(g_cq.reshape(1, q_lora), vecsp(q_lora)), (g_ckv.reshape(1, kv_lora), vecsp(kv_lora))],
        [(jax.ShapeDtypeStruct((t, n), dt), rowsp(n)) for n, dt in zip(widths, (BF16, BF16, BF16, F32))],
        in_epilogue, grid=(t // tm, 1), name="dsa_in_proj")

    hpt = 4
    tn = hpt * idx_dim
    tm_q = min(2048, seq)
    qpt_q = tm_q // Q_BLOCK

    def qidx_epilogue(accs, ex, outs, ids):
        for hh in range(hpt):
            q_h = accs[0][:, hh * idx_dim:(hh + 1) * idx_dim]
            outs[0][:, hh] = q_h.reshape(qpt_q, Q_BLOCK, idx_dim).astype(BF16)

    qidx = _mm(
        c_q, pl.BlockSpec((tm_q, q_lora), lambda i, j: (i, 0)),
        [(w_qidx, pl.BlockSpec((q_lora, tn), lambda i, j: (0, j)))], [],
        [(jax.ShapeDtypeStruct((t // Q_BLOCK, idx_heads, Q_BLOCK, idx_dim), BF16),
          pl.BlockSpec((qpt_q, hpt, Q_BLOCK, idx_dim), lambda i, j: (i, j, 0, 0)))],
        qidx_epilogue, grid=(t // tm_q, idx_heads // hpt), name="dsa_qidx", cast_w=True)[0]

    w_ukt = jnp.transpose(w_uk, (1, 2, 0)).astype(BF16)
    logit_scale = head_dim ** -0.5

    def qlat_kernel(cq_ref, wuq_ref, wuk_ref, out_ref):
        qg = jnp.dot(cq_ref[...], wuq_ref[...], preferred_element_type=F32).astype(BF16)
        for hh in range(HEAD_GROUP):
            ql = jnp.dot(qg[:, hh * head_dim:(hh + 1) * head_dim], wuk_ref[hh], preferred_element_type=F32)
            out_ref[:, hh] = (ql * logit_scale).reshape(qpt, Q_BLOCK, kv_lora).astype(BF16)

    qlat = pl.pallas_call(
        qlat_kernel, grid=(t // tm, n_heads // HEAD_GROUP),
        in_specs=[pl.BlockSpec((tm, q_lora), lambda i, g: (i, 0)),
                  pl.BlockSpec((q_lora, HEAD_GROUP * head_dim), lambda i, g: (0, g)),
                  pl.BlockSpec((HEAD_GROUP, head_dim, kv_lora), lambda i, g: (g, 0, 0))],
        out_specs=pl.BlockSpec((qpt, HEAD_GROUP, Q_BLOCK, kv_lora), lambda i, g: (i, g, 0, 0)),
        out_shape=jax.ShapeDtypeStruct((t // Q_BLOCK, n_heads, Q_BLOCK, kv_lora), BF16),
        compiler_params=_cparams("parallel", "parallel"), name="dsa_qlat")(c_q, w_uq.astype(BF16), w_ukt)

    wrow = w_head.reshape(bsz * n_qb, Q_BLOCK, idx_heads // HEAD_GROUP, HEAD_GROUP).transpose(0, 2, 3, 1)
    wrow = wrow.reshape(bsz * n_qb, idx_heads // HEAD_GROUP, 1, HEAD_GROUP * Q_BLOCK)
    ckv_pad = jnp.pad(c_kv.reshape(bsz, seq, kv_lora), ((0, 0), (Q_BLOCK, 0), (0, 0)))
    ckvt_pad = ckv_pad.reshape(bsz, n_qb + 1, Q_BLOCK, kv_lora).transpose(0, 1, 3, 2)
    w_uvt = jnp.transpose(w_uv, (1, 2, 0)).astype(BF16)
    o = _dsa_core(qidx, wrow, k_idx.reshape(bsz, seq, idx_dim), ckv_pad, ckvt_pad, qlat,
                  _bias_tile(rel_bias.astype(F32)), w_uvt, bsz=bsz, seq=seq, k_sel=k_sel)
    return _matmul_resid(o, w_out, x, gate, seq, name="dsa_out_proj")


def _split_proj(h, w, n_parts, epilogue, out_dtypes, *, name, tm=512, tn=512):
    t, d = h.shape
    n = w.shape[1] // n_parts
    tm, tn = min(tm, t), min(tn, n)
    nb = n // tn
    out_sp = pl.BlockSpec((tm, tn), lambda i, j: (i, j))
    return _mm(
        h, pl.BlockSpec((tm, d), lambda i, j: (i, 0)),
        [(w, pl.BlockSpec((d, tn), lambda i, j, p=p: (0, j + p * nb))) for p in range(n_parts)], [],
        [(jax.ShapeDtypeStruct((t, n), dt), out_sp) for dt in out_dtypes],
        epilogue, grid=(t // tm, nb), name=name, cast_w=True)


def _short_conv_mixer(h, x, gate, w_in, conv_w, w_out, *, bsz, seq):
    def epilogue(accs, ex, outs, ids):
        outs[0][...] = accs[0].astype(BF16)
        outs[1][...] = accs[1] * accs[2]

    gate_b, cx = _split_proj(h, w_in, 3, epilogue, (BF16, F32), name="sconv_in_proj", tn=256)
    ch = cx.shape[1]
    y = _causal_conv(cx.reshape(bsz, seq, ch), conv_w, mul=gate_b.reshape(bsz, seq, ch), out_dtype=BF16)
    return _matmul_resid(y.reshape(bsz * seq, ch), w_out, x, gate, seq, name="sconv_out_proj")


def _conformer_mixer(h, x, gate, w_in, conv_w, conv_b, ln_g, ln_b, w_out, *, bsz, seq):
    def epilogue(accs, ex, outs, ids):
        outs[0][...] = accs[0] * _sigmoid(accs[1])

    u = _split_proj(h, w_in, 2, epilogue, (F32,), name="conf_in_proj")[0]
    ch = u.shape[1]
    u = _causal_conv(u.reshape(bsz, seq, ch), conv_w, bias=conv_b)
    u = _ln_silu(u.reshape(bsz * seq, ch), ln_g, ln_b)
    return _matmul_resid(u, w_out, x, gate, seq, name="conf_out_proj")


def _rglru_mixer(h, x, gate, w_in, conv_w, conv_b, w_a, b_a, w_x, b_x, lam, w_out, *, bsz, seq):
    def epilogue(accs, ex, outs, ids):
        outs[0][...] = _gelu_tanh(accs[0]).astype(BF16)
        outs[1][...] = accs[1]

    gate_br, x_br = _split_proj(h, w_in, 2, epilogue, (BF16, F32), name="lru_in_proj")
    width = x_br.shape[1]
    xc = _causal_conv(x_br.reshape(bsz, seq, width), conv_w, bias=conv_b)
    y = _rglru(xc.reshape(bsz * seq, width), gate_br, w_a.astype(BF16), w_x.astype(BF16), b_a, b_x, lam, seq)
    return _matmul_resid(y, w_out, x, gate, seq, name="lru_out_proj")


def _dense_ffn(h, x, gate, w13, w2, layer, *, seq):
    act = _swiglu_act(h, w13, layer)
    return _matmul_resid(act, w2, x, gate, seq, name="ffn_down", layer=layer, tm=512, tk=w2.shape[1])


def _moe_ffn(h_seg, route, counts, x, gate, w13, w2, layer, *, seq):
    t = x.shape[0]
    n_e = w13.shape[1]
    tile = MOE_ROW_TILE
    n_rows = MOE_TOPK * t + n_e * tile
    counts = counts[0, :n_e].astype(jnp.int32)
    padded = (counts + tile - 1) // tile * tile
    ends = jnp.cumsum(padded)
    starts = ends - padded
    experts = route[:, ROUTE_EXPERT:ROUTE_EXPERT + MOE_TOPK].astype(jnp.int32)
    ranks = route[:, ROUTE_RANK:ROUTE_RANK + MOE_TOPK].astype(jnp.int32)
    pos = (starts[experts] + ranks).T.reshape(MOE_TOPK * t)
    tile_start = jnp.arange(n_rows // tile, dtype=jnp.int32) * tile
    tile_expert = jnp.minimum(jnp.sum(tile_start[:, None] >= ends[None, :], axis=1), n_e - 1).astype(jnp.int32)
    n_used = (ends[-1:] // tile).astype(jnp.int32)

    lay = _SegLayout(x.shape[1])
    hs = _moe_dispatch(h_seg, pos, n_rows, lay)
    act = _grouped_swiglu(hs, w13, layer, tile_expert, n_used, lay)
    y = _grouped_down(act, w2, layer, tile_expert, n_used, lay)
    return _moe_combine(y, pos, x, route, gate, seq, lay)


def kernel(x, c, ada_w, ada_b, ada_table, norm_mix, norm_ffn, norm_final, rel_bias, att_w_in, att_g_cq, att_g_ckv, att_w_uq, att_w_uk, att_w_uv, att_w_qidx, att_w_out, sconv_w_in, sconv_conv_w, sconv_w_out, conf_w_in, conf_conv_w, conf_conv_b, conf_ln_g, conf_ln_b, conf_w_out, lru_w_in, lru_conv_w, lru_conv_b, lru_w_a, lru_b_a, lru_w_x, lru_b_x, lru_lambda, lru_w_out, ffn_w13, ffn_w2, moe_router, moe_w13, moe_w2):
    bsz, seq, d = x.shape
    depth = ada_table.shape[0]
    xf = x.reshape(bsz * seq, d)
    mods = _ada_mod(c, ada_w, ada_b, ada_table)
    ffn_w2_b, moe_w13_b, moe_w2_b = ffn_w2.astype(BF16), moe_w13.astype(BF16), moe_w2.astype(BF16)
    for i in range(depth):
        shift_m, scale_m, gate_m, shift_f, scale_f, gate_f = (mods[i][:, k:k + 1, :] for k in range(N_ADA))
        h = _norm_mod(xf, norm_mix[i], shift_m, scale_m, seq)
        kind, j = i % 4, i // 4
        if kind == 0:
            xf = _dsa_attention(h, xf, gate_m, rel_bias, att_w_in[j], att_g_cq[j], att_g_ckv[j], att_w_uq[j],
                                att_w_uk[j], att_w_uv[j], att_w_qidx[j], att_w_out[j], bsz=bsz, seq=seq)
        elif kind == 1:
            xf = _short_conv_mixer(h, xf, gate_m, sconv_w_in[j], sconv_conv_w[j], sconv_w_out[j], bsz=bsz, seq=seq)
        elif kind == 2:
            xf = _conformer_mixer(h, xf, gate_m, conf_w_in[j], conf_conv_w[j], conf_conv_b[j], conf_ln_g[j],
                                  conf_ln_b[j], conf_w_out[j], bsz=bsz, seq=seq)
        else:
            xf = _rglru_mixer(h, xf, gate_m, lru_w_in[j], lru_conv_w[j], lru_conv_b[j], lru_w_a[j], lru_b_a[j],
                              lru_w_x[j], lru_b_x[j], lru_lambda[j], lru_w_out[j], bsz=bsz, seq=seq)
        if i % 2 == 0:
            h = _norm_mod(xf, norm_ffn[i], shift_f, scale_f, seq)
            xf = _dense_ffn(h, xf, gate_f, ffn_w13, ffn_w2_b, i // 2, seq=seq)
        else:
            h, route, counts = _norm_mod(xf, norm_ffn[i], shift_f, scale_f, seq, router=moe_router[i // 2])
            xf = _moe_ffn(h, route, counts, xf, gate_f, moe_w13_b, moe_w2_b, i // 2, seq=seq)
    return _final_norm(xf, norm_final).reshape(bsz, seq, d)
```

```python
import functools
import math

import jax
import jax.numpy as jnp
from jax import lax
from jax.experimental import pallas as pl
from jax.experimental.pallas import tpu as pltpu

F32 = jnp.float32
BF16 = jnp.bfloat16

NORM_EPS = 1e-6
N_ADA = 6
MOE_TOPK = 2
LRU_C = 8.0
REL_BUCKETS = 32
REL_MAX_DIST = 128
TOPK_MAX = 256

LANES = 128
SUBLANES = 8
Q_BLOCK = 128
KEY_CHUNK = 256
HEAD_GROUP = 8
CONV_HALO = 32
VMEM_LIMIT = 56 * 1024 * 1024
MASK_NEG = -1e30
INT_MIN = -2 ** 31


def _cparams(*sem):
    return pltpu.CompilerParams(dimension_semantics=sem, vmem_limit_bytes=VMEM_LIMIT)


def _sigmoid(x):
    return 1.0 / (1.0 + jnp.exp(-x))


def _silu(x):
    return x * _sigmoid(x)


def _gelu_tanh(x):
    return 0.5 * x * (1.0 + jnp.tanh(math.sqrt(2.0 / math.pi) * (x + 0.044715 * (x * x * x))))


def _rms(x, g):
    return x * lax.rsqrt(jnp.mean(x * x, axis=-1, keepdims=True) + NORM_EPS) * g


def _mm_kernel(*refs, n_w, n_ex, n_out, nk, cast_w, epilogue, a_fn):
    a_ref = refs[0]
    w_refs = refs[1:1 + n_w]
    ex_refs = refs[1 + n_w:1 + n_w + n_ex]
    out_refs = refs[1 + n_w + n_ex:1 + n_w + n_ex + n_out]
    scr_refs = refs[1 + n_w + n_ex + n_out:]
    a = a_fn(a_ref[...])
    if cast_w:
        i, j = pl.program_id(1), pl.program_id(0)

        @pl.when(i == 0)
        def _():
            for scr, w in zip(scr_refs, w_refs):
                scr[...] = w[...].astype(BF16)

        accs = [jnp.dot(a, scr[...], preferred_element_type=F32) for scr in scr_refs]
        epilogue(accs, ex_refs, out_refs, (i, j))
        return
    ids = (pl.program_id(0), pl.program_id(1))
    if nk == 1:
        accs = [jnp.dot(a, w[...], preferred_element_type=F32) for w in w_refs]
        epilogue(accs, ex_refs, out_refs, ids)
        return
    k = pl.program_id(2)

    @pl.when(k == 0)
    def _():
        for acc in scr_refs:
            acc[...] = jnp.zeros_like(acc)

    for acc, w in zip(scr_refs, w_refs):
        acc[...] += jnp.dot(a, w[...], preferred_element_type=F32)

    @pl.when(k == nk - 1)
    def _():
        epilogue([acc[...] for acc in scr_refs], ex_refs, out_refs, ids)


def _identity(v):
    return v


def _mm(a, a_spec, ws, extras, outs, epilogue, *, grid, name, acc_shapes=(), cast_w=False, a_fn=_identity):
    nk = grid[2] if len(grid) == 3 else 1
    in_specs = [a_spec] + [s for _, s in ws] + [s for _, s in extras]
    out_specs = [s for _, s in outs]
    if cast_w:
        assert nk == 1
        swap = lambda s: pl.BlockSpec(s.block_shape, lambda j, i, f=s.index_map: f(i, j))
        in_specs, out_specs = [swap(s) for s in in_specs], [swap(s) for s in out_specs]
        grid, sem = (grid[1], grid[0]), ("parallel", "arbitrary")
        scratch = [pltpu.VMEM(tuple(d for d in s.block_shape if d is not None), BF16) for _, s in ws]
    else:
        sem = ("parallel", "parallel") + (("arbitrary",) if nk > 1 else ())
        scratch = [pltpu.VMEM(s, F32) for s in acc_shapes] if nk > 1 else []
    kern = functools.partial(_mm_kernel, n_w=len(ws), n_ex=len(extras), n_out=len(outs), nk=nk, cast_w=cast_w,
                             epilogue=epilogue, a_fn=a_fn)
    return pl.pallas_call(
        kern, grid=grid, in_specs=in_specs, out_specs=out_specs, out_shape=[o for o, _ in outs],
        scratch_shapes=scratch, compiler_params=_cparams(*sem), name=name,
    )(a, *[w for w, _ in ws], *[e for e, _ in extras])


def _ep_resid(accs, ex, outs, ids):
    x_ref, gate_ref = ex
    outs[0][...] = x_ref[...] + gate_ref[0] * accs[0]


def _matmul_resid(a, w, x, gate, seq, *, name, layer=None, tm=1024, tn=512, tk=4096):
    t, kdim = a.shape
    n = w.shape[-1]
    if layer is None:
        w_spec = lambda tk, tn: pl.BlockSpec((tk, tn), lambda i, j, *k: (k[0] if k else 0, j))
    else:
        w_spec = lambda tk, tn: pl.BlockSpec((None, tk, tn), lambda i, j, *k: (layer, k[0] if k else 0, j))
    tm, tn, tk = min(tm, seq), min(tn, n), min(tk, kdim)
    nk = kdim // tk
    grid = (t // tm, n // tn) + ((nk,) if nk > 1 else ())
    rows_per_batch = seq // tm
    out = _mm(
        a, pl.BlockSpec((tm, tk), lambda i, j, *k: (i, k[0] if k else 0)),
        [(w, w_spec(tk, tn))],
        [(x, pl.BlockSpec((tm, tn), lambda i, j, *k: (i, j))),
         (gate, pl.BlockSpec((1, 1, tn), lambda i, j, *k: (i // rows_per_batch, 0, j)))],
        [(jax.ShapeDtypeStruct((t, n), F32), pl.BlockSpec((tm, tn), lambda i, j, *k: (i, j)))],
        _ep_resid, grid=grid, acc_shapes=[(tm, tn)], name=name, cast_w=w.dtype == F32)
    return out[0]


def _swiglu_act(h, w13, layer, *, tm=512, tn=512):
    t, d = h.shape
    f = w13.shape[2] // 2
    tm, tn = min(tm, t), min(tn, f)
    nf = f // tn

    def epilogue(accs, ex, outs, ids):
        outs[0][...] = (_silu(accs[0]) * accs[1]).astype(BF16)

    out = _mm(
        h, pl.BlockSpec((tm, d), lambda i, j: (i, 0)),
        [(w13, pl.BlockSpec((None, d, tn), lambda i, j: (layer, 0, j))),
         (w13, pl.BlockSpec((None, d, tn), lambda i, j: (layer, 0, j + nf)))],
        [],
        [(jax.ShapeDtypeStruct((t, f), BF16), pl.BlockSpec((tm, tn), lambda i, j: (i, j)))],
        epilogue, grid=(t // tm, nf), name="ffn_swiglu", cast_w=True)
    return out[0]


MOE_ROW_TILE = 512
MOE_DMA_TILE = 256
SEG_PAD = SUBLANES


class _SegLayout:
    def __init__(self, d):
        self.n_seg = d // LANES
        self.slab = -(-self.n_seg // SUBLANES) * SUBLANES
        self.pitch = self.slab + SEG_PAD

    def seg(self, s, n_rows):
        return pl.ds(s, n_rows, stride=self.pitch)

    def slab_of(self, row):
        return pl.ds(pl.multiple_of(row * self.pitch, SUBLANES), self.slab)

    def store(self, ref, value):
        n_rows = value.shape[0]
        for s in range(self.n_seg):
            ref[self.seg(s, n_rows), :] = value[:, s * LANES:(s + 1) * LANES]
        for s in range(self.n_seg, self.pitch):
            ref[self.seg(s, n_rows), :] = jnp.zeros((n_rows, LANES), ref.dtype)

    def row_copy(self, src, src_row, dst, dst_row, sem):
        return pltpu.make_async_copy(src.at[self.slab_of(src_row)], dst.at[self.slab_of(dst_row)], sem)


def _dispatch_kernel(pos_ref, h_ref, zero_hbm, out_hbm, sem, *, n_tok, tile, lay):
    del zero_hbm
    base = pl.program_id(0) * tile

    def copies(i):
        return [lay.row_copy(h_ref, i, out_hbm, pos_ref[k * n_tok + base + i], sem) for k in range(MOE_TOPK)]

    def start(i, carry):
        for cp in copies(i):
            cp.start()
        return carry

    def wait(i, carry):
        for cp in copies(i):
            cp.wait()
        return carry

    lax.fori_loop(0, tile, start, 0)
    lax.fori_loop(0, tile, wait, 0)


def _moe_dispatch(h_seg, pos, n_rows, lay):
    t = h_seg.shape[0] // lay.pitch
    tile = min(MOE_DMA_TILE, t)
    any_spec = pl.BlockSpec(memory_space=pl.ANY)
    return pl.pallas_call(
        functools.partial(_dispatch_kernel, n_tok=t, tile=tile, lay=lay),
        grid_spec=pltpu.PrefetchScalarGridSpec(
            num_scalar_prefetch=1, grid=(t // tile,),
            in_specs=[pl.BlockSpec((tile * lay.pitch, LANES), lambda i, p: (i, 0)), any_spec], out_specs=any_spec,
            scratch_shapes=[pltpu.SemaphoreType.DMA(())]),
        out_shape=jax.ShapeDtypeStruct((n_rows * lay.pitch, LANES), h_seg.dtype),
        input_output_aliases={2: 0},
        compiler_params=_cparams("arbitrary"), name="moe_dispatch")(
            pos, h_seg, jnp.zeros((n_rows * lay.pitch, LANES), h_seg.dtype))


def _grouped_swiglu_kernel(te_ref, nu_ref, a_ref, wg_ref, wu_ref, out_ref, a_scr, *, tm, lay):
    active = pl.program_id(1) < nu_ref[0]

    @pl.when(active)
    def _():
        for s in range(lay.n_seg):
            a_scr[:, s * LANES:(s + 1) * LANES] = a_ref[lay.seg(s, tm), :].astype(BF16)
        a = a_scr[...]
        g = jnp.dot(a, wg_ref[...], preferred_element_type=F32)
        u = jnp.dot(a, wu_ref[...], preferred_element_type=F32)
        out_ref[...] = (_silu(g) * u).astype(BF16)

    @pl.when(jnp.logical_not(active))
    def _():
        out_ref[...] = jnp.zeros_like(out_ref)


def _grouped_swiglu(a_seg, w13, layer, tile_expert, n_used, lay, *, tn=512):
    rows, kdim = a_seg.shape[0] // lay.pitch, w13.shape[2]
    f = w13.shape[3] // 2
    tm, tn = MOE_ROW_TILE, min(tn, f)
    nb = f // tn
    return pl.pallas_call(
        functools.partial(_grouped_swiglu_kernel, tm=tm, lay=lay),
        grid_spec=pltpu.PrefetchScalarGridSpec(
            num_scalar_prefetch=2, grid=(nb, rows // tm),
            in_specs=[pl.BlockSpec((tm * lay.pitch, LANES), lambda j, r, te, nu: (r, 0)),
                      pl.BlockSpec((None, None, kdim, tn), lambda j, r, te, nu: (layer, te[r], 0, j)),
                      pl.BlockSpec((None, None, kdim, tn), lambda j, r, te, nu: (layer, te[r], 0, j + nb))],
            out_specs=pl.BlockSpec((tm, tn), lambda j, r, te, nu: (r, j)),
            scratch_shapes=[pltpu.VMEM((tm, kdim), BF16)]),
        out_shape=jax.ShapeDtypeStruct((rows, f), BF16),
        compiler_params=_cparams("parallel", "parallel"), name="moe_swiglu")(
            tile_expert, n_used, a_seg, w13, w13)


def _grouped_down_kernel(te_ref, nu_ref, a_ref, w_ref, out_ref, acc_ref, *, nk, lay):
    active = pl.program_id(0) < nu_ref[0]
    k = pl.program_id(1)

    @pl.when(active & (k == 0))
    def _():
        acc_ref[...] = jnp.zeros_like(acc_ref)

    @pl.when(active)
    def _():
        acc_ref[...] += jnp.dot(a_ref[...], w_ref[...], preferred_element_type=F32)

    @pl.when(active & (k == nk - 1))
    def _():
        lay.store(out_ref, acc_ref[...])

    @pl.when(jnp.logical_not(active) & (k == nk - 1))
    def _():
        out_ref[...] = jnp.zeros_like(out_ref)


def _grouped_down(a, w2, layer, tile_expert, n_used, lay, *, tk=1024):
    rows, kdim = a.shape
    d = w2.shape[3]
    tm, tk = MOE_ROW_TILE, min(tk, kdim)
    nk = kdim // tk
    return pl.pallas_call(
        functools.partial(_grouped_down_kernel, nk=nk, lay=lay),
        grid_spec=pltpu.PrefetchScalarGridSpec(
            num_scalar_prefetch=2, grid=(rows // tm, nk),
            in_specs=[pl.BlockSpec((tm, tk), lambda r, k, te, nu: (r, k)),
                      pl.BlockSpec((None, None, tk, d), lambda r, k, te, nu: (layer, te[r], k, 0))],
            out_specs=pl.BlockSpec((tm * lay.pitch, LANES), lambda r, k, te, nu: (r, 0)),
            scratch_shapes=[pltpu.VMEM((tm, d), F32)]),
        out_shape=jax.ShapeDtypeStruct((rows * lay.pitch, LANES), F32),
        compiler_params=_cparams("parallel", "arbitrary"), name="moe_down")(tile_expert, n_used, a, w2)


def _combine_kernel(pos_ref, y_hbm, x_ref, route_ref, gate_ref, out_ref, buf0, buf1, sem, *, n_tok, tile, lay):
    base = pl.program_id(0) * tile
    bufs = (buf0, buf1)

    def copies(i):
        return [lay.row_copy(y_hbm, pos_ref[k * n_tok + base + i], bufs[k], i, sem) for k in range(MOE_TOPK)]

    def start(i, carry):
        for cp in copies(i):
            cp.start()
        return carry

    def wait(i, carry):
        for cp in copies(i):
            cp.wait()
        return carry

    lax.fori_loop(0, tile, start, 0)
    lax.fori_loop(0, tile, wait, 0)
    route = route_ref[...]
    w0 = route[:, ROUTE_WEIGHT:ROUTE_WEIGHT + 1]
    w1 = route[:, ROUTE_WEIGHT + 1:ROUTE_WEIGHT + 2]
    for s in range(lay.n_seg):
        cols = slice(s * LANES, (s + 1) * LANES)
        y = w0 * buf0[lay.seg(s, tile), :] + w1 * buf1[lay.seg(s, tile), :]
        out_ref[:, cols] = x_ref[:, cols] + gate_ref[0][:, cols] * y


def _moe_combine(y_seg, pos, x, route, gate, seq, lay):
    t, d = x.shape
    tile = min(MOE_DMA_TILE, seq)
    rpb = seq // tile
    row = pl.BlockSpec((tile, d), lambda i, p: (i, 0))
    buf = pltpu.VMEM((tile * lay.pitch, LANES), F32)
    return pl.pallas_call(
        functools.partial(_combine_kernel, n_tok=t, tile=tile, lay=lay),
        grid_spec=pltpu.PrefetchScalarGridSpec(
            num_scalar_prefetch=1, grid=(t // tile,),
            in_specs=[pl.BlockSpec(memory_space=pl.ANY), row,
                      pl.BlockSpec((tile, LANES), lambda i, p: (i, 0)),
                      pl.BlockSpec((1, 1, d), lambda i, p: (i // rpb, 0, 0))],
            out_specs=row,
            scratch_shapes=[buf, buf, pltpu.SemaphoreType.DMA(())]),
        out_shape=jax.ShapeDtypeStruct((t, d), F32),
        compiler_params=_cparams("arbitrary"), name="moe_combine")(pos, y_seg, x, route, gate)


def _ada_mod(c, ada_w, ada_b, ada_table):
    bsz, d = c.shape
    b = 16
    c = jnp.pad(c, ((0, b - bsz), (0, 0)))
    depth = ada_table.shape[0]
    n = ada_w.shape[1]
    tn = min(512, n)

    def epilogue(accs, ex, outs, ids):
        bias_ref, tab_ref = ex
        outs[0][...] = (accs[0] + bias_ref[...])[None] + tab_ref[...]

    out = _mm(
        c, pl.BlockSpec((b, d), lambda i, j: (0, 0)),
        [(ada_w, pl.BlockSpec((d, tn), lambda i, j: (0, j)))],
        [(ada_b.reshape(1, n), pl.BlockSpec((1, tn), lambda i, j: (0, j))),
         (ada_table.reshape(depth, 1, n), pl.BlockSpec((depth, 1, tn), lambda i, j: (0, 0, j)))],
        [(jax.ShapeDtypeStruct((depth, b, n), F32), pl.BlockSpec((depth, b, tn), lambda i, j: (0, 0, j)))],
        epilogue, grid=(1, n // tn), name="ada_mod", cast_w=True, a_fn=lambda v: _silu(v).astype(BF16))
    return out[0][:, :bsz].reshape(depth, bsz, N_ADA, d)


def _norm_kernel(*refs, modulate, n_experts):
    if not modulate:
        x_ref, g_ref, out_ref = refs
        out_ref[...] = _rms(x_ref[...], g_ref[...])
        return
    x_ref, g_ref, sh_ref, sc_ref = refs[:4]
    h = _rms(x_ref[...], g_ref[...]) * (1.0 + sc_ref[0]) + sh_ref[0]
    if not n_experts:
        refs[4][...] = h.astype(BF16)
        return
    r_ref, hseg_ref, route_ref, count_ref, carry_ref = refs[4:]

    @pl.when(pl.program_id(0) == 0)
    def _():
        carry_ref[...] = jnp.zeros_like(carry_ref)

    _SegLayout(h.shape[1]).store(hseg_ref, h)
    logits = jnp.dot(h, r_ref[...], preferred_element_type=F32, precision=lax.Precision.HIGHEST)
    lane = lax.broadcasted_iota(jnp.int32, logits.shape, 1)
    lg = jnp.where(lane < n_experts, logits, -jnp.inf)
    m1 = jnp.max(lg, axis=1, keepdims=True)
    i1 = jnp.min(jnp.where(lg == m1, lane, LANES), axis=1, keepdims=True)
    lg2 = jnp.where(lane == i1, -jnp.inf, lg)
    m2 = jnp.max(lg2, axis=1, keepdims=True)
    i2 = jnp.min(jnp.where(lg2 == m2, lane, LANES), axis=1, keepdims=True)
    e2 = jnp.exp(m2 - m1)
    w1 = 1.0 / (1.0 + e2)
    sel = jnp.where((lane == i1) | (lane == i2), 1.0, 0.0)
    tm = sel.shape[0]
    lower = lax.broadcasted_iota(jnp.int32, (tm, tm), 0) >= lax.broadcasted_iota(jnp.int32, (tm, tm), 1)
    cum = jnp.dot(jnp.where(lower, 1.0, 0.0).astype(BF16), sel.astype(BF16), preferred_element_type=F32)
    rank = cum - sel + carry_ref[...]
    r1 = jnp.sum(jnp.where(lane == i1, rank, 0.0), axis=1, keepdims=True)
    r2 = jnp.sum(jnp.where(lane == i2, rank, 0.0), axis=1, keepdims=True)
    carry_ref[...] += jnp.sum(sel, axis=0, keepdims=True)
    count_ref[...] = carry_ref[...]
    cols = (i1.astype(F32), i2.astype(F32), r1, r2, w1, e2 * w1)
    table = jnp.zeros_like(logits)
    for k, col in enumerate(cols):
        table = jnp.where(lane == k, col, table)
    route_ref[...] = table


ROUTE_EXPERT, ROUTE_RANK, ROUTE_WEIGHT = 0, 2, 4


def _norm_mod(x, g, shift, scale, seq, router=None, *, tm=256):
    t, d = x.shape
    tm = min(tm, seq)
    rpb = seq // tm
    row = pl.BlockSpec((tm, d), lambda i: (i, 0))
    vec = pl.BlockSpec((1, d), lambda i: (0, 0))
    per_batch = pl.BlockSpec((1, 1, d), lambda i: (i // rpb, 0, 0))
    ins = [x, g.reshape(1, d), shift, scale]
    in_specs = [row, vec, per_batch, per_batch]
    out_shape = [jax.ShapeDtypeStruct((t, d), BF16)]
    out_specs = [row]
    n_experts = 0
    scratch = []
    if router is not None:
        n_experts = router.shape[1]
        pitch = _SegLayout(d).pitch
        out_shape = [jax.ShapeDtypeStruct((t * pitch, LANES), F32)]
        out_specs = [pl.BlockSpec((tm * pitch, LANES), lambda i: (i, 0))]
        ins.append(jnp.pad(router, ((0, 0), (0, LANES - n_experts))))
        in_specs.append(pl.BlockSpec((d, LANES), lambda i: (0, 0)))
        out_shape += [jax.ShapeDtypeStruct((t, LANES), F32), jax.ShapeDtypeStruct((1, LANES), F32)]
        out_specs += [pl.BlockSpec((tm, LANES), lambda i: (i, 0)), pl.BlockSpec((1, LANES), lambda i: (0, 0))]
        scratch = [pltpu.VMEM((1, LANES), F32)]
    res = pl.pallas_call(
        functools.partial(_norm_kernel, modulate=True, n_experts=n_experts),
        grid=(t // tm,), in_specs=in_specs, out_specs=out_specs, out_shape=out_shape, scratch_shapes=scratch,
        compiler_params=_cparams("arbitrary" if router is not None else "parallel"),
        name="norm_router" if router is not None else "norm_mod")(*ins)
    return res if router is not None else res[0]


def _final_norm(x, g, *, tm=256):
    t, d = x.shape
    tm = min(tm, t)
    row = pl.BlockSpec((tm, d), lambda i: (i, 0))
    return pl.pallas_call(
        functools.partial(_norm_kernel, modulate=False, n_experts=0),
        grid=(t // tm,), in_specs=[row, pl.BlockSpec((1, d), lambda i: (0, 0))], out_specs=row,
        out_shape=jax.ShapeDtypeStruct((t, d), F32), compiler_params=_cparams("parallel"),
        name="final_norm")(x, g.reshape(1, d))


def _conv_kernel(*refs, width, ts, tc, has_bias, has_mul):
    cur_ref, halo_ref, w_ref = refs[:3]
    rest = list(refs[3:])
    b_ref = rest.pop(0) if has_bias else None
    mul_ref = rest.pop(0) if has_mul else None
    out_ref, buf = rest
    first = pl.program_id(1) == 0
    buf[0:CONV_HALO, :] = jnp.where(first, 0.0, halo_ref[0])
    buf[CONV_HALO:CONV_HALO + ts, :] = cur_ref[0]
    rows = 32
    for r in range(0, ts, rows):
        acc = None
        for k in range(width):
            term = w_ref[k:k + 1, :] * buf[pl.ds(CONV_HALO + r - (width - 1) + k, rows), :]
            acc = term if acc is None else acc + term
        if has_bias:
            acc = acc + b_ref[...]
        if has_mul:
            acc = acc * mul_ref[0, r:r + rows, :].astype(F32)
        out_ref[0, r:r + rows, :] = acc.astype(out_ref.dtype)


def _causal_conv(x, conv_w, bias=None, mul=None, out_dtype=F32, *, ts=256, tc=512):
    b, s, ch = x.shape
    width = conv_w.shape[0]
    assert width - 1 <= CONV_HALO
    if width <= 4:
        tc *= 2
    ts, tc = min(ts, s), min(tc, ch)
    hpb = ts // CONV_HALO
    blk = pl.BlockSpec((1, ts, tc), lambda bi, si, ci: (bi, si, ci))
    ins = [x, x, conv_w]
    in_specs = [blk,
                pl.BlockSpec((1, CONV_HALO, tc), lambda bi, si, ci: (bi, jnp.maximum(si * hpb - 1, 0), ci)),
                pl.BlockSpec((width, tc), lambda bi, si, ci: (0, ci))]
    if bias is not None:
        ins.append(bias.reshape(1, ch))
        in_specs.append(pl.BlockSpec((1, tc), lambda bi, si, ci: (0, ci)))
    if mul is not None:
        ins.append(mul)
        in_specs.append(blk)
    return pl.pallas_call(
        functools.partial(_conv_kernel, width=width, ts=ts, tc=tc, has_bias=bias is not None,
                          has_mul=mul is not None),
        grid=(b, s // ts, ch // tc), in_specs=in_specs, out_specs=blk,
        out_shape=jax.ShapeDtypeStruct((b, s, ch), out_dtype),
        scratch_shapes=[pltpu.VMEM((CONV_HALO + ts, tc), F32)],
        compiler_params=_cparams("parallel", "parallel", "parallel"), name=f"causal_conv{width}")(*ins)


def _ln_silu_kernel(x_ref, g_ref, b_ref, out_ref):
    x = x_ref[...]
    mu = jnp.mean(x, axis=-1, keepdims=True)
    xc = x - mu
    var = jnp.mean(xc * xc, axis=-1, keepdims=True)
    y = xc * lax.rsqrt(var + NORM_EPS) * g_ref[...] + b_ref[...]
    out_ref[...] = _silu(y).astype(BF16)


def _ln_silu(x, g, b, *, tm=256):
    t, d = x.shape
    tm = min(tm, t)
    row = pl.BlockSpec((tm, d), lambda i: (i, 0))
    vec = pl.BlockSpec((1, d), lambda i: (0, 0))
    return pl.pallas_call(
        _ln_silu_kernel, grid=(t // tm,), in_specs=[row, vec, vec], out_specs=row,
        out_shape=jax.ShapeDtypeStruct((t, d), BF16), compiler_params=_cparams("parallel"), name="ln_silu")(
            x, g.reshape(1, d), b.reshape(1, d))


def _rglru_kernel(xc_ref, gate_ref, wa_ref, wx_ref, ba_ref, bx_ref, lam_ref, out_ref,
                  a_scr, u_scr, h_scr, *, ts, n_heads, blk, cw):
    @pl.when(pl.program_id(1) == 0)
    def _():
        h_scr[...] = jnp.zeros_like(h_scr)

    neg_lam = -lam_ref[...]
    softplus = jnp.maximum(neg_lam, 0.0) + jnp.log1p(jnp.exp(-jnp.abs(neg_lam)))
    for hd in range(n_heads):
        sl = slice(hd * blk, (hd + 1) * blk)
        xh = xc_ref[:, sl]
        xb = xh.astype(BF16)
        r = _sigmoid(jnp.dot(xb, wa_ref[hd], preferred_element_type=F32) + ba_ref[:, sl])
        i_g = _sigmoid(jnp.dot(xb, wx_ref[hd], preferred_element_type=F32) + bx_ref[:, sl])
        log_a = -LRU_C * r * softplus[:, sl]
        a_scr[:, sl] = jnp.exp(log_a)
        u_scr[:, sl] = jnp.sqrt(1.0 - jnp.exp(2.0 * log_a)) * (i_g * xh)

    row = lax.broadcasted_iota(jnp.int32, (SUBLANES, cw), 0)
    width = n_heads * blk
    for c0 in range(0, width, cw):
        def body(g, h_prev, c0=c0):
            r0 = pl.multiple_of(g * SUBLANES, SUBLANES)
            a8 = a_scr[pl.ds(r0, SUBLANES), c0:c0 + cw]
            b8 = u_scr[pl.ds(r0, SUBLANES), c0:c0 + cw]
            for d in (1, 2, 4):
                keep = row >= d
                b8 = jnp.where(keep, a8 * pltpu.roll(b8, d, axis=0) + b8, b8)
                a8 = jnp.where(keep, a8 * pltpu.roll(a8, d, axis=0), a8)
            h8 = b8 + a8 * h_prev
            u_scr[pl.ds(r0, SUBLANES), c0:c0 + cw] = h8
            return h8[SUBLANES - 1:SUBLANES, :]

        h_scr[:, c0:c0 + cw] = lax.fori_loop(0, ts // SUBLANES, body, h_scr[:, c0:c0 + cw])
    out_ref[...] = (u_scr[...] * gate_ref[...].astype(F32)).astype(BF16)


def _rglru(xc, gate, w_a, w_x, b_a, b_x, lam, seq, *, ts=256):
    t, width = xc.shape
    n_heads, blk, _ = w_a.shape
    ts = min(ts, seq)
    nts = seq // ts
    row = pl.BlockSpec((ts, width), lambda b, j: (b * nts + j, 0))
    wsp = pl.BlockSpec((n_heads, blk, blk), lambda b, j: (0, 0, 0))
    vec = pl.BlockSpec((1, width), lambda b, j: (0, 0))
    return pl.pallas_call(
        functools.partial(_rglru_kernel, ts=ts, n_heads=n_heads, blk=blk, cw=min(1024, width)),
        grid=(t // seq, nts), in_specs=[row, row, wsp, wsp, vec, vec, vec], out_specs=row,
        out_shape=jax.ShapeDtypeStruct((t, width), BF16),
        scratch_shapes=[pltpu.VMEM((ts, width), F32), pltpu.VMEM((ts, width), F32), pltpu.VMEM((1, width), F32)],
        compiler_params=_cparams("parallel", "arbitrary"), name="rglru")(
            xc, gate, w_a, w_x, b_a.reshape(1, width), b_x.reshape(1, width), lam.reshape(1, width))


def _bias_tile_kernel(rb_ref, out_ref):
    hd = pl.program_id(0)
    shape = (2 * Q_BLOCK, Q_BLOCK)
    dist = Q_BLOCK + lax.broadcasted_iota(jnp.int32, shape, 1) - lax.broadcasted_iota(jnp.int32, shape, 0)
    dist = jnp.maximum(dist, 0)
    max_exact = REL_BUCKETS // 2
    large = max_exact + (jnp.log(jnp.maximum(dist, 1).astype(F32) / max_exact)
                         / math.log(REL_MAX_DIST / max_exact) * (REL_BUCKETS - max_exact)).astype(jnp.int32)
    bucket = jnp.where(dist < max_exact, dist, jnp.minimum(large, REL_BUCKETS - 1))
    tile = jnp.zeros(shape, F32)
    for bkt in range(REL_BUCKETS):
        tile = jnp.where(bucket == bkt, rb_ref[bkt, hd], tile)
    out_ref[0] = tile - rb_ref[REL_BUCKETS - 1, hd]


def _bias_tile(rel_bias):
    n_heads = rel_bias.shape[1]
    return pl.pallas_call(
        _bias_tile_kernel, grid=(n_heads,),
        in_specs=[pl.BlockSpec(memory_space=pltpu.SMEM)],
        out_specs=pl.BlockSpec((1, 2 * Q_BLOCK, Q_BLOCK), lambda h: (h, 0, 0)),
        out_shape=jax.ShapeDtypeStruct((n_heads, 2 * Q_BLOCK, Q_BLOCK), F32),
        compiler_params=_cparams("parallel"), name="dsa_bias_tile")(rel_bias)


def _dsa_kernel(qidx_ref, wrow_ref, kidx_ref, ckv_ref, ckvt_ref, qlat_ref, bias_ref, wuv_ref, out_ref,
                score_ref, key_ref, madd_ref, z_ref, acc_ref, *, n_slab, k_sel, idx_heads, kv_dim, head_dim):
    qb = pl.program_id(1)
    cols = HEAD_GROUP * Q_BLOCK
    sub = KEY_CHUNK // SUBLANES
    nt = (((1,), (1,)), ((), ()))
    n_chunks = (qb + 2) // 2

    @pl.when(pl.program_id(2) == 0)
    def _select():
        def chunk_body(c, carry):
            k0 = pl.multiple_of(c * KEY_CHUNK, KEY_CHUNK)
            keys = kidx_ref[0, pl.ds(k0, KEY_CHUNK), :]

            def head_body(hg, acc):
                h0 = pl.multiple_of(hg * HEAD_GROUP, HEAD_GROUP)
                q = qidx_ref[0, pl.ds(h0, HEAD_GROUP)].reshape(cols, LANES)
                dots = lax.dot_general(keys, q, nt, preferred_element_type=F32)
                weighted = jnp.maximum(dots, 0.0) * wrow_ref[0, hg]
                for hh in range(HEAD_GROUP):
                    acc = acc + weighted[:, hh * Q_BLOCK:(hh + 1) * Q_BLOCK]
                return acc

            score_ref[pl.ds(k0, KEY_CHUNK), :] = lax.fori_loop(
                0, idx_heads // HEAD_GROUP, head_body, jnp.zeros((KEY_CHUNK, Q_BLOCK), F32), unroll=True)
            return carry

        lax.fori_loop(0, n_chunks, chunk_body, 0)

        def zero_body(c, carry):
            score_ref[pl.ds(pl.multiple_of(c * KEY_CHUNK, KEY_CHUNK), KEY_CHUNK), :] = jnp.zeros(
                (KEY_CHUNK, Q_BLOCK), F32)
            return carry

        lax.fori_loop(n_chunks, n_slab // 2, zero_body, 0)

        shape = (n_slab * LANES, Q_BLOCK)
        bits = pltpu.bitcast(score_ref[...], jnp.int32)
        key = jnp.where(bits >= 0, bits, bits ^ jnp.int32(0x7FFFFFFF))
        causal = lax.broadcasted_iota(jnp.int32, shape, 0) <= qb * Q_BLOCK + lax.broadcasted_iota(jnp.int32, shape, 1)
        key = jnp.where(causal, key, INT_MIN)
        key_ref[...] = key

        def bit_body(it, thr):
            cand = thr + lax.shift_left(jnp.int32(1), 31 - it)
            cand_b = jnp.broadcast_to(cand, (SUBLANES, Q_BLOCK))[None, None]
            lanes_of_sum = (SUBLANES, SUBLANES, Q_BLOCK)

            def count_body(c, cnt):
                kc = key_ref[pl.ds(pl.multiple_of(c * 2 * KEY_CHUNK, 2 * KEY_CHUNK), 2 * KEY_CHUNK), :]
                hit = jnp.where(kc.reshape((2 * sub // SUBLANES,) + lanes_of_sum) >= cand_b, 1.0, 0.0)
                return cnt + jnp.sum(hit, axis=0)

            cnt = lax.fori_loop(0, (n_chunks + 1) // 2, count_body, jnp.zeros(lanes_of_sum, F32))
            cnt = jnp.sum(jnp.sum(cnt, axis=0), axis=0, keepdims=True)
            return jnp.where(cnt >= k_sel, cand, thr)

        thr = lax.fori_loop(0, 32, bit_body, jnp.full((1, Q_BLOCK), INT_MIN, jnp.int32))
        madd_ref[0:Q_BLOCK, :] = jnp.full((Q_BLOCK, Q_BLOCK), MASK_NEG, F32)
        madd_ref[Q_BLOCK:, :] = jnp.where(causal & (key >= thr), 0.0, MASK_NEG)

    q = qlat_ref[0].reshape(cols, kv_dim)
    n_far = qb // 2
    near_row = qb * Q_BLOCK

    def logits_of(kv):
        return lax.dot_general(kv, q, nt, preferred_element_type=F32)

    def per_head(m):
        return jnp.concatenate([m] * HEAD_GROUP, axis=1)

    def col_max(z):
        return jnp.max(z.reshape(sub, SUBLANES, cols), axis=0)

    def far_chunk(c, m8):
        r0 = pl.multiple_of(Q_BLOCK + c * KEY_CHUNK, Q_BLOCK)
        kv = ckv_ref[0, pl.ds(r0, KEY_CHUNK), :]
        row = r0 + lax.broadcasted_iota(jnp.int32, (KEY_CHUNK, Q_BLOCK), 0)
        madd = jnp.where(row < near_row, madd_ref[pl.ds(r0, KEY_CHUNK), :], MASK_NEG)
        z = logits_of(kv) + per_head(madd)
        z_ref[c] = z
        return jnp.maximum(m8, col_max(z))

    def in_pairs(n, one, init):
        quad = lambda g, v: one(4 * g + 3, one(4 * g + 2, one(4 * g + 1, one(4 * g, v))))
        carry = lax.fori_loop(0, n // 4, quad, init)
        base = n // 4 * 4
        carry = lax.cond(n % 4 >= 2, lambda v: one(base + 1, one(base, v)), lambda v: v, carry)
        return lax.cond(n % 2 == 1, lambda v: one(n - 1, v), lambda v: v, carry)

    m8 = in_pairs(n_far, far_chunk, jnp.full((SUBLANES, cols), MASK_NEG, F32))
    r0 = pl.multiple_of(near_row, Q_BLOCK)
    bias = jnp.concatenate([bias_ref[hh] for hh in range(HEAD_GROUP)], axis=1)
    z = (logits_of(ckv_ref[0, pl.ds(r0, KEY_CHUNK), :]) + per_head(madd_ref[pl.ds(r0, KEY_CHUNK), :])) + bias
    z_ref[n_far] = z
    m8 = jnp.maximum(m8, col_max(z))
    m8 = jnp.broadcast_to(jnp.max(m8, axis=0, keepdims=True), (SUBLANES, cols))[None]
    acc_ref[...] = jnp.zeros(acc_ref.shape, F32)

    def weigh(c, l8):
        s0 = jnp.where(c < n_far, 2 * c + 1, qb)
        kvt = jnp.concatenate([ckvt_ref[0, s0], ckvt_ref[0, s0 + 1]], axis=1)
        p = jnp.exp(z_ref[c].reshape(sub, SUBLANES, cols) - m8)
        acc_ref[...] += jnp.dot(kvt, p.reshape(KEY_CHUNK, cols).astype(BF16), preferred_element_type=F32)
        return l8 + jnp.sum(p, axis=0)

    l8 = in_pairs(n_far + 1, weigh, jnp.zeros((SUBLANES, cols), F32))
    inv = 1.0 / jnp.broadcast_to(jnp.sum(l8, axis=0, keepdims=True), (SUBLANES, cols))
    o_lat = (acc_ref[...].reshape(kv_dim // SUBLANES, SUBLANES, cols) * inv[None]).reshape(kv_dim, cols)
    o_lat = o_lat.astype(BF16)
    for hh in range(HEAD_GROUP):
        o_h = jnp.dot(wuv_ref[hh], o_lat[:, hh * Q_BLOCK:(hh + 1) * Q_BLOCK], preferred_element_type=F32)
        out_ref[:, hh * head_dim:(hh + 1) * head_dim] = o_h.T.astype(BF16)


def _dsa_core(qidx, wrow, kidx, ckv_pad, ckvt_pad, qlat, bias, w_uvt, *, bsz, seq, k_sel):
    n_qb = seq // Q_BLOCK
    idx_heads = qidx.shape[1]
    n_heads, kv_dim = qlat.shape[1], qlat.shape[3]
    head_dim = w_uvt.shape[1]
    n_groups = n_heads // HEAD_GROUP
    cols = HEAD_GROUP * Q_BLOCK
    blk_q = lambda b, i, g: (b * n_qb + i, 0, 0, 0)
    per_batch = lambda b, i, g: (b, 0, 0)
    return pl.pallas_call(
        functools.partial(_dsa_kernel, n_slab=n_qb, k_sel=k_sel, idx_heads=idx_heads, kv_dim=kv_dim,
                          head_dim=head_dim),
        grid=(bsz, n_qb, n_groups),
        in_specs=[
            pl.BlockSpec((1, idx_heads, Q_BLOCK, LANES), blk_q),
            pl.BlockSpec((1, idx_heads // HEAD_GROUP, 1, cols), blk_q),
            pl.BlockSpec((1, seq, LANES), per_batch),
            pl.BlockSpec((1, seq + Q_BLOCK, kv_dim), per_batch),
            pl.BlockSpec((1, n_qb + 1, kv_dim, Q_BLOCK), lambda b, i, g: (b, 0, 0, 0)),
            pl.BlockSpec((1, HEAD_GROUP, Q_BLOCK, kv_dim), lambda b, i, g: (b * n_qb + i, g, 0, 0)),
            pl.BlockSpec((HEAD_GROUP, KEY_CHUNK, Q_BLOCK), lambda b, i, g: (g, 0, 0)),
            pl.BlockSpec((HEAD_GROUP, head_dim, kv_dim), lambda b, i, g: (g, 0, 0)),
        ],
        out_specs=pl.BlockSpec((Q_BLOCK, HEAD_GROUP * head_dim), lambda b, i, g: (b * n_qb + i, g)),
        out_shape=jax.ShapeDtypeStruct((bsz * seq, n_heads * head_dim), BF16),
        scratch_shapes=[
            pltpu.VMEM((seq, Q_BLOCK), F32),
            pltpu.VMEM((seq, Q_BLOCK), jnp.int32),
            pltpu.VMEM((seq + Q_BLOCK, Q_BLOCK), F32),
            pltpu.VMEM((n_qb // 2 + 1, KEY_CHUNK, cols), F32),
            pltpu.VMEM((kv_dim, cols), F32),
        ],
        compiler_params=_cparams("parallel", "parallel", "arbitrary"), name="dsa_core")(
            qidx, wrow, kidx, ckv_pad, ckvt_pad, qlat, bias, w_uvt)


def _dsa_attention(h, x, gate, rel_bias, w_in, g_cq, g_ckv, w_uq, w_uk, w_uv, w_qidx, w_out, *, bsz, seq):
    t, d = h.shape
    q_lora, kv_lora = g_cq.shape[0], g_ckv.shape[0]
    n_heads, head_dim = w_uk.shape[1], w_uk.shape[2]
    idx_dim = LANES
    idx_heads = w_qidx.shape[1] // idx_dim
    k_sel = min(TOPK_MAX, seq // 4)
    n_qb = seq // Q_BLOCK
    tm = min(512, seq)
    qpt = tm // Q_BLOCK

    w_in = w_in.astype(BF16)
    splits = (0, q_lora, q_lora + kv_lora, q_lora + kv_lora + idx_dim, w_in.shape[1])
    w_parts = [w_in[:, splits[i]:splits[i + 1]] for i in range(4)]
    head_scale = idx_heads ** -0.5 * idx_dim ** -0.5

    def in_epilogue(accs, ex, outs, ids):
        outs[0][...] = _rms(accs[0], ex[0][...]).astype(BF16)
        outs[1][...] = _rms(accs[1], ex[1][...]).astype(BF16)
        outs[2][...] = accs[2].astype(BF16)
        outs[3][...] = accs[3] * head_scale

    full = lambda n: pl.BlockSpec((d, n), lambda i, j: (0, 0))
    rowsp = lambda n: pl.BlockSpec((tm, n), lambda i, j: (i, 0))
    vecsp = lambda n: pl.BlockSpec((1, n), lambda i, j: (0, 0))
    widths = [w.shape[1] for w in w_parts]
    c_q, c_kv, k_idx, w_head = _mm(
        h, pl.BlockSpec((tm, d), lambda i, j: (i, 0)),
        [(w, full(n)) for w, n in zip(w_parts, widths)],
        [(g_cq.reshape(1, q_lora), vecsp(q_lora)), (g_ckv.reshape(1, kv_lora), vecsp(kv_lora))],
        [(jax.ShapeDtypeStruct((t, n), dt), rowsp(n)) for n, dt in zip(widths, (BF16, BF16, BF16, F32))],
        in_epilogue, grid=(t // tm, 1), name="dsa_in_proj")

    hpt = 4
    tn = hpt * idx_dim
    tm_q = min(2048, seq)
    qpt_q = tm_q // Q_BLOCK

    def qidx_epilogue(accs, ex, outs, ids):
        for hh in range(hpt):
            q_h = accs[0][:, hh * idx_dim:(hh + 1) * idx_dim]
            outs[0][:, hh] = q_h.reshape(qpt_q, Q_BLOCK, idx_dim).astype(BF16)

    qidx = _mm(
        c_q, pl.BlockSpec((tm_q, q_lora), lambda i, j: (i, 0)),
        [(w_qidx, pl.BlockSpec((q_lora, tn), lambda i, j: (0, j)))], [],
        [(jax.ShapeDtypeStruct((t // Q_BLOCK, idx_heads, Q_BLOCK, idx_dim), BF16),
          pl.BlockSpec((qpt_q, hpt, Q_BLOCK, idx_dim), lambda i, j: (i, j, 0, 0)))],
        qidx_epilogue, grid=(t // tm_q, idx_heads // hpt), name="dsa_qidx", cast_w=True)[0]

    w_ukt = jnp.transpose(w_uk, (1, 2, 0)).astype(BF16)
    logit_scale = head_dim ** -0.5

    def qlat_kernel(cq_ref, wuq_ref, wuk_ref, out_ref):
        qg = jnp.dot(cq_ref[...], wuq_ref[...], preferred_element_type=F32).astype(BF16)
        for hh in range(HEAD_GROUP):
            ql = jnp.dot(qg[:, hh * head_dim:(hh + 1) * head_dim], wuk_ref[hh], preferred_element_type=F32)
            out_ref[:, hh] = (ql * logit_scale).reshape(qpt, Q_BLOCK, kv_lora).astype(BF16)

    qlat = pl.pallas_call(
        qlat_kernel, grid=(t // tm, n_heads // HEAD_GROUP),
        in_specs=[pl.BlockSpec((tm, q_lora), lambda i, g: (i, 0)),
                  pl.BlockSpec((q_lora, HEAD_GROUP * head_dim), lambda i, g: (0, g)),
                  pl.BlockSpec((HEAD_GROUP, head_dim, kv_lora), lambda i, g: (g, 0, 0))],
        out_specs=pl.BlockSpec((qpt, HEAD_GROUP, Q_BLOCK, kv_lora), lambda i, g: (i, g, 0, 0)),
        out_shape=jax.ShapeDtypeStruct((t // Q_BLOCK, n_heads, Q_BLOCK, kv_lora), BF16),
        compiler_params=_cparams("parallel", "parallel"), name="dsa_qlat")(c_q, w_uq.astype(BF16), w_ukt)

    wrow = w_head.reshape(bsz * n_qb, Q_BLOCK, idx_heads // HEAD_GROUP, HEAD_GROUP).transpose(0, 2, 3, 1)
    wrow = wrow.reshape(bsz * n_qb, idx_heads // HEAD_GROUP, 1, HEAD_GROUP * Q_BLOCK)
    ckv_pad = jnp.pad(c_kv.reshape(bsz, seq, kv_lora), ((0, 0), (Q_BLOCK, 0), (0, 0)))
    ckvt_pad = ckv_pad.reshape(bsz, n_qb + 1, Q_BLOCK, kv_lora).transpose(0, 1, 3, 2)
    w_uvt = jnp.transpose(w_uv, (1, 2, 0)).astype(BF16)
    o = _dsa_core(qidx, wrow, k_idx.reshape(bsz, seq, idx_dim), ckv_pad, ckvt_pad, qlat,
                  _bias_tile(rel_bias.astype(F32)), w_uvt, bsz=bsz, seq=seq, k_sel=k_sel)
    return _matmul_resid(o, w_out, x, gate, seq, name="dsa_out_proj")


def _split_proj(h, w, n_parts, epilogue, out_dtypes, *, name, tm=512, tn=512):
    t, d = h.shape
    n = w.shape[1] // n_parts
    tm, tn = min(tm, t), min(tn, n)
    nb = n // tn
    out_sp = pl.BlockSpec((tm, tn), lambda i, j: (i, j))
    return _mm(
        h, pl.BlockSpec((tm, d), lambda i, j: (i, 0)),
        [(w, pl.BlockSpec((d, tn), lambda i, j, p=p: (0, j + p * nb))) for p in range(n_parts)], [],
        [(jax.ShapeDtypeStruct((t, n), dt), out_sp) for dt in out_dtypes],
        epilogue, grid=(t // tm, nb), name=name, cast_w=True)


def _short_conv_mixer(h, x, gate, w_in, conv_w, w_out, *, bsz, seq):
    def epilogue(accs, ex, outs, ids):
        outs[0][...] = accs[0].astype(BF16)
        outs[1][...] = accs[1] * accs[2]

    gate_b, cx = _split_proj(h, w_in, 3, epilogue, (BF16, F32), name="sconv_in_proj", tn=256)
    ch = cx.shape[1]
    y = _causal_conv(cx.reshape(bsz, seq, ch), conv_w, mul=gate_b.reshape(bsz, seq, ch), out_dtype=BF16)
    return _matmul_resid(y.reshape(bsz * seq, ch), w_out, x, gate, seq, name="sconv_out_proj")


def _conformer_mixer(h, x, gate, w_in, conv_w, conv_b, ln_g, ln_b, w_out, *, bsz, seq):
    def epilogue(accs, ex, outs, ids):
        outs[0][...] = accs[0] * _sigmoid(accs[1])

    u = _split_proj(h, w_in, 2, epilogue, (F32,), name="conf_in_proj")[0]
    ch = u.shape[1]
    u = _causal_conv(u.reshape(bsz, seq, ch), conv_w, bias=conv_b)
    u = _ln_silu(u.reshape(bsz * seq, ch), ln_g, ln_b)
    return _matmul_resid(u, w_out, x, gate, seq, name="conf_out_proj")


def _rglru_mixer(h, x, gate, w_in, conv_w, conv_b, w_a, b_a, w_x, b_x, lam, w_out, *, bsz, seq):
    def epilogue(accs, ex, outs, ids):
        outs[0][...] = _gelu_tanh(accs[0]).astype(BF16)
        outs[1][...] = accs[1]

    gate_br, x_br = _split_proj(h, w_in, 2, epilogue, (BF16, F32), name="lru_in_proj")
    width = x_br.shape[1]
    xc = _causal_conv(x_br.reshape(bsz, seq, width), conv_w, bias=conv_b)
    y = _rglru(xc.reshape(bsz * seq, width), gate_br, w_a.astype(BF16), w_x.astype(BF16), b_a, b_x, lam, seq)
    return _matmul_resid(y, w_out, x, gate, seq, name="lru_out_proj")


def _dense_ffn(h, x, gate, w13, w2, layer, *, seq):
    act = _swiglu_act(h, w13, layer)
    return _matmul_resid(act, w2, x, gate, seq, name="ffn_down", layer=layer, tm=512, tk=w2.shape[1])


def _moe_ffn(h_seg, route, counts, x, gate, w13, w2, layer, *, seq):
    t = x.shape[0]
    n_e = w13.shape[1]
    tile = MOE_ROW_TILE
    n_rows = MOE_TOPK * t + n_e * tile
    counts = counts[0, :n_e].astype(jnp.int32)
    padded = (counts + tile - 1) // tile * tile
    ends = jnp.cumsum(padded)
    starts = ends - padded
    experts = route[:, ROUTE_EXPERT:ROUTE_EXPERT + MOE_TOPK].astype(jnp.int32)
    ranks = route[:, ROUTE_RANK:ROUTE_RANK + MOE_TOPK].astype(jnp.int32)
    pos = (starts[experts] + ranks).T.reshape(MOE_TOPK * t)
    tile_start = jnp.arange(n_rows // tile, dtype=jnp.int32) * tile
    tile_expert = jnp.minimum(jnp.sum(tile_start[:, None] >= ends[None, :], axis=1), n_e - 1).astype(jnp.int32)
    n_used = (ends[-1:] // tile).astype(jnp.int32)

    lay = _SegLayout(x.shape[1])
    hs = _moe_dispatch(h_seg, pos, n_rows, lay)
    act = _grouped_swiglu(hs, w13, layer, tile_expert, n_used, lay)
    y = _grouped_down(act, w2, layer, tile_expert, n_used, lay)
    return _moe_combine(y, pos, x, route, gate, seq, lay)


def kernel(x, c, ada_w, ada_b, ada_table, norm_mix, norm_ffn, norm_final, rel_bias, att_w_in, att_g_cq, att_g_ckv, att_w_uq, att_w_uk, att_w_uv, att_w_qidx, att_w_out, sconv_w_in, sconv_conv_w, sconv_w_out, conf_w_in, conf_conv_w, conf_conv_b, conf_ln_g, conf_ln_b, conf_w_out, lru_w_in, lru_conv_w, lru_conv_b, lru_w_a, lru_b_a, lru_w_x, lru_b_x, lru_lambda, lru_w_out, ffn_w13, ffn_w2, moe_router, moe_w13, moe_w2):
    bsz, seq, d = x.shape
    depth = ada_table.shape[0]
    xf = x.reshape(bsz * seq, d)
    mods = _ada_mod(c, ada_w, ada_b, ada_table)
    ffn_w2_b, moe_w13_b, moe_w2_b = ffn_w2.astype(BF16), moe_w13.astype(BF16), moe_w2.astype(BF16)
    for i in range(depth):
        shift_m, scale_m, gate_m, shift_f, scale_f, gate_f = (mods[i][:, k:k + 1, :] for k in range(N_ADA))
        h = _norm_mod(xf, norm_mix[i], shift_m, scale_m, seq)
        kind, j = i % 4, i // 4
        if kind == 0:
            xf = _dsa_attention(h, xf, gate_m, rel_bias, att_w_in[j], att_g_cq[j], att_g_ckv[j], att_w_uq[j],
                                att_w_uk[j], att_w_uv[j], att_w_qidx[j], att_w_out[j], bsz=bsz, seq=seq)
        elif kind == 1:
            xf = _short_conv_mixer(h, xf, gate_m, sconv_w_in[j], sconv_conv_w[j], sconv_w_out[j], bsz=bsz, seq=seq)
        elif kind == 2:
            xf = _conformer_mixer(h, xf, gate_m, conf_w_in[j], conf_conv_w[j], conf_conv_b[j], conf_ln_g[j],
                                  conf_ln_b[j], conf_w_out[j], bsz=bsz, seq=seq)
        else:
            xf = _rglru_mixer(h, xf, gate_m, lru_w_in[j], lru_conv_w[j], lru_conv_b[j], lru_w_a[j], lru_b_a[j],
                              lru_w_x[j], lru_b_x[j], lru_lambda[j], lru_w_out[j], bsz=bsz, seq=seq)
        if i % 2 == 0:
            h = _norm_mod(xf, norm_ffn[i], shift_f, scale_f, seq)
            xf = _dense_ffn(h, xf, gate_f, ffn_w13, ffn_w2_b, i // 2, seq=seq)
        else:
            h, route, counts = _norm_mod(xf, norm_ffn[i], shift_f, scale_f, seq, router=moe_router[i // 2])
            xf = _moe_ffn(h, route, counts, xf, gate_f, moe_w13_b, moe_w2_b, i // 2, seq=seq)
    return _final_norm(xf, norm_final).reshape(bsz, seq, d)
```

```python
import functools
import math

import jax
import jax.numpy as jnp
from jax import lax
from jax.experimental import pallas as pl
from jax.experimental.pallas import tpu as pltpu

F32 = jnp.float32
BF16 = jnp.bfloat16

NORM_EPS = 1e-6
N_ADA = 6
MOE_TOPK = 2
LRU_C = 8.0
REL_BUCKETS = 32
REL_MAX_DIST = 128
TOPK_MAX = 256

LANES = 128
SUBLANES = 8
Q_BLOCK = 128
KEY_CHUNK = 256
HEAD_GROUP = 8
CONV_HALO = 32
VMEM_LIMIT = 56 * 1024 * 1024
MASK_NEG = -1e30
INT_MIN = -2 ** 31


def _cparams(*sem):
    return pltpu.CompilerParams(dimension_semantics=sem, vmem_limit_bytes=VMEM_LIMIT)


def _sigmoid(x):
    return 1.0 / (1.0 + jnp.exp(-x))


def _silu(x):
    return x * _sigmoid(x)


def _gelu_tanh(x):
    return 0.5 * x * (1.0 + jnp.tanh(math.sqrt(2.0 / math.pi) * (x + 0.044715 * (x * x * x))))


def _rms(x, g):
    return x * lax.rsqrt(jnp.mean(x * x, axis=-1, keepdims=True) + NORM_EPS) * g


def _mm_kernel(*refs, n_w, n_ex, n_out, nk, cast_w, epilogue, a_fn):
    a_ref = refs[0]
    w_refs = refs[1:1 + n_w]
    ex_refs = refs[1 + n_w:1 + n_w + n_ex]
    out_refs = refs[1 + n_w + n_ex:1 + n_w + n_ex + n_out]
    scr_refs = refs[1 + n_w + n_ex + n_out:]
    a = a_fn(a_ref[...])
    if cast_w:
        i, j = pl.program_id(1), pl.program_id(0)

        @pl.when(i == 0)
        def _():
            for scr, w in zip(scr_refs, w_refs):
                scr[...] = w[...].astype(BF16)

        accs = [jnp.dot(a, scr[...], preferred_element_type=F32) for scr in scr_refs]
        epilogue(accs, ex_refs, out_refs, (i, j))
        return
    ids = (pl.program_id(0), pl.program_id(1))
    if nk == 1:
        accs = [jnp.dot(a, w[...], preferred_element_type=F32) for w in w_refs]
        epilogue(accs, ex_refs, out_refs, ids)
        return
    k = pl.program_id(2)

    @pl.when(k == 0)
    def _():
        for acc in scr_refs:
            acc[...] = jnp.zeros_like(acc)

    for acc, w in zip(scr_refs, w_refs):
        acc[...] += jnp.dot(a, w[...], preferred_element_type=F32)

    @pl.when(k == nk - 1)
    def _():
        epilogue([acc[...] for acc in scr_refs], ex_refs, out_refs, ids)


def _identity(v):
    return v


def _mm(a, a_spec, ws, extras, outs, epilogue, *, grid, name, acc_shapes=(), cast_w=False, a_fn=_identity):
    nk = grid[2] if len(grid) == 3 else 1
    in_specs = [a_spec] + [s for _, s in ws] + [s for _, s in extras]
    out_specs = [s for _, s in outs]
    if cast_w:
        assert nk == 1
        swap = lambda s: pl.BlockSpec(s.block_shape, lambda j, i, f=s.index_map: f(i, j))
        in_specs, out_specs = [swap(s) for s in in_specs], [swap(s) for s in out_specs]
        grid, sem = (grid[1], grid[0]), ("parallel", "arbitrary")
        scratch = [pltpu.VMEM(tuple(d for d in s.block_shape if d is not None), BF16) for _, s in ws]
    else:
        sem = ("parallel", "parallel") + (("arbitrary",) if nk > 1 else ())
        scratch = [pltpu.VMEM(s, F32) for s in acc_shapes] if nk > 1 else []
    kern = functools.partial(_mm_kernel, n_w=len(ws), n_ex=len(extras), n_out=len(outs), nk=nk, cast_w=cast_w,
                             epilogue=epilogue, a_fn=a_fn)
    return pl.pallas_call(
        kern, grid=grid, in_specs=in_specs, out_specs=out_specs, out_shape=[o for o, _ in outs],
        scratch_shapes=scratch, compiler_params=_cparams(*sem), name=name,
    )(a, *[w for w, _ in ws], *[e for e, _ in extras])


def _ep_resid(accs, ex, outs, ids):
    x_ref, gate_ref = ex
    outs[0][...] = x_ref[...] + gate_ref[0] * accs[0]


def _matmul_resid(a, w, x, gate, seq, *, name, layer=None, tm=1024, tn=512, tk=4096):
    t, kdim = a.shape
    n = w.shape[-1]
    if layer is None:
        w_spec = lambda tk, tn: pl.BlockSpec((tk, tn), lambda i, j, *k: (k[0] if k else 0, j))
    else:
        w_spec = lambda tk, tn: pl.BlockSpec((None, tk, tn), lambda i, j, *k: (layer, k[0] if k else 0, j))
    tm, tn, tk = min(tm, seq), min(tn, n), min(tk, kdim)
    nk = kdim // tk
    grid = (t // tm, n // tn) + ((nk,) if nk > 1 else ())
    rows_per_batch = seq // tm
    out = _mm(
        a, pl.BlockSpec((tm, tk), lambda i, j, *k: (i, k[0] if k else 0)),
        [(w, w_spec(tk, tn))],
        [(x, pl.BlockSpec((tm, tn), lambda i, j, *k: (i, j))),
         (gate, pl.BlockSpec((1, 1, tn), lambda i, j, *k: (i // rows_per_batch, 0, j)))],
        [(jax.ShapeDtypeStruct((t, n), F32), pl.BlockSpec((tm, tn), lambda i, j, *k: (i, j)))],
        _ep_resid, grid=grid, acc_shapes=[(tm, tn)], name=name, cast_w=w.dtype == F32)
    return out[0]


def _swiglu_act(h, w13, layer, *, tm=512, tn=512):
    t, d = h.shape
    f = w13.shape[2] // 2
    tm, tn = min(tm, t), min(tn, f)
    nf = f // tn

    def epilogue(accs, ex, outs, ids):
        outs[0][...] = (_silu(accs[0]) * accs[1]).astype(BF16)

    out = _mm(
        h, pl.BlockSpec((tm, d), lambda i, j: (i, 0)),
        [(w13, pl.BlockSpec((None, d, tn), lambda i, j: (layer, 0, j))),
         (w13, pl.BlockSpec((None, d, tn), lambda i, j: (layer, 0, j + nf)))],
        [],
        [(jax.ShapeDtypeStruct((t, f), BF16), pl.BlockSpec((tm, tn), lambda i, j: (i, j)))],
        epilogue, grid=(t // tm, nf), name="ffn_swiglu", cast_w=True)
    return out[0]


MOE_ROW_TILE = 512
MOE_DMA_TILE = 256
SEG_PAD = SUBLANES


class _SegLayout:
    def __init__(self, d):
        self.n_seg = d // LANES
        self.slab = -(-self.n_seg // SUBLANES) * SUBLANES
        self.pitch = self.slab + SEG_PAD

    def seg(self, s, n_rows):
        return pl.ds(s, n_rows, stride=self.pitch)

    def slab_of(self, row):
        return pl.ds(pl.multiple_of(row * self.pitch, SUBLANES), self.slab)

    def store(self, ref, value):
        n_rows = value.shape[0]
        for s in range(self.n_seg):
            ref[self.seg(s, n_rows), :] = value[:, s * LANES:(s + 1) * LANES]
        for s in range(self.n_seg, self.pitch):
            ref[self.seg(s, n_rows), :] = jnp.zeros((n_rows, LANES), ref.dtype)

    def row_copy(self, src, src_row, dst, dst_row, sem):
        return pltpu.make_async_copy(src.at[self.slab_of(src_row)], dst.at[self.slab_of(dst_row)], sem)


def _dispatch_kernel(pos_ref, h_ref, zero_hbm, out_hbm, sem, *, n_tok, tile, lay):
    del zero_hbm
    base = pl.program_id(0) * tile

    def copies(i):
        return [lay.row_copy(h_ref, i, out_hbm, pos_ref[k * n_tok + base + i], sem) for k in range(MOE_TOPK)]

    def start(i, carry):
        for k, cp in enumerate(copies(i)):
            cp.start(priority=k)
        return carry

    def wait(i, carry):
        for cp in copies(i):
            cp.wait()
        return carry

    lax.fori_loop(0, tile, start, 0)
    lax.fori_loop(0, tile, wait, 0)


def _moe_dispatch(h_seg, pos, n_rows, lay):
    t = h_seg.shape[0] // lay.pitch
    tile = min(MOE_DMA_TILE, t)
    any_spec = pl.BlockSpec(memory_space=pl.ANY)
    return pl.pallas_call(
        functools.partial(_dispatch_kernel, n_tok=t, tile=tile, lay=lay),
        grid_spec=pltpu.PrefetchScalarGridSpec(
            num_scalar_prefetch=1, grid=(t // tile,),
            in_specs=[pl.BlockSpec((tile * lay.pitch, LANES), lambda i, p: (i, 0)), any_spec], out_specs=any_spec,
            scratch_shapes=[pltpu.SemaphoreType.DMA(())]),
        out_shape=jax.ShapeDtypeStruct((n_rows * lay.pitch, LANES), h_seg.dtype),
        input_output_aliases={2: 0},
        compiler_params=_cparams("arbitrary"), name="moe_dispatch")(
            pos, h_seg, jnp.zeros((n_rows * lay.pitch, LANES), h_seg.dtype))


def _grouped_swiglu_kernel(te_ref, nu_ref, a_ref, wg_ref, wu_ref, out_ref, a_scr, *, tm, lay):
    active = pl.program_id(1) < nu_ref[0]

    @pl.when(active)
    def _():
        for s in range(lay.n_seg):
            a_scr[:, s * LANES:(s + 1) * LANES] = a_ref[lay.seg(s, tm), :].astype(BF16)
        a = a_scr[...]
        g = jnp.dot(a, wg_ref[...], preferred_element_type=F32)
        u = jnp.dot(a, wu_ref[...], preferred_element_type=F32)
        out_ref[...] = (_silu(g) * u).astype(BF16)

    @pl.when(jnp.logical_not(active))
    def _():
        out_ref[...] = jnp.zeros_like(out_ref)


def _grouped_swiglu(a_seg, w13, layer, tile_expert, n_used, lay, *, tn=512):
    rows, kdim = a_seg.shape[0] // lay.pitch, w13.shape[2]
    f = w13.shape[3] // 2
    tm, tn = MOE_ROW_TILE, min(tn, f)
    nb = f // tn
    return pl.pallas_call(
        functools.partial(_grouped_swiglu_kernel, tm=tm, lay=lay),
        grid_spec=pltpu.PrefetchScalarGridSpec(
            num_scalar_prefetch=2, grid=(nb, rows // tm),
            in_specs=[pl.BlockSpec((tm * lay.pitch, LANES), lambda j, r, te, nu: (r, 0)),
                      pl.BlockSpec((None, None, kdim, tn), lambda j, r, te, nu: (layer, te[r], 0, j)),
                      pl.BlockSpec((None, None, kdim, tn), lambda j, r, te, nu: (layer, te[r], 0, j + nb))],
            out_specs=pl.BlockSpec((tm, tn), lambda j, r, te, nu: (r, j)),
            scratch_shapes=[pltpu.VMEM((tm, kdim), BF16)]),
        out_shape=jax.ShapeDtypeStruct((rows, f), BF16),
        compiler_params=_cparams("parallel", "parallel"), name="moe_swiglu")(
            tile_expert, n_used, a_seg, w13, w13)


def _grouped_down_kernel(te_ref, nu_ref, a_ref, w_ref, out_ref, acc_ref, *, nk, lay):
    active = pl.program_id(0) < nu_ref[0]
    k = pl.program_id(1)

    @pl.when(active & (k == 0))
    def _():
        acc_ref[...] = jnp.zeros_like(acc_ref)

    @pl.when(active)
    def _():
        acc_ref[...] += jnp.dot(a_ref[...], w_ref[...], preferred_element_type=F32)

    @pl.when(active & (k == nk - 1))
    def _():
        lay.store(out_ref, acc_ref[...])

    @pl.when(jnp.logical_not(active) & (k == nk - 1))
    def _():
        out_ref[...] = jnp.zeros_like(out_ref)


def _grouped_down(a, w2, layer, tile_expert, n_used, lay, *, tk=1024):
    rows, kdim = a.shape
    d = w2.shape[3]
    tm, tk = MOE_ROW_TILE, min(tk, kdim)
    nk = kdim // tk
    return pl.pallas_call(
        functools.partial(_grouped_down_kernel, nk=nk, lay=lay),
        grid_spec=pltpu.PrefetchScalarGridSpec(
            num_scalar_prefetch=2, grid=(rows // tm, nk),
            in_specs=[pl.BlockSpec((tm, tk), lambda r, k, te, nu: (r, k)),
                      pl.BlockSpec((None, None, tk, d), lambda r, k, te, nu: (layer, te[r], k, 0))],
            out_specs=pl.BlockSpec((tm * lay.pitch, LANES), lambda r, k, te, nu: (r, 0)),
            scratch_shapes=[pltpu.VMEM((tm, d), F32)]),
        out_shape=jax.ShapeDtypeStruct((rows * lay.pitch, LANES), F32),
        compiler_params=_cparams("parallel", "arbitrary"), name="moe_down")(tile_expert, n_used, a, w2)


def _combine_kernel(pos_ref, y_hbm, x_ref, route_ref, gate_ref, out_ref, buf0, buf1, sem, *, n_tok, tile, lay):
    base = pl.program_id(0) * tile
    bufs = (buf0, buf1)

    def copies(i):
        return [lay.row_copy(y_hbm, pos_ref[k * n_tok + base + i], bufs[k], i, sem) for k in range(MOE_TOPK)]

    def start(i, carry):
        for k, cp in enumerate(copies(i)):
            cp.start(priority=k)
        return carry

    def wait(i, carry):
        for cp in copies(i):
            cp.wait()
        return carry

    lax.fori_loop(0, tile, start, 0)
    lax.fori_loop(0, tile, wait, 0)
    route = route_ref[...]
    w0 = route[:, ROUTE_WEIGHT:ROUTE_WEIGHT + 1]
    w1 = route[:, ROUTE_WEIGHT + 1:ROUTE_WEIGHT + 2]
    for s in range(lay.n_seg):
        cols = slice(s * LANES, (s + 1) * LANES)
        y = w0 * buf0[lay.seg(s, tile), :] + w1 * buf1[lay.seg(s, tile), :]
        out_ref[:, cols] = x_ref[:, cols] + gate_ref[0][:, cols] * y


def _moe_combine(y_seg, pos, x, route, gate, seq, lay):
    t, d = x.shape
    tile = min(MOE_DMA_TILE, seq)
    rpb = seq // tile
    row = pl.BlockSpec((tile, d), lambda i, p: (i, 0))
    buf = pltpu.VMEM((tile * lay.pitch, LANES), F32)
    return pl.pallas_call(
        functools.partial(_combine_kernel, n_tok=t, tile=tile, lay=lay),
        grid_spec=pltpu.PrefetchScalarGridSpec(
            num_scalar_prefetch=1, grid=(t // tile,),
            in_specs=[pl.BlockSpec(memory_space=pl.ANY), row,
                      pl.BlockSpec((tile, LANES), lambda i, p: (i, 0)),
                      pl.BlockSpec((1, 1, d), lambda i, p: (i // rpb, 0, 0))],
            out_specs=row,
            scratch_shapes=[buf, buf, pltpu.SemaphoreType.DMA(())]),
        out_shape=jax.ShapeDtypeStruct((t, d), F32),
        compiler_params=_cparams("arbitrary"), name="moe_combine")(pos, y_seg, x, route, gate)


def _ada_mod(c, ada_w, ada_b, ada_table):
    bsz, d = c.shape
    b = 16
    c = jnp.pad(c, ((0, b - bsz), (0, 0)))
    depth = ada_table.shape[0]
    n = ada_w.shape[1]
    tn = min(512, n)

    def epilogue(accs, ex, outs, ids):
        bias_ref, tab_ref = ex
        outs[0][...] = (accs[0] + bias_ref[...])[None] + tab_ref[...]

    out = _mm(
        c, pl.BlockSpec((b, d), lambda i, j: (0, 0)),
        [(ada_w, pl.BlockSpec((d, tn), lambda i, j: (0, j)))],
        [(ada_b.reshape(1, n), pl.BlockSpec((1, tn), lambda i, j: (0, j))),
         (ada_table.reshape(depth, 1, n), pl.BlockSpec((depth, 1, tn), lambda i, j: (0, 0, j)))],
        [(jax.ShapeDtypeStruct((depth, b, n), F32), pl.BlockSpec((depth, b, tn), lambda i, j: (0, 0, j)))],
        epilogue, grid=(1, n // tn), name="ada_mod", cast_w=True, a_fn=lambda v: _silu(v).astype(BF16))
    return out[0][:, :bsz].reshape(depth, bsz, N_ADA, d)


def _norm_kernel(*refs, modulate, n_experts):
    if not modulate:
        x_ref, g_ref, out_ref = refs
        out_ref[...] = _rms(x_ref[...], g_ref[...])
        return
    x_ref, g_ref, sh_ref, sc_ref = refs[:4]
    h = _rms(x_ref[...], g_ref[...]) * (1.0 + sc_ref[0]) + sh_ref[0]
    if not n_experts:
        refs[4][...] = h.astype(BF16)
        return
    r_ref, hseg_ref, route_ref, count_ref, carry_ref = refs[4:]

    @pl.when(pl.program_id(0) == 0)
    def _():
        carry_ref[...] = jnp.zeros_like(carry_ref)

    _SegLayout(h.shape[1]).store(hseg_ref, h)
    logits = jnp.dot(h, r_ref[...], preferred_element_type=F32, precision=lax.Precision.HIGHEST)
    lane = lax.broadcasted_iota(jnp.int32, logits.shape, 1)
    lg = jnp.where(lane < n_experts, logits, -jnp.inf)
    m1 = jnp.max(lg, axis=1, keepdims=True)
    i1 = jnp.min(jnp.where(lg == m1, lane, LANES), axis=1, keepdims=True)
    lg2 = jnp.where(lane == i1, -jnp.inf, lg)
    m2 = jnp.max(lg2, axis=1, keepdims=True)
    i2 = jnp.min(jnp.where(lg2 == m2, lane, LANES), axis=1, keepdims=True)
    e2 = jnp.exp(m2 - m1)
    w1 = 1.0 / (1.0 + e2)
    sel = jnp.where((lane == i1) | (lane == i2), 1.0, 0.0)
    tm = sel.shape[0]
    lower = lax.broadcasted_iota(jnp.int32, (tm, tm), 0) >= lax.broadcasted_iota(jnp.int32, (tm, tm), 1)
    cum = jnp.dot(jnp.where(lower, 1.0, 0.0).astype(BF16), sel.astype(BF16), preferred_element_type=F32)
    rank = cum - sel + carry_ref[...]
    r1 = jnp.sum(jnp.where(lane == i1, rank, 0.0), axis=1, keepdims=True)
    r2 = jnp.sum(jnp.where(lane == i2, rank, 0.0), axis=1, keepdims=True)
    carry_ref[...] += jnp.sum(sel, axis=0, keepdims=True)
    count_ref[...] = carry_ref[...]
    cols = (i1.astype(F32), i2.astype(F32), r1, r2, w1, e2 * w1)
    table = jnp.zeros_like(logits)
    for k, col in enumerate(cols):
        table = jnp.where(lane == k, col, table)
    route_ref[...] = table


ROUTE_EXPERT, ROUTE_RANK, ROUTE_WEIGHT = 0, 2, 4


def _norm_mod(x, g, shift, scale, seq, router=None, *, tm=256):
    t, d = x.shape
    tm = min(tm, seq)
    rpb = seq // tm
    row = pl.BlockSpec((tm, d), lambda i: (i, 0))
    vec = pl.BlockSpec((1, d), lambda i: (0, 0))
    per_batch = pl.BlockSpec((1, 1, d), lambda i: (i // rpb, 0, 0))
    ins = [x, g.reshape(1, d), shift, scale]
    in_specs = [row, vec, per_batch, per_batch]
    out_shape = [jax.ShapeDtypeStruct((t, d), BF16)]
    out_specs = [row]
    n_experts = 0
    scratch = []
    if router is not None:
        n_experts = router.shape[1]
        pitch = _SegLayout(d).pitch
        out_shape = [jax.ShapeDtypeStruct((t * pitch, LANES), F32)]
        out_specs = [pl.BlockSpec((tm * pitch, LANES), lambda i: (i, 0))]
        ins.append(jnp.pad(router, ((0, 0), (0, LANES - n_experts))))
        in_specs.append(pl.BlockSpec((d, LANES), lambda i: (0, 0)))
        out_shape += [jax.ShapeDtypeStruct((t, LANES), F32), jax.ShapeDtypeStruct((1, LANES), F32)]
        out_specs += [pl.BlockSpec((tm, LANES), lambda i: (i, 0)), pl.BlockSpec((1, LANES), lambda i: (0, 0))]
        scratch = [pltpu.VMEM((1, LANES), F32)]
    res = pl.pallas_call(
        functools.partial(_norm_kernel, modulate=True, n_experts=n_experts),
        grid=(t // tm,), in_specs=in_specs, out_specs=out_specs, out_shape=out_shape, scratch_shapes=scratch,
        compiler_params=_cparams("arbitrary" if router is not None else "parallel"),
        name="norm_router" if router is not None else "norm_mod")(*ins)
    return res if router is not None else res[0]


def _final_norm(x, g, *, tm=256):
    t, d = x.shape
    tm = min(tm, t)
    row = pl.BlockSpec((tm, d), lambda i: (i, 0))
    return pl.pallas_call(
        functools.partial(_norm_kernel, modulate=False, n_experts=0),
        grid=(t // tm,), in_specs=[row, pl.BlockSpec((1, d), lambda i: (0, 0))], out_specs=row,
        out_shape=jax.ShapeDtypeStruct((t, d), F32), compiler_params=_cparams("parallel"),
        name="final_norm")(x, g.reshape(1, d))


def _conv_kernel(*refs, width, ts, tc, has_bias, has_mul):
    cur_ref, halo_ref, w_ref = refs[:3]
    rest = list(refs[3:])
    b_ref = rest.pop(0) if has_bias else None
    mul_ref = rest.pop(0) if has_mul else None
    out_ref, buf = rest
    first = pl.program_id(1) == 0
    buf[0:CONV_HALO, :] = jnp.where(first, 0.0, halo_ref[0])
    buf[CONV_HALO:CONV_HALO + ts, :] = cur_ref[0]
    rows = 32
    for r in range(0, ts, rows):
        acc = None
        for k in range(width):
            term = w_ref[k:k + 1, :] * buf[pl.ds(CONV_HALO + r - (width - 1) + k, rows), :]
            acc = term if acc is None else acc + term
        if has_bias:
            acc = acc + b_ref[...]
        if has_mul:
            acc = acc * mul_ref[0, r:r + rows, :].astype(F32)
        out_ref[0, r:r + rows, :] = acc.astype(out_ref.dtype)


def _causal_conv(x, conv_w, bias=None, mul=None, out_dtype=F32, *, ts=256, tc=512):
    b, s, ch = x.shape
    width = conv_w.shape[0]
    assert width - 1 <= CONV_HALO
    if width <= 4:
        tc *= 2
    ts, tc = min(ts, s), min(tc, ch)
    hpb = ts // CONV_HALO
    blk = pl.BlockSpec((1, ts, tc), lambda bi, si, ci: (bi, si, ci))
    ins = [x, x, conv_w]
    in_specs = [blk,
                pl.BlockSpec((1, CONV_HALO, tc), lambda bi, si, ci: (bi, jnp.maximum(si * hpb - 1, 0), ci)),
                pl.BlockSpec((width, tc), lambda bi, si, ci: (0, ci))]
    if bias is not None:
        ins.append(bias.reshape(1, ch))
        in_specs.append(pl.BlockSpec((1, tc), lambda bi, si, ci: (0, ci)))
    if mul is not None:
        ins.append(mul)
        in_specs.append(blk)
    return pl.pallas_call(
        functools.partial(_conv_kernel, width=width, ts=ts, tc=tc, has_bias=bias is not None,
                          has_mul=mul is not None),
        grid=(b, s // ts, ch // tc), in_specs=in_specs, out_specs=blk,
        out_shape=jax.ShapeDtypeStruct((b, s, ch), out_dtype),
        scratch_shapes=[pltpu.VMEM((CONV_HALO + ts, tc), F32)],
        compiler_params=_cparams("parallel", "parallel", "parallel"), name=f"causal_conv{width}")(*ins)


def _ln_silu_kernel(x_ref, g_ref, b_ref, out_ref):
    x = x_ref[...]
    mu = jnp.mean(x, axis=-1, keepdims=True)
    xc = x - mu
    var = jnp.mean(xc * xc, axis=-1, keepdims=True)
    y = xc * lax.rsqrt(var + NORM_EPS) * g_ref[...] + b_ref[...]
    out_ref[...] = _silu(y).astype(BF16)


def _ln_silu(x, g, b, *, tm=256):
    t, d = x.shape
    tm = min(tm, t)
    row = pl.BlockSpec((tm, d), lambda i: (i, 0))
    vec = pl.BlockSpec((1, d), lambda i: (0, 0))
    return pl.pallas_call(
        _ln_silu_kernel, grid=(t // tm,), in_specs=[row, vec, vec], out_specs=row,
        out_shape=jax.ShapeDtypeStruct((t, d), BF16), compiler_params=_cparams("parallel"), name="ln_silu")(
            x, g.reshape(1, d), b.reshape(1, d))


def _rglru_kernel(xc_ref, gate_ref, wa_ref, wx_ref, ba_ref, bx_ref, lam_ref, out_ref,
                  a_scr, u_scr, h_scr, *, ts, n_heads, blk, cw):
    @pl.when(pl.program_id(1) == 0)
    def _():
        h_scr[...] = jnp.zeros_like(h_scr)

    neg_lam = -lam_ref[...]
    softplus = jnp.maximum(neg_lam, 0.0) + jnp.log1p(jnp.exp(-jnp.abs(neg_lam)))
    for hd in range(n_heads):
        sl = slice(hd * blk, (hd + 1) * blk)
        xh = xc_ref[:, sl]
        xb = xh.astype(BF16)
        r = _sigmoid(jnp.dot(xb, wa_ref[hd], preferred_element_type=F32) + ba_ref[:, sl])
        i_g = _sigmoid(jnp.dot(xb, wx_ref[hd], preferred_element_type=F32) + bx_ref[:, sl])
        log_a = -LRU_C * r * softplus[:, sl]
        a_scr[:, sl] = jnp.exp(log_a)
        u_scr[:, sl] = jnp.sqrt(1.0 - jnp.exp(2.0 * log_a)) * (i_g * xh)

    row = lax.broadcasted_iota(jnp.int32, (SUBLANES, cw), 0)
    width = n_heads * blk
    for c0 in range(0, width, cw):
        def body(g, h_prev, c0=c0):
            r0 = pl.multiple_of(g * SUBLANES, SUBLANES)
            a8 = a_scr[pl.ds(r0, SUBLANES), c0:c0 + cw]
            b8 = u_scr[pl.ds(r0, SUBLANES), c0:c0 + cw]
            for d in (1, 2, 4):
                keep = row >= d
                b8 = jnp.where(keep, a8 * pltpu.roll(b8, d, axis=0) + b8, b8)
                a8 = jnp.where(keep, a8 * pltpu.roll(a8, d, axis=0), a8)
            h8 = b8 + a8 * h_prev
            u_scr[pl.ds(r0, SUBLANES), c0:c0 + cw] = h8
            return h8[SUBLANES - 1:SUBLANES, :]

        h_scr[:, c0:c0 + cw] = lax.fori_loop(0, ts // SUBLANES, body, h_scr[:, c0:c0 + cw])
    out_ref[...] = (u_scr[...] * gate_ref[...].astype(F32)).astype(BF16)


def _rglru(xc, gate, w_a, w_x, b_a, b_x, lam, seq, *, ts=256):
    t, width = xc.shape
    n_heads, blk, _ = w_a.shape
    ts = min(ts, seq)
    nts = seq // ts
    row = pl.BlockSpec((ts, width), lambda b, j: (b * nts + j, 0))
    wsp = pl.BlockSpec((n_heads, blk, blk), lambda b, j: (0, 0, 0))
    vec = pl.BlockSpec((1, width), lambda b, j: (0, 0))
    return pl.pallas_call(
        functools.partial(_rglru_kernel, ts=ts, n_heads=n_heads, blk=blk, cw=min(1024, width)),
        grid=(t // seq, nts), in_specs=[row, row, wsp, wsp, vec, vec, vec], out_specs=row,
        out_shape=jax.ShapeDtypeStruct((t, width), BF16),
        scratch_shapes=[pltpu.VMEM((ts, width), F32), pltpu.VMEM((ts, width), F32), pltpu.VMEM((1, width), F32)],
        compiler_params=_cparams("parallel", "arbitrary"), name="rglru")(
            xc, gate, w_a, w_x, b_a.reshape(1, width), b_x.reshape(1, width), lam.reshape(1, width))


def _bias_tile_kernel(rb_ref, out_ref):
    hd = pl.program_id(0)
    shape = (2 * Q_BLOCK, Q_BLOCK)
    dist = Q_BLOCK + lax.broadcasted_iota(jnp.int32, shape, 1) - lax.broadcasted_iota(jnp.int32, shape, 0)
    dist = jnp.maximum(dist, 0)
    max_exact = REL_BUCKETS // 2
    large = max_exact + (jnp.log(jnp.maximum(dist, 1).astype(F32) / max_exact)
                         / math.log(REL_MAX_DIST / max_exact) * (REL_BUCKETS - max_exact)).astype(jnp.int32)
    bucket = jnp.where(dist < max_exact, dist, jnp.minimum(large, REL_BUCKETS - 1))
    tile = jnp.zeros(shape, F32)
    for bkt in range(REL_BUCKETS):
        tile = jnp.where(bucket == bkt, rb_ref[bkt, hd], tile)
    out_ref[0] = tile - rb_ref[REL_BUCKETS - 1, hd]


def _bias_tile(rel_bias):
    n_heads = rel_bias.shape[1]
    return pl.pallas_call(
        _bias_tile_kernel, grid=(n_heads,),
        in_specs=[pl.BlockSpec(memory_space=pltpu.SMEM)],
        out_specs=pl.BlockSpec((1, 2 * Q_BLOCK, Q_BLOCK), lambda h: (h, 0, 0)),
        out_shape=jax.ShapeDtypeStruct((n_heads, 2 * Q_BLOCK, Q_BLOCK), F32),
        compiler_params=_cparams("parallel"), name="dsa_bias_tile")(rel_bias)


def _dsa_kernel(qidx_ref, wrow_ref, kidx_ref, ckv_ref, ckvt_ref, qlat_ref, bias_ref, wuv_ref, out_ref,
                score_ref, key_ref, madd_ref, z_ref, acc_ref, *, n_slab, k_sel, idx_heads, kv_dim, head_dim):
    qb = pl.program_id(1)
    cols = HEAD_GROUP * Q_BLOCK
    sub = KEY_CHUNK // SUBLANES
    nt = (((1,), (1,)), ((), ()))
    n_chunks = (qb + 2) // 2

    @pl.when(pl.program_id(2) == 0)
    def _select():
        def chunk_body(c, carry):
            k0 = pl.multiple_of(c * KEY_CHUNK, KEY_CHUNK)
            keys = kidx_ref[0, pl.ds(k0, KEY_CHUNK), :]

            def head_body(hg, acc):
                h0 = pl.multiple_of(hg * HEAD_GROUP, HEAD_GROUP)
                q = qidx_ref[0, pl.ds(h0, HEAD_GROUP)].reshape(cols, LANES)
                dots = lax.dot_general(keys, q, nt, preferred_element_type=F32)
                weighted = jnp.maximum(dots, 0.0) * wrow_ref[0, hg]
                for hh in range(HEAD_GROUP):
                    acc = acc + weighted[:, hh * Q_BLOCK:(hh + 1) * Q_BLOCK]
                return acc

            score_ref[pl.ds(k0, KEY_CHUNK), :] = lax.fori_loop(
                0, idx_heads // HEAD_GROUP, head_body, jnp.zeros((KEY_CHUNK, Q_BLOCK), F32), unroll=True)
            return carry

        lax.fori_loop(0, n_chunks, chunk_body, 0)

        def zero_body(c, carry):
            score_ref[pl.ds(pl.multiple_of(c * KEY_CHUNK, KEY_CHUNK), KEY_CHUNK), :] = jnp.zeros(
                (KEY_CHUNK, Q_BLOCK), F32)
            return carry

        lax.fori_loop(n_chunks, n_slab // 2, zero_body, 0)

        shape = (n_slab * LANES, Q_BLOCK)
        bits = pltpu.bitcast(score_ref[...], jnp.int32)
        key = jnp.where(bits >= 0, bits, bits ^ jnp.int32(0x7FFFFFFF))
        causal = lax.broadcasted_iota(jnp.int32, shape, 0) <= qb * Q_BLOCK + lax.broadcasted_iota(jnp.int32, shape, 1)
        key = jnp.where(causal, key, INT_MIN)
        key_ref[...] = key

        def bit_body(it, thr):
            cand = thr + lax.shift_left(jnp.int32(1), 31 - it)
            cand_b = jnp.broadcast_to(cand, (SUBLANES, Q_BLOCK))[None, None]
            lanes_of_sum = (SUBLANES, SUBLANES, Q_BLOCK)

            def count_body(c, cnt):
                kc = key_ref[pl.ds(pl.multiple_of(c * 2 * KEY_CHUNK, 2 * KEY_CHUNK), 2 * KEY_CHUNK), :]
                hit = jnp.where(kc.reshape((2 * sub // SUBLANES,) + lanes_of_sum) >= cand_b, 1.0, 0.0)
                return cnt + jnp.sum(hit, axis=0)

            cnt = lax.fori_loop(0, (n_chunks + 1) // 2, count_body, jnp.zeros(lanes_of_sum, F32))
            cnt = jnp.sum(jnp.sum(cnt, axis=0), axis=0, keepdims=True)
            return jnp.where(cnt >= k_sel, cand, thr)

        thr = lax.fori_loop(0, 32, bit_body, jnp.full((1, Q_BLOCK), INT_MIN, jnp.int32))
        madd_ref[0:Q_BLOCK, :] = jnp.full((Q_BLOCK, Q_BLOCK), MASK_NEG, F32)
        madd_ref[Q_BLOCK:, :] = jnp.where(causal & (key >= thr), 0.0, MASK_NEG)

    q = qlat_ref[0].reshape(cols, kv_dim)
    n_far = qb // 2
    near_row = qb * Q_BLOCK

    def logits_of(kv):
        return lax.dot_general(kv, q, nt, preferred_element_type=F32)

    def per_head(m):
        return jnp.concatenate([m] * HEAD_GROUP, axis=1)

    def col_max(z):
        return jnp.max(z.reshape(sub, SUBLANES, cols), axis=0)

    def far_chunk(c, m8):
        r0 = pl.multiple_of(Q_BLOCK + c * KEY_CHUNK, Q_BLOCK)
        kv = ckv_ref[0, pl.ds(r0, KEY_CHUNK), :]
        row = r0 + lax.broadcasted_iota(jnp.int32, (KEY_CHUNK, Q_BLOCK), 0)
        madd = jnp.where(row < near_row, madd_ref[pl.ds(r0, KEY_CHUNK), :], MASK_NEG)
        z = logits_of(kv) + per_head(madd)
        z_ref[c] = z
        return jnp.maximum(m8, col_max(z))

    def in_pairs(n, one, init):
        carry = lax.fori_loop(0, n // 2, lambda pair, v: one(2 * pair + 1, one(2 * pair, v)), init)
        return lax.cond(n % 2 == 1, lambda v: one(n - 1, v), lambda v: v, carry)

    m8 = in_pairs(n_far, far_chunk, jnp.full((SUBLANES, cols), MASK_NEG, F32))
    r0 = pl.multiple_of(near_row, Q_BLOCK)
    bias = jnp.concatenate([bias_ref[hh] for hh in range(HEAD_GROUP)], axis=1)
    z = (logits_of(ckv_ref[0, pl.ds(r0, KEY_CHUNK), :]) + per_head(madd_ref[pl.ds(r0, KEY_CHUNK), :])) + bias
    z_ref[n_far] = z
    m8 = jnp.maximum(m8, col_max(z))
    m8 = jnp.broadcast_to(jnp.max(m8, axis=0, keepdims=True), (SUBLANES, cols))[None]
    acc_ref[...] = jnp.zeros(acc_ref.shape, F32)

    def weigh(c, l8):
        s0 = jnp.where(c < n_far, 2 * c + 1, qb)
        kvt = jnp.concatenate([ckvt_ref[0, s0], ckvt_ref[0, s0 + 1]], axis=1)
        p = jnp.exp(z_ref[c].reshape(sub, SUBLANES, cols) - m8)
        acc_ref[...] += jnp.dot(kvt, p.reshape(KEY_CHUNK, cols).astype(BF16), preferred_element_type=F32)
        return l8 + jnp.sum(p, axis=0)

    l8 = in_pairs(n_far + 1, weigh, jnp.zeros((SUBLANES, cols), F32))
    inv = 1.0 / jnp.broadcast_to(jnp.sum(l8, axis=0, keepdims=True), (SUBLANES, cols))
    o_lat = (acc_ref[...].reshape(kv_dim // SUBLANES, SUBLANES, cols) * inv[None]).reshape(kv_dim, cols)
    o_lat = o_lat.astype(BF16)
    for hh in range(HEAD_GROUP):
        o_h = jnp.dot(wuv_ref[hh], o_lat[:, hh * Q_BLOCK:(hh + 1) * Q_BLOCK], preferred_element_type=F32)
        out_ref[:, hh * head_dim:(hh + 1) * head_dim] = o_h.T.astype(BF16)


def _dsa_core(qidx, wrow, kidx, ckv_pad, ckvt_pad, qlat, bias, w_uvt, *, bsz, seq, k_sel):
    n_qb = seq // Q_BLOCK
    idx_heads = qidx.shape[1]
    n_heads, kv_dim = qlat.shape[1], qlat.shape[3]
    head_dim = w_uvt.shape[1]
    n_groups = n_heads // HEAD_GROUP
    cols = HEAD_GROUP * Q_BLOCK
    blk_q = lambda b, i, g: (b * n_qb + i, 0, 0, 0)
    per_batch = lambda b, i, g: (b, 0, 0)
    return pl.pallas_call(
        functools.partial(_dsa_kernel, n_slab=n_qb, k_sel=k_sel, idx_heads=idx_heads, kv_dim=kv_dim,
                          head_dim=head_dim),
        grid=(bsz, n_qb, n_groups),
        in_specs=[
            pl.BlockSpec((1, idx_heads, Q_BLOCK, LANES), blk_q),
            pl.BlockSpec((1, idx_heads // HEAD_GROUP, 1, cols), blk_q),
            pl.BlockSpec((1, seq, LANES), per_batch),
            pl.BlockSpec((1, seq + Q_BLOCK, kv_dim), per_batch),
            pl.BlockSpec((1, n_qb + 1, kv_dim, Q_BLOCK), lambda b, i, g: (b, 0, 0, 0)),
            pl.BlockSpec((1, HEAD_GROUP, Q_BLOCK, kv_dim), lambda b, i, g: (b * n_qb + i, g, 0, 0)),
            pl.BlockSpec((HEAD_GROUP, KEY_CHUNK, Q_BLOCK), lambda b, i, g: (g, 0, 0)),
            pl.BlockSpec((HEAD_GROUP, head_dim, kv_dim), lambda b, i, g: (g, 0, 0)),
        ],
        out_specs=pl.BlockSpec((Q_BLOCK, HEAD_GROUP * head_dim), lambda b, i, g: (b * n_qb + i, g)),
        out_shape=jax.ShapeDtypeStruct((bsz * seq, n_heads * head_dim), BF16),
        scratch_shapes=[
            pltpu.VMEM((seq, Q_BLOCK), F32),
            pltpu.VMEM((seq, Q_BLOCK), jnp.int32),
            pltpu.VMEM((seq + Q_BLOCK, Q_BLOCK), F32),
            pltpu.VMEM((n_qb // 2 + 1, KEY_CHUNK, cols), F32),
            pltpu.VMEM((kv_dim, cols), F32),
        ],
        compiler_params=_cparams("parallel", "parallel", "arbitrary"), name="dsa_core")(
            qidx, wrow, kidx, ckv_pad, ckvt_pad, qlat, bias, w_uvt)


def _dsa_attention(h, x, gate, rel_bias, w_in, g_cq, g_ckv, w_uq, w_uk, w_uv, w_qidx, w_out, *, bsz, seq):
    t, d = h.shape
    q_lora, kv_lora = g_cq.shape[0], g_ckv.shape[0]
    n_heads, head_dim = w_uk.shape[1], w_uk.shape[2]
    idx_dim = LANES
    idx_heads = w_qidx.shape[1] // idx_dim
    k_sel = min(TOPK_MAX, seq // 4)
    n_qb = seq // Q_BLOCK
    tm = min(512, seq)
    qpt = tm // Q_BLOCK

    w_in = w_in.astype(BF16)
    splits = (0, q_lora, q_lora + kv_lora, q_lora + kv_lora + idx_dim, w_in.shape[1])
    w_parts = [w_in[:, splits[i]:splits[i + 1]] for i in range(4)]
    head_scale = idx_heads ** -0.5 * idx_dim ** -0.5

    def in_epilogue(accs, ex, outs, ids):
        outs[0][...] = _rms(accs[0], ex[0][...]).astype(BF16)
        outs[1][...] = _rms(accs[1], ex[1][...]).astype(BF16)
        outs[2][...] = accs[2].astype(BF16)
        outs[3][...] = accs[3] * head_scale

    full = lambda n: pl.BlockSpec((d, n), lambda i, j: (0, 0))
    rowsp = lambda n: pl.BlockSpec((tm, n), lambda i, j: (i, 0))
    vecsp = lambda n: pl.BlockSpec((1, n), lambda i, j: (0, 0))
    widths = [w.shape[1] for w in w_parts]
    c_q, c_kv, k_idx, w_head = _mm(
        h, pl.BlockSpec((tm, d), lambda i, j: (i, 0)),
        [(w, full(n)) for w, n in zip(w_parts, widths)],
        [(g_cq.reshape(1, q_lora), vecsp(q_lora)), (g_ckv.reshape(1, kv_lora), vecsp(kv_lora))],
        [(jax.ShapeDtypeStruct((t, n), dt), rowsp(n)) for n, dt in zip(widths, (BF16, BF16, BF16, F32))],
        in_epilogue, grid=(t // tm, 1), name="dsa_in_proj")

    hpt = 4
    tn = hpt * idx_dim
    tm_q = min(2048, seq)
    qpt_q = tm_q // Q_BLOCK

    def qidx_epilogue(accs, ex, outs, ids):
        for hh in range(hpt):
            q_h = accs[0][:, hh * idx_dim:(hh + 1) * idx_dim]
            outs[0][:, hh] = q_h.reshape(qpt_q, Q_BLOCK, idx_dim).astype(BF16)

    qidx = _mm(
        c_q, pl.BlockSpec((tm_q, q_lora), lambda i, j: (i, 0)),
        [(w_qidx, pl.BlockSpec((q_lora, tn), lambda i, j: (0, j)))], [],
        [(jax.ShapeDtypeStruct((t // Q_BLOCK, idx_heads, Q_BLOCK, idx_dim), BF16),
          pl.BlockSpec((qpt_q, hpt, Q_BLOCK, idx_dim), lambda i, j: (i, j, 0, 0)))],
        qidx_epilogue, grid=(t // tm_q, idx_heads // hpt), name="dsa_qidx", cast_w=True)[0]

    w_ukt = jnp.transpose(w_uk, (1, 2, 0)).astype(BF16)
    logit_scale = head_dim ** -0.5

    def qlat_kernel(cq_ref, wuq_ref, wuk_ref, out_ref):
        qg = jnp.dot(cq_ref[...], wuq_ref[...], preferred_element_type=F32).astype(BF16)
        for hh in range(HEAD_GROUP):
            ql = jnp.dot(qg[:, hh * head_dim:(hh + 1) * head_dim], wuk_ref[hh], preferred_element_type=F32)
            out_ref[:, hh] = (ql * logit_scale).reshape(qpt, Q_BLOCK, kv_lora).astype(BF16)

    qlat = pl.pallas_call(
        qlat_kernel, grid=(t // tm, n_heads // HEAD_GROUP),
        in_specs=[pl.BlockSpec((tm, q_lora), lambda i, g: (i, 0)),
                  pl.BlockSpec((q_lora, HEAD_GROUP * head_dim), lambda i, g: (0, g)),
                  pl.BlockSpec((HEAD_GROUP, head_dim, kv_lora), lambda i, g: (g, 0, 0))],
        out_specs=pl.BlockSpec((qpt, HEAD_GROUP, Q_BLOCK, kv_lora), lambda i, g: (i, g, 0, 0)),
        out_shape=jax.ShapeDtypeStruct((t // Q_BLOCK, n_heads, Q_BLOCK, kv_lora), BF16),
        compiler_params=_cparams("parallel", "parallel"), name="dsa_qlat")(c_q, w_uq.astype(BF16), w_ukt)

    wrow = w_head.reshape(bsz * n_qb, Q_BLOCK, idx_heads // HEAD_GROUP, HEAD_GROUP).transpose(0, 2, 3, 1)
    wrow = wrow.reshape(bsz * n_qb, idx_heads // HEAD_GROUP, 1, HEAD_GROUP * Q_BLOCK)
    ckv_pad = jnp.pad(c_kv.reshape(bsz, seq, kv_lora), ((0, 0), (Q_BLOCK, 0), (0, 0)))
    ckvt_pad = ckv_pad.reshape(bsz, n_qb + 1, Q_BLOCK, kv_lora).transpose(0, 1, 3, 2)
    w_uvt = jnp.transpose(w_uv, (1, 2, 0)).astype(BF16)
    o = _dsa_core(qidx, wrow, k_idx.reshape(bsz, seq, idx_dim), ckv_pad, ckvt_pad, qlat,
                  _bias_tile(rel_bias.astype(F32)), w_uvt, bsz=bsz, seq=seq, k_sel=k_sel)
    return _matmul_resid(o, w_out, x, gate, seq, name="dsa_out_proj")


def _split_proj(h, w, n_parts, epilogue, out_dtypes, *, name, tm=512, tn=512):
    t, d = h.shape
    n = w.shape[1] // n_parts
    tm, tn = min(tm, t), min(tn, n)
    nb = n // tn
    out_sp = pl.BlockSpec((tm, tn), lambda i, j: (i, j))
    return _mm(
        h, pl.BlockSpec((tm, d), lambda i, j: (i, 0)),
        [(w, pl.BlockSpec((d, tn), lambda i, j, p=p: (0, j + p * nb))) for p in range(n_parts)], [],
        [(jax.ShapeDtypeStruct((t, n), dt), out_sp) for dt in out_dtypes],
        epilogue, grid=(t // tm, nb), name=name, cast_w=True)


def _short_conv_mixer(h, x, gate, w_in, conv_w, w_out, *, bsz, seq):
    def epilogue(accs, ex, outs, ids):
        outs[0][...] = accs[0].astype(BF16)
        outs[1][...] = accs[1] * accs[2]

    gate_b, cx = _split_proj(h, w_in, 3, epilogue, (BF16, F32), name="sconv_in_proj", tn=256)
    ch = cx.shape[1]
    y = _causal_conv(cx.reshape(bsz, seq, ch), conv_w, mul=gate_b.reshape(bsz, seq, ch), out_dtype=BF16)
    return _matmul_resid(y.reshape(bsz * seq, ch), w_out, x, gate, seq, name="sconv_out_proj")


def _conformer_mixer(h, x, gate, w_in, conv_w, conv_b, ln_g, ln_b, w_out, *, bsz, seq):
    def epilogue(accs, ex, outs, ids):
        outs[0][...] = accs[0] * _sigmoid(accs[1])

    u = _split_proj(h, w_in, 2, epilogue, (F32,), name="conf_in_proj")[0]
    ch = u.shape[1]
    u = _causal_conv(u.reshape(bsz, seq, ch), conv_w, bias=conv_b)
    u = _ln_silu(u.reshape(bsz * seq, ch), ln_g, ln_b)
    return _matmul_resid(u, w_out, x, gate, seq, name="conf_out_proj")


def _rglru_mixer(h, x, gate, w_in, conv_w, conv_b, w_a, b_a, w_x, b_x, lam, w_out, *, bsz, seq):
    def epilogue(accs, ex, outs, ids):
        outs[0][...] = _gelu_tanh(accs[0]).astype(BF16)
        outs[1][...] = accs[1]

    gate_br, x_br = _split_proj(h, w_in, 2, epilogue, (BF16, F32), name="lru_in_proj")
    width = x_br.shape[1]
    xc = _causal_conv(x_br.reshape(bsz, seq, width), conv_w, bias=conv_b)
    y = _rglru(xc.reshape(bsz * seq, width), gate_br, w_a.astype(BF16), w_x.astype(BF16), b_a, b_x, lam, seq)
    return _matmul_resid(y, w_out, x, gate, seq, name="lru_out_proj")


def _dense_ffn(h, x, gate, w13, w2, layer, *, seq):
    act = _swiglu_act(h, w13, layer)
    return _matmul_resid(act, w2, x, gate, seq, name="ffn_down", layer=layer, tm=512, tk=w2.shape[1])


def _moe_ffn(h_seg, route, counts, x, gate, w13, w2, layer, *, seq):
    t = x.shape[0]
    n_e = w13.shape[1]
    tile = MOE_ROW_TILE
    n_rows = MOE_TOPK * t + n_e * tile
    counts = counts[0, :n_e].astype(jnp.int32)
    padded = (counts + tile - 1) // tile * tile
    ends = jnp.cumsum(padded)
    starts = ends - padded
    experts = route[:, ROUTE_EXPERT:ROUTE_EXPERT + MOE_TOPK].astype(jnp.int32)
    ranks = route[:, ROUTE_RANK:ROUTE_RANK + MOE_TOPK].astype(jnp.int32)
    pos = (starts[experts] + ranks).T.reshape(MOE_TOPK * t)
    tile_start = jnp.arange(n_rows // tile, dtype=jnp.int32) * tile
    tile_expert = jnp.minimum(jnp.sum(tile_start[:, None] >= ends[None, :], axis=1), n_e - 1).astype(jnp.int32)
    n_used = (ends[-1:] // tile).astype(jnp.int32)

    lay = _SegLayout(x.shape[1])
    hs = _moe_dispatch(h_seg, pos, n_rows, lay)
    act = _grouped_swiglu(hs, w13, layer, tile_expert, n_used, lay)
    y = _grouped_down(act, w2, layer, tile_expert, n_used, lay)
    return _moe_combine(y, pos, x, route, gate, seq, lay)


def kernel(x, c, ada_w, ada_b, ada_table, norm_mix, norm_ffn, norm_final, rel_bias, att_w_in, att_g_cq, att_g_ckv, att_w_uq, att_w_uk, att_w_uv, att_w_qidx, att_w_out, sconv_w_in, sconv_conv_w, sconv_w_out, conf_w_in, conf_conv_w, conf_conv_b, conf_ln_g, conf_ln_b, conf_w_out, lru_w_in, lru_conv_w, lru_conv_b, lru_w_a, lru_b_a, lru_w_x, lru_b_x, lru_lambda, lru_w_out, ffn_w13, ffn_w2, moe_router, moe_w13, moe_w2):
    bsz, seq, d = x.shape
    depth = ada_table.shape[0]
    xf = x.reshape(bsz * seq, d)
    mods = _ada_mod(c, ada_w, ada_b, ada_table)
    ffn_w2_b, moe_w13_b, moe_w2_b = ffn_w2.astype(BF16), moe_w13.astype(BF16), moe_w2.astype(BF16)
    for i in range(depth):
        shift_m, scale_m, gate_m, shift_f, scale_f, gate_f = (mods[i][:, k:k + 1, :] for k in range(N_ADA))
        h = _norm_mod(xf, norm_mix[i], shift_m, scale_m, seq)
        kind, j = i % 4, i // 4
        if kind == 0:
            xf = _dsa_attention(h, xf, gate_m, rel_bias, att_w_in[j], att_g_cq[j], att_g_ckv[j], att_w_uq[j],
                                att_w_uk[j], att_w_uv[j], att_w_qidx[j], att_w_out[j], bsz=bsz, seq=seq)
        elif kind == 1:
            xf = _short_conv_mixer(h, xf, gate_m, sconv_w_in[j], sconv_conv_w[j], sconv_w_out[j], bsz=bsz, seq=seq)
        elif kind == 2:
            xf = _conformer_mixer(h, xf, gate_m, conf_w_in[j], conf_conv_w[j], conf_conv_b[j], conf_ln_g[j],
                                  conf_ln_b[j], conf_w_out[j], bsz=bsz, seq=seq)
        else:
            xf = _rglru_mixer(h, xf, gate_m, lru_w_in[j], lru_conv_w[j], lru_conv_b[j], lru_w_a[j], lru_b_a[j],
                              lru_w_x[j], lru_b_x[j], lru_lambda[j], lru_w_out[j], bsz=bsz, seq=seq)
        if i % 2 == 0:
            h = _norm_mod(xf, norm_ffn[i], shift_f, scale_f, seq)
            xf = _dense_ffn(h, xf, gate_f, ffn_w13, ffn_w2_b, i // 2, seq=seq)
        else:
            h, route, counts = _norm_mod(xf, norm_ffn[i], shift_f, scale_f, seq, router=moe_router[i // 2])
            xf = _moe_ffn(h, route, counts, xf, gate_f, moe_w13_b, moe_w2_b, i // 2, seq=seq)
    return _final_norm(xf, norm_final).reshape(bsz, seq, d)
```
